```python
import jax, jax.numpy as jnp
from jax import lax
import numpy as np

D_MODEL = 2048
BATCH = 8
SEQ = 4096
DEPTH = 2

CHUNK = 64
N_HEADS = 16
HEAD_DIM = D_MODEL // N_HEADS
D_FF = 4 * D_MODEL
Q_BLOCK = 128
N_PREV_CHUNKS = 8
REL_CLIP = 256
N_REL = REL_CLIP + CHUNK
N_A = DEPTH // 2
N_B = DEPTH - N_A
EPS = 1e-6
FGATE_BIAS = 3.0

kernel_name = "fox_yoco_chunked_relbias_hybrid"


def rms_norm(x, g):
    xf = x.astype(jnp.float32)
    y = xf * lax.rsqrt(jnp.mean(xf * xf, axis=-1, keepdims=True) + EPS)
    return (y * g.astype(jnp.float32)).astype(x.dtype)


def sq_relu_mlp(h, g, w1, w2):
    a = jax.nn.relu(rms_norm(h, g) @ w1)
    return (a * a) @ w2


def forgetting_attention(q, k, v, logf):
    S = q.shape[1]
    scale = HEAD_DIM ** -0.5
    c = jnp.transpose(jnp.cumsum(logf, axis=1), (0, 2, 1))
    outs = []
    for i in range(S // Q_BLOCK):
        q0, q1 = i * Q_BLOCK, (i + 1) * Q_BLOCK
        qb = q[:, q0:q1]
        kb, vb = k[:, :q1], v[:, :q1]
        s = jnp.einsum('bqhd,bkhd->bhqk', qb, kb).astype(jnp.float32) * scale
        s = s + c[:, :, q0:q1, None] - c[:, :, None, :q1]
        causal = (q0 + jnp.arange(Q_BLOCK))[:, None] >= jnp.arange(q1)[None, :]
        s = jnp.where(causal[None, None], s, -jnp.inf)
        p = jax.nn.softmax(s, axis=-1).astype(vb.dtype)
        outs.append(jnp.einsum('bhqk,bkhd->bqhd', p, vb))
    return jnp.concatenate(outs, axis=1)


def chunked_relbias_attention(q, k, v, rel_table):
    B, S, H, Dh = q.shape
    n_chunks = S // CHUNK
    pad = N_PREV_CHUNKS * CHUNK
    band = pad + CHUNK
    scale = HEAD_DIM ** -0.5
    kp = jnp.pad(k, ((0, 0), (pad, 0), (0, 0), (0, 0)))
    vp = jnp.pad(v, ((0, 0), (pad, 0), (0, 0), (0, 0)))
    qi = jnp.arange(CHUNK)[:, None]
    km = jnp.arange(band)[None, :]
    dist = pad + qi - km
    idx = jnp.clip(dist, -(CHUNK - 1), REL_CLIP) + (CHUNK - 1)
    bias = rel_table[:, idx].astype(jnp.float32)
    qc = q.reshape(B, n_chunks, CHUNK, H, Dh)

    def one_chunk(ci):
        qb = lax.dynamic_index_in_dim(qc, ci, axis=1, keepdims=False)
        kb = lax.dynamic_slice_in_dim(kp, ci * CHUNK, band, axis=1)
        vb = lax.dynamic_slice_in_dim(vp, ci * CHUNK, band, axis=1)
        s = jnp.einsum('bqhd,bkhd->bhqk', qb, kb).astype(jnp.float32) * scale + bias[None]
        valid = km >= pad - ci * CHUNK
        s = jnp.where(valid[None, None], s, -jnp.inf)
        p = jax.nn.softmax(s, axis=-1).astype(vb.dtype)
        return jnp.einsum('bhqk,bkhd->bqhd', p, vb)

    out = lax.map(one_chunk, jnp.arange(n_chunks))
    return jnp.transpose(out, (1, 0, 2, 3, 4)).reshape(B, S, H, Dh)


def setup_inputs(seed: int = 0) -> dict:
    key = jax.random.key(seed)
    ks = jax.random.split(key, 20)
    D, H, Dh = D_MODEL, N_HEADS, HEAD_DIM
    nrm = jax.random.normal
    f32 = jnp.float32
    return {
        "x": nrm(ks[0], (BATCH, SEQ, D), f32),
        "a_norm_g": 1.0 + 0.02 * nrm(ks[1], (N_A, D), f32),
        "a_w_in": nrm(ks[2], (N_A, D, 3 * D + H), f32) * D ** -0.5,
        "a_b_f": FGATE_BIAS + 0.5 * nrm(ks[3], (N_A, H), f32),
        "a_q_g": 1.0 + 0.02 * nrm(ks[4], (N_A, Dh), f32),
        "a_k_g": 1.0 + 0.02 * nrm(ks[5], (N_A, Dh), f32),
        "a_w_out": nrm(ks[6], (N_A, D, D), f32) * D ** -0.5,
        "mlp_norm_g": 1.0 + 0.02 * nrm(ks[7], (DEPTH, D), f32),
        "mlp_w1": nrm(ks[8], (DEPTH, D, D_FF), f32) * D ** -0.5,
        "mlp_w2": nrm(ks[9], (DEPTH, D_FF, D), f32) * D_FF ** -0.5,
        "kv_norm_g": 1.0 + 0.02 * nrm(ks[10], (D,), f32),
        "kv_w": nrm(ks[11], (D, 2 * D), f32) * D ** -0.5,
        "kv_k_g": 1.0 + 0.02 * nrm(ks[12], (Dh,), f32),
        "b_norm_g": 1.0 + 0.02 * nrm(ks[13], (N_B, D), f32),
        "b_w_q": nrm(ks[14], (N_B, D, D), f32) * D ** -0.5,
        "b_q_g": 1.0 + 0.02 * nrm(ks[15], (N_B, Dh), f32),
        "b_rel": 0.5 * nrm(ks[16], (N_B, H, N_REL), f32),
        "b_w_out": nrm(ks[17], (N_B, D, D), f32) * D ** -0.5,
    }


def reference(x, a_norm_g, a_w_in, a_b_f, a_q_g, a_k_g, a_w_out,
              mlp_norm_g, mlp_w1, mlp_w2,
              kv_norm_g, kv_w, kv_k_g,
              b_norm_g, b_w_q, b_q_g, b_rel, b_w_out):
    B, S, D = x.shape
    H, Dh = N_HEADS, HEAD_DIM
    h = x
    layer = 0
    for l in range(N_A):
        u = rms_norm(h, a_norm_g[l])
        proj = u @ a_w_in[l]
        q, k, v, fz = jnp.split(proj, [D, 2 * D, 3 * D], axis=-1)
        q = rms_norm(q.reshape(B, S, H, Dh), a_q_g[l])
        k = rms_norm(k.reshape(B, S, H, Dh), a_k_g[l])
        v = v.reshape(B, S, H, Dh)
        logf = jax.nn.log_sigmoid(fz.astype(jnp.float32) + a_b_f[l].astype(jnp.float32))
        o = forgetting_attention(q, k, v, logf)
        h = h + o.reshape(B, S, D) @ a_w_out[l]
        h = h + sq_relu_mlp(h, mlp_norm_g[layer], mlp_w1[layer], mlp_w2[layer])
        layer += 1
    kv = rms_norm(h, kv_norm_g) @ kv_w
    k_sh, v_sh = jnp.split(kv, [D], axis=-1)
    k_sh = rms_norm(k_sh.reshape(B, S, H, Dh), kv_k_g)
    v_sh = v_sh.reshape(B, S, H, Dh)
    for l in range(N_B):
        u = rms_norm(h, b_norm_g[l])
        q = rms_norm((u @ b_w_q[l]).reshape(B, S, H, Dh), b_q_g[l])
        o = chunked_relbias_attention(q, k_sh, v_sh, b_rel[l])
        h = h + o.reshape(B, S, D) @ b_w_out[l]
        h = h + sq_relu_mlp(h, mlp_norm_g[layer], mlp_w1[layer], mlp_w2[layer])
        layer += 1
    return h
```

```python
import functools

import jax
import jax.numpy as jnp
from jax import lax
from jax.experimental import pallas as pl
from jax.experimental.pallas import tpu as pltpu

N_HEADS = 16
HEAD_DIM = 128
CHUNK = 64
N_PREV_CHUNKS = 8
REL_CLIP = 256
EPS = 1e-6

LANES = 128
VMEM_LIMIT = 56 * 1024 * 1024

_BAND = (N_PREV_CHUNKS + 1) * CHUNK
_PAD = N_PREV_CHUNKS * CHUNK
_GROUP = 4 * CHUNK
_WIN = _PAD + _GROUP


def _params(sem):
    return pltpu.CompilerParams(dimension_semantics=sem, vmem_limit_bytes=VMEM_LIMIT)


def _proj_kernel(*refs, n_q, n_k, has_f, tn):
    if has_f:
        x_ref, g_ref, w_ref, gq_ref, gk_ref, wf_ref, o_ref, f_ref, u_ref = refs
    else:
        x_ref, g_ref, w_ref, gq_ref, gk_ref, o_ref, u_ref = refs
    j = pl.program_id(1)

    @pl.when(j == 0)
    def _():
        x = x_ref[...]
        r = lax.rsqrt(jnp.mean(x * x, axis=-1, keepdims=True) + EPS)
        u_ref[...] = (x * r * g_ref[...]).astype(u_ref.dtype)
        if has_f:
            f_ref[...] = jnp.dot(u_ref[...], wf_ref[...],
                                 preferred_element_type=jnp.float32)

    y = jnp.dot(u_ref[...], w_ref[...], preferred_element_type=jnp.float32)

    def head_norm(gain_ref):
        for c in range(tn // HEAD_DIM):
            sl = slice(c * HEAD_DIM, (c + 1) * HEAD_DIM)
            ys = y[:, sl]
            r = lax.rsqrt(jnp.mean(ys * ys, axis=-1, keepdims=True) + EPS)
            o_ref[:, sl] = (ys * r * gain_ref[...]).astype(o_ref.dtype)

    if n_q > 0:
        @pl.when(j < n_q)
        def _():
            head_norm(gq_ref)
    if n_k > 0:
        @pl.when((j >= n_q) & (j < n_q + n_k))
        def _():
            head_norm(gk_ref)

    @pl.when(j >= n_q + n_k)
    def _():
        o_ref[...] = y.astype(o_ref.dtype)


def _proj(x, g, w, gq, gk, wf=None, *, n_q_cols, n_k_cols, tm=512, tn=1024):
    n, d = x.shape
    nout = w.shape[1]
    has_f = wf is not None
    in_specs = [
        pl.BlockSpec((tm, d), lambda i, j: (i, 0)),
        pl.BlockSpec((1, d), lambda i, j: (0, 0)),
        pl.BlockSpec((d, tn), lambda i, j: (0, j)),
        pl.BlockSpec((1, HEAD_DIM), lambda i, j: (0, 0)),
        pl.BlockSpec((1, HEAD_DIM), lambda i, j: (0, 0)),
    ]
    args = [x, g.reshape(1, d), w, gq.reshape(1, HEAD_DIM), gk.reshape(1, HEAD_DIM)]
    out_shape = [jax.ShapeDtypeStruct((n, nout), jnp.bfloat16)]
    out_specs = [pl.BlockSpec((tm, tn), lambda i, j: (i, j))]
    if has_f:
        in_specs.append(pl.BlockSpec((d, LANES), lambda i, j: (0, 0)))
        args.append(wf)
        out_shape.append(jax.ShapeDtypeStruct((n, LANES), jnp.float32))
        out_specs.append(pl.BlockSpec((tm, LANES), lambda i, j: (i, 0)))
    res = pl.pallas_call(
        functools.partial(_proj_kernel, n_q=n_q_cols // tn, n_k=n_k_cols // tn,
                          has_f=has_f, tn=tn),
        grid=(n // tm, nout // tn),
        in_specs=in_specs,
        out_specs=out_specs,
        out_shape=out_shape,
        scratch_shapes=[pltpu.VMEM((tm, d), jnp.bfloat16)],
        compiler_params=_params(("arbitrary", "arbitrary")),
        name="norm_proj",
    )(*args)
    return res if has_f else res[0]


def _logf_cumsum_kernel(f_ref, b_ref, c_ref):
    z = f_ref[0] + b_ref[...]
    c = jnp.minimum(z, 0.0) - jnp.log1p(jnp.exp(-jnp.abs(z)))
    s = c.shape[0]
    row = lax.broadcasted_iota(jnp.int32, c.shape, 0)
    d = 1
    while d < s:
        c = c + jnp.where(row >= d, pltpu.roll(c, d, axis=0), 0.0)
        d *= 2
    c_ref[0] = c


def _logf_cumsum(fz, b):
    bsz, s, _ = fz.shape
    return pl.pallas_call(
        _logf_cumsum_kernel,
        grid=(bsz,),
        in_specs=[pl.BlockSpec((1, s, LANES), lambda i: (i, 0, 0)),
                  pl.BlockSpec((1, LANES), lambda i: (0, 0))],
        out_specs=pl.BlockSpec((1, s, LANES), lambda i: (i, 0, 0)),
        out_shape=jax.ShapeDtypeStruct((bsz, s, LANES), jnp.float32),
        compiler_params=_params(("arbitrary",)),
        name="logf_cumsum",
    )(fz, b)


def _fox_kernel(q_ref, k_ref, v_ref, cq_ref, ck_ref, o_ref, m_ref, l_ref, acc_ref, *, t):
    h = pl.program_id(1)
    qi = pl.program_id(2)
    q = q_ref[0]
    lane = lax.broadcasted_iota(jnp.int32, cq_ref.shape[1:], 1)
    cq = jnp.sum(jnp.where(lane == h, cq_ref[0], 0.0), axis=1, keepdims=True)

    m_ref[...] = jnp.full_like(m_ref, -jnp.inf)
    l_ref[...] = jnp.zeros_like(l_ref)
    acc_ref[...] = jnp.zeros_like(acc_ref)

    def step(kv, masked):
        start = pl.multiple_of(kv * t, t)
        k = k_ref[0, pl.ds(start, t), :]
        v = v_ref[0, pl.ds(start, t), :]
        s = lax.dot_general(q, k, (((1,), (1,)), ((), ())),
                            preferred_element_type=jnp.float32)
        s = s + cq - ck_ref[0, 0, pl.ds(kv, 1), :]
        if masked:
            r = lax.broadcasted_iota(jnp.int32, s.shape, 0)
            c = lax.broadcasted_iota(jnp.int32, s.shape, 1)
            s = jnp.where(r >= c, s, -jnp.inf)
        m_old = m_ref[...]
        m_new = jnp.maximum(m_old, jnp.max(s, axis=-1, keepdims=True))
        alpha = jnp.exp(m_old - m_new)
        p = jnp.exp(s - m_new)
        l_ref[...] = alpha * l_ref[...] + jnp.sum(p, axis=-1, keepdims=True)
        acc_ref[...] = alpha * acc_ref[...] + jnp.dot(
            p.astype(v.dtype), v, preferred_element_type=jnp.float32)
        m_ref[...] = m_new

    def body(kv, carry):
        step(kv, False)
        return carry

    lax.fori_loop(0, qi, body, 0)
    step(qi, True)
    o_ref[0] = (acc_ref[...] / l_ref[...]).astype(o_ref.dtype)


def _fox_attention(qkv, c_bsh, ck, *, t=512):
    bsz, s, d3 = qkv.shape
    d = d3 // 3
    nh = d // HEAD_DIM
    return pl.pallas_call(
        functools.partial(_fox_kernel, t=t),
        grid=(bsz, nh, s // t),
        in_specs=[
            pl.BlockSpec((1, t, HEAD_DIM), lambda b, h, i: (b, i, h)),
            pl.BlockSpec((1, s, HEAD_DIM), lambda b, h, i: (b, 0, nh + h)),
            pl.BlockSpec((1, s, HEAD_DIM), lambda b, h, i: (b, 0, 2 * nh + h)),
            pl.BlockSpec((1, t, nh), lambda b, h, i: (b, i, 0)),
            pl.BlockSpec((1, 1, s // t, t), lambda b, h, i: (b, h, 0, 0)),
        ],
        out_specs=pl.BlockSpec((1, t, HEAD_DIM), lambda b, h, i: (b, i, h)),
        out_shape=jax.ShapeDtypeStruct((bsz, s, d), jnp.bfloat16),
        scratch_shapes=[pltpu.VMEM((t, 1), jnp.float32),
                        pltpu.VMEM((t, 1), jnp.float32),
                        pltpu.VMEM((t, HEAD_DIM), jnp.float32)],
        compiler_params=_params(("arbitrary", "arbitrary", "arbitrary")),
        name="fox_attention",
    )(qkv, qkv, qkv, c_bsh, ck)


def _out_proj_kernel(h_ref, o_ref, w_ref, out_ref):
    out_ref[...] = h_ref[...] + jnp.dot(o_ref[...], w_ref[...],
                                        preferred_element_type=jnp.float32)


def _out_proj(h, o, w, *, tm=512, tn=1024):
    n, d = h.shape
    return pl.pallas_call(
        _out_proj_kernel,
        grid=(n // tm, d // tn),
        in_specs=[pl.BlockSpec((tm, tn), lambda i, j: (i, j)),
                  pl.BlockSpec((tm, d), lambda i, j: (i, 0)),
                  pl.BlockSpec((d, tn), lambda i, j: (0, j))],
        out_specs=pl.BlockSpec((tm, tn), lambda i, j: (i, j)),
        out_shape=jax.ShapeDtypeStruct((n, d), jnp.float32),
        compiler_params=_params(("arbitrary", "arbitrary")),
        name="out_proj",
    )(h, o, w)


def _mlp_kernel(h_ref, g_ref, w1_ref, w2_ref, out_ref, u_ref, acc_ref):
    f = pl.program_id(1)

    @pl.when(f == 0)
    def _():
        x = h_ref[...]
        r = lax.rsqrt(jnp.mean(x * x, axis=-1, keepdims=True) + EPS)
        u_ref[...] = (x * r * g_ref[...]).astype(u_ref.dtype)
        acc_ref[...] = jnp.zeros_like(acc_ref)

    a = jnp.dot(u_ref[...], w1_ref[...], preferred_element_type=jnp.float32)
    a = jnp.maximum(a, 0.0)
    a = (a * a).astype(w2_ref.dtype)
    acc_ref[...] += jnp.dot(a, w2_ref[...], preferred_element_type=jnp.float32)

    @pl.when(f == pl.num_programs(1) - 1)
    def _():
        out_ref[...] = h_ref[...] + acc_ref[...]


def _mlp(h, g, w1, w2, *, tm=512, tf=1024):
    n, d = h.shape
    dff = w1.shape[1]
    return pl.pallas_call(
        _mlp_kernel,
        grid=(n // tm, dff // tf),
        in_specs=[pl.BlockSpec((tm, d), lambda i, f: (i, 0)),
                  pl.BlockSpec((1, d), lambda i, f: (0, 0)),
                  pl.BlockSpec((d, tf), lambda i, f: (0, f)),
                  pl.BlockSpec((tf, d), lambda i, f: (f, 0))],
        out_specs=pl.BlockSpec((tm, d), lambda i, f: (i, 0)),
        out_shape=jax.ShapeDtypeStruct((n, d), jnp.float32),
        scratch_shapes=[pltpu.VMEM((tm, d), jnp.bfloat16),
                        pltpu.VMEM((tm, d), jnp.float32)],
        compiler_params=_params(("arbitrary", "arbitrary")),
        name="sq_relu_mlp",
    )(h, g.reshape(1, d), w1, w2)


def _band_kernel(q_ref, k_ref, v_ref, bias_ref, o_ref, kp_ref, vp_ref):
    g = pl.program_id(2)

    @pl.when(g == 0)
    def _():
        kp_ref[0:_PAD, :] = jnp.zeros((_PAD, HEAD_DIM), kp_ref.dtype)
        vp_ref[0:_PAD, :] = jnp.zeros((_PAD, HEAD_DIM), vp_ref.dtype)
        kp_ref[_PAD:, :] = k_ref[0]
        vp_ref[_PAD:, :] = v_ref[0]

    start = pl.multiple_of(g * _GROUP, _GROUP)
    kw = kp_ref[pl.ds(start, _WIN), :]
    vw = vp_ref[pl.ds(start, _WIN), :]
    s = lax.dot_general(q_ref[0], kw, (((1,), (1,)), ((), ())),
                        preferred_element_type=jnp.float32)
    s = s + bias_ref[0]
    col = lax.broadcasted_iota(jnp.int32, s.shape, 1)
    s = jnp.where(col + start >= _PAD, s, -jnp.inf)
    m = jnp.max(s, axis=-1, keepdims=True)
    p = jnp.exp(s - m)
    l = jnp.sum(p, axis=-1, keepdims=True)
    o = jnp.dot(p.astype(vw.dtype), vw, preferred_element_type=jnp.float32)
    o_ref[0] = (o / l).astype(o_ref.dtype)


def _band_attention(q, kv, bias_g):
    bsz, s, d = q.shape
    nh = d // HEAD_DIM
    return pl.pallas_call(
        _band_kernel,
        grid=(bsz, nh, s // _GROUP),
        in_specs=[
            pl.BlockSpec((1, _GROUP, HEAD_DIM), lambda b, h, g: (b, g, h)),
            pl.BlockSpec((1, s, HEAD_DIM), lambda b, h, g: (b, 0, h)),
            pl.BlockSpec((1, s, HEAD_DIM), lambda b, h, g: (b, 0, nh + h)),
            pl.BlockSpec((1, _GROUP, _WIN), lambda b, h, g: (h, 0, 0)),
        ],
        out_specs=pl.BlockSpec((1, _GROUP, HEAD_DIM), lambda b, h, g: (b, g, h)),
        out_shape=jax.ShapeDtypeStruct((bsz, s, d), jnp.bfloat16),
        scratch_shapes=[pltpu.VMEM((_PAD + s, HEAD_DIM), jnp.bfloat16),
                        pltpu.VMEM((_PAD + s, HEAD_DIM), jnp.bfloat16)],
        compiler_params=_params(("arbitrary", "arbitrary", "arbitrary")),
        name="band_attention",
    )(q, kv, kv, bias_g)


def _group_bias(rel_table):
    qi = jnp.arange(CHUNK)[:, None]
    km = jnp.arange(_BAND)[None, :]
    idx = jnp.clip(_PAD + qi - km, -(CHUNK - 1), REL_CLIP) + (CHUNK - 1)
    bias = rel_table[:, idx].astype(jnp.float32)
    out = jnp.full((rel_table.shape[0], _GROUP, _WIN), -jnp.inf, jnp.float32)
    for c in range(_GROUP // CHUNK):
        out = out.at[:, c * CHUNK:(c + 1) * CHUNK, c * CHUNK:c * CHUNK + _BAND].set(bias)
    return out


def kernel(x, a_norm_g, a_w_in, a_b_f, a_q_g, a_k_g, a_w_out, mlp_norm_g, mlp_w1, mlp_w2,
           kv_norm_g, kv_w, kv_k_g, b_norm_g, b_w_q, b_q_g, b_rel, b_w_out):
    bsz, s, d = x.shape
    n = bsz * s
    nh = N_HEADS
    bf = jnp.bfloat16
    scale = HEAD_DIM ** -0.5
    t_att = 512
    ones = jnp.ones((HEAD_DIM,), jnp.float32)

    h = x.reshape(n, d)

    w_in = a_w_in[0]
    w_qkv = w_in[:, :3 * d].astype(bf)
    w_f = jnp.pad(w_in[:, 3 * d:], ((0, 0), (0, LANES - nh))).astype(bf)
    qkv, fz = _proj(h, a_norm_g[0], w_qkv, a_q_g[0] * scale, a_k_g[0], w_f,
                    n_q_cols=d, n_k_cols=d)
    b_f = jnp.pad(a_b_f[0], (0, LANES - nh)).reshape(1, LANES)
    c = _logf_cumsum(fz.reshape(bsz, s, LANES), b_f)[:, :, :nh]
    ck = jnp.transpose(c, (0, 2, 1)).reshape(bsz, nh, s // t_att, t_att)
    o = _fox_attention(qkv.reshape(bsz, s, 3 * d), c, ck, t=t_att)
    h = _out_proj(h, o.reshape(n, d), a_w_out[0].astype(bf))
    h = _mlp(h, mlp_norm_g[0], mlp_w1[0].astype(bf), mlp_w2[0].astype(bf))

    kv = _proj(h, kv_norm_g, kv_w.astype(bf), ones, kv_k_g, n_q_cols=0, n_k_cols=d)

    qb = _proj(h, b_norm_g[0], b_w_q[0].astype(bf), b_q_g[0] * scale, ones,
               n_q_cols=d, n_k_cols=0)
    o = _band_attention(qb.reshape(bsz, s, d), kv.reshape(bsz, s, 2 * d),
                        _group_bias(b_rel[0]))
    h = _out_proj(h, o.reshape(n, d), b_w_out[0].astype(bf))
    h = _mlp(h, mlp_norm_g[1], mlp_w1[1].astype(bf), mlp_w2[1].astype(bf))
    return h.reshape(bsz, s, d)
```

```python
import functools
import math

import jax
import jax.numpy as jnp
from jax import lax
from jax.experimental import pallas as pl
from jax.experimental.pallas import tpu as pltpu

N_HEADS = 16
HEAD_DIM = 128
CHUNK = 64
N_PREV_CHUNKS = 8
REL_CLIP = 256
EPS = 1e-6
LOG2E = math.log2(math.e)

LANES = 128
VMEM_LIMIT = 56 * 1024 * 1024

_BAND = (N_PREV_CHUNKS + 1) * CHUNK
_PAD = N_PREV_CHUNKS * CHUNK
_GROUP = 4 * CHUNK
_WIN = _PAD + _GROUP


def _params(sem):
    return pltpu.CompilerParams(dimension_semantics=sem, vmem_limit_bytes=VMEM_LIMIT)


def _proj_kernel(*refs, n_q, n_k, has_f, tn):
    if has_f:
        x_ref, g_ref, w_ref, gq_ref, gk_ref, wf_ref, o_ref, f_ref, u_ref = refs
    else:
        x_ref, g_ref, w_ref, gq_ref, gk_ref, o_ref, u_ref = refs
    j = pl.program_id(1)

    @pl.when(j == 0)
    def _():
        x = x_ref[...]
        r = lax.rsqrt(jnp.mean(x * x, axis=-1, keepdims=True) + EPS)
        u_ref[...] = (x * r * g_ref[...]).astype(u_ref.dtype)
        if has_f:
            f_ref[...] = jnp.dot(u_ref[...], wf_ref[...],
                                 preferred_element_type=jnp.float32)

    y = jnp.dot(u_ref[...], w_ref[...], preferred_element_type=jnp.float32)

    def head_norm(gain_ref):
        for c in range(tn // HEAD_DIM):
            sl = slice(c * HEAD_DIM, (c + 1) * HEAD_DIM)
            ys = y[:, sl]
            r = lax.rsqrt(jnp.mean(ys * ys, axis=-1, keepdims=True) + EPS)
            o_ref[:, sl] = (ys * r * gain_ref[...]).astype(o_ref.dtype)

    if n_q > 0:
        @pl.when(j < n_q)
        def _():
            head_norm(gq_ref)
    if n_k > 0:
        @pl.when((j >= n_q) & (j < n_q + n_k))
        def _():
            head_norm(gk_ref)

    @pl.when(j >= n_q + n_k)
    def _():
        o_ref[...] = y.astype(o_ref.dtype)


def _proj(x, g, w, gq, gk, wf=None, *, n_q_cols, n_k_cols, tm=512, tn=1024):
    n, d = x.shape
    nout = w.shape[1]
    has_f = wf is not None
    in_specs = [
        pl.BlockSpec((tm, d), lambda i, j: (i, 0)),
        pl.BlockSpec((1, d), lambda i, j: (0, 0)),
        pl.BlockSpec((d, tn), lambda i, j: (0, j)),
        pl.BlockSpec((1, HEAD_DIM), lambda i, j: (0, 0)),
        pl.BlockSpec((1, HEAD_DIM), lambda i, j: (0, 0)),
    ]
    args = [x, g.reshape(1, d), w, gq.reshape(1, HEAD_DIM), gk.reshape(1, HEAD_DIM)]
    out_shape = [jax.ShapeDtypeStruct((n, nout), jnp.bfloat16)]
    out_specs = [pl.BlockSpec((tm, tn), lambda i, j: (i, j))]
    if has_f:
        in_specs.append(pl.BlockSpec((d, LANES), lambda i, j: (0, 0)))
        args.append(wf)
        out_shape.append(jax.ShapeDtypeStruct((n, LANES), jnp.float32))
        out_specs.append(pl.BlockSpec((tm, LANES), lambda i, j: (i, 0)))
    res = pl.pallas_call(
        functools.partial(_proj_kernel, n_q=n_q_cols // tn, n_k=n_k_cols // tn,
                          has_f=has_f, tn=tn),
        grid=(n // tm, nout // tn),
        in_specs=in_specs,
        out_specs=out_specs,
        out_shape=out_shape,
        scratch_shapes=[pltpu.VMEM((tm, d), jnp.bfloat16)],
        compiler_params=_params(("arbitrary", "arbitrary")),
        name="norm_proj",
    )(*args)
    return res if has_f else res[0]


def _logf_cumsum_kernel(f_ref, b_ref, hi_ref, mid_ref, lo_ref):
    z = f_ref[0] + b_ref[...]
    c = jnp.minimum(z, 0.0) - jnp.log1p(jnp.exp(-jnp.abs(z)))
    s = c.shape[0]
    row = lax.broadcasted_iota(jnp.int32, c.shape, 0)
    d = 1
    while d < s:
        c = c + jnp.where(row >= d, pltpu.roll(c, d, axis=0), 0.0)
        d *= 2
    c = c * LOG2E
    hi = c.astype(jnp.bfloat16)
    r = c - hi.astype(jnp.float32)
    mid = r.astype(jnp.bfloat16)
    lo = (r - mid.astype(jnp.float32)).astype(jnp.bfloat16)
    hi_ref[0] = hi
    mid_ref[0] = mid
    lo_ref[0] = lo


def _logf_cumsum(fz, b):
    bsz, s, _ = fz.shape
    spec = pl.BlockSpec((1, s, LANES), lambda i: (i, 0, 0))
    out = jax.ShapeDtypeStruct((bsz, s, LANES), jnp.bfloat16)
    return pl.pallas_call(
        _logf_cumsum_kernel,
        grid=(bsz,),
        in_specs=[spec, pl.BlockSpec((1, LANES), lambda i: (0, 0))],
        out_specs=[spec, spec, spec],
        out_shape=[out, out, out],
        compiler_params=_params(("arbitrary",)),
        name="logf_cumsum",
    )(fz, b)


def _bias_columns(hi, mid, lo, nh):
    pieces = jnp.stack([hi[..., :nh], mid[..., :nh], lo[..., :nh]], axis=-1)
    pieces = jnp.transpose(pieces, (0, 2, 1, 3))
    ones = jnp.ones_like(pieces)
    zeros = jnp.zeros(pieces.shape[:-1] + (LANES - 6,), pieces.dtype)
    qx = jnp.concatenate([ones, pieces, zeros], axis=-1)
    kx = jnp.concatenate([-pieces, ones, zeros], axis=-1)
    return qx, kx


_NT = (((1,), (1,)), ((), ()))


def _fox_kernel(q_ref, qx_ref, k_ref, kx_ref, v_ref, o_ref,
                qa_ref, ka_ref, vt_ref, s0_ref, s1_ref, p0_ref, p1_ref, mc0_ref, mc1_ref,
                m_ref, l_ref, lp_ref, al_ref, acc_ref, *, t, tc, rb):
    nt = ka_ref.shape[0]
    n_steps = nt * (nt + 1) // 2
    nc = t // tc
    nr = t // rb
    f32 = jnp.float32
    s_refs, p_refs, mc_refs = (s0_ref, s1_ref), (p0_ref, p1_ref), (mc0_ref, mc1_ref)

    for j in range(nt):
        rows = slice(j * t, (j + 1) * t)
        ka_ref[j, :, 0:HEAD_DIM] = k_ref[0, rows, :]
        ka_ref[j, :, HEAD_DIM:] = kx_ref[0, 0, rows, :]
        qa_ref[j, :, 0:HEAD_DIM] = q_ref[0, rows, :]
        qa_ref[j, :, HEAD_DIM:] = qx_ref[0, 0, rows, :]
        vt_ref[j] = v_ref[0, rows, :].T
    p1_ref[...] = jnp.zeros_like(p1_ref)
    l_ref[...] = jnp.zeros_like(l_ref)
    al_ref[...] = jnp.zeros_like(al_ref)
    acc_ref[...] = jnp.zeros_like(acc_ref)
    m_ref[...] = jnp.full_like(m_ref, -jnp.inf)

    def logits(qi_n, kv_n, r, c, masked):
        s = lax.dot_general(ka_ref[kv_n, r * rb:(r + 1) * rb, :],
                            qa_ref[qi_n, c * tc:(c + 1) * tc, :], _NT,
                            preferred_element_type=f32)
        if masked:
            key = lax.broadcasted_iota(jnp.int32, s.shape, 0) + r * rb
            qry = lax.broadcasted_iota(jnp.int32, s.shape, 1) + c * tc
            s = jnp.where(key <= qry, s, -jnp.inf)
        return s

    def fold8(x, op):
        return op(x.reshape(x.shape[0] // 8, 8, x.shape[1]), axis=0)

    def finalize(qb, l):
        o_ref[0, pl.ds(pl.multiple_of(qb * t, t), t), :] = (
            acc_ref[...] / l).T.astype(o_ref.dtype)

    for c in range(nc):
        mx = None
        for r in range(nr):
            s = logits(0, 0, r, c, True)
            s0_ref[r * rb:(r + 1) * rb, c * tc:(c + 1) * tc] = s
            pm = fold8(s, jnp.max)
            mx = pm if mx is None else jnp.maximum(mx, pm)
        mc0_ref[:, c * tc:(c + 1) * tc] = jnp.max(mx, axis=0, keepdims=True)

    def body(cur, qi, kv, qi_p, kv_p, qi_n, kv_n, masked_next):
        nxt = 1 - cur
        s_cur, s_nxt = s_refs[cur], s_refs[nxt]
        p_cur, p_prev = p_refs[cur], p_refs[nxt]
        kvp = jnp.maximum(kv_p, 0)
        m_old = jnp.where(kv == 0, -jnp.inf, m_ref[...])
        m_new = jnp.maximum(m_old, mc_refs[cur][...])
        alpha = jnp.exp2(m_old - m_new)
        alpha_prev = al_ref[...]
        mx = [None] * nc
        ls = [None] * nc
        pv = [None] * nc
        for r in range(nr):
            rows = slice(r * rb, (r + 1) * rb)
            for c in range(nc):
                cols = slice(c * tc, (c + 1) * tc)
                s = logits(qi_n, kv_n, r, c, masked_next)
                s_nxt[rows, cols] = s
                pm = fold8(s, jnp.max)
                mx[c] = pm if r == 0 else jnp.maximum(mx[c], pm)
                p = jnp.exp2(s_cur[rows, cols] - m_new[:, cols])
                ps = fold8(p, jnp.sum)
                ls[c] = ps if r == 0 else ls[c] + ps
                p_cur[rows, cols] = p.astype(p_cur.dtype)
                d = jnp.dot(vt_ref[kvp, :, rows], p_prev[rows, cols],
                            preferred_element_type=f32)
                pv[c] = d if r == 0 else pv[c] + d
        for c in range(nc):
            cols = slice(c * tc, (c + 1) * tc)
            mc_refs[nxt][:, cols] = jnp.max(mx[c], axis=0, keepdims=True)
            l_old = l_ref[:, cols]
            lp_ref[:, cols] = l_old
            l_ref[:, cols] = alpha[:, cols] * l_old + jnp.sum(ls[c], axis=0, keepdims=True)
            acc_ref[:, cols] = alpha_prev[:, cols] * acc_ref[:, cols] + pv[c]
        m_ref[...] = m_new
        al_ref[...] = alpha

        @pl.when(kv_p == qi_p)
        def _():
            finalize(qi_p, lp_ref[...])

    def one_step(cur, qi, kv, qi_p, kv_p):
        wrap = kv + 1 > qi
        qi_n = jnp.where(wrap, qi + 1, qi)
        kv_n = jnp.where(wrap, 0, kv + 1)
        qi_c = jnp.minimum(qi_n, nt - 1)
        diag = kv_n == qi_n

        @pl.when(diag)
        def _():
            body(cur, qi, kv, qi_p, kv_p, qi_c, kv_n, True)

        @pl.when(jnp.logical_not(diag))
        def _():
            body(cur, qi, kv, qi_p, kv_p, qi_c, kv_n, False)

        return qi_n, kv_n, qi, kv

    def loop_body(_, carry):
        return one_step(1, *one_step(0, *carry))

    zero = jnp.int32(0)
    lax.fori_loop(0, n_steps // 2, loop_body, (zero, zero, zero, jnp.int32(-1)))

    alpha_prev = al_ref[...]
    for c in range(nc):
        cols = slice(c * tc, (c + 1) * tc)
        d = jnp.dot(vt_ref[nt - 1], p_refs[(n_steps - 1) % 2][:, cols],
                    preferred_element_type=f32)
        acc_ref[:, cols] = alpha_prev[:, cols] * acc_ref[:, cols] + d
    finalize(nt - 1, l_ref[...])


def _fox_attention(qkv, qx, kx, *, t=512, tc=256, rb=256):
    bsz, s, d3 = qkv.shape
    d = d3 // 3
    nh = d // HEAD_DIM
    nt = s // t
    assert (nt * (nt + 1) // 2) % 2 == 0
    f32 = jnp.float32
    bf = jnp.bfloat16
    return pl.pallas_call(
        functools.partial(_fox_kernel, t=t, tc=tc, rb=rb),
        grid=(bsz, nh),
        in_specs=[
            pl.BlockSpec((1, s, HEAD_DIM), lambda b, h: (b, 0, h)),
            pl.BlockSpec((1, 1, s, LANES), lambda b, h: (b, h, 0, 0)),
            pl.BlockSpec((1, s, HEAD_DIM), lambda b, h: (b, 0, nh + h)),
            pl.BlockSpec((1, 1, s, LANES), lambda b, h: (b, h, 0, 0)),
            pl.BlockSpec((1, s, HEAD_DIM), lambda b, h: (b, 0, 2 * nh + h)),
        ],
        out_specs=pl.BlockSpec((1, s, HEAD_DIM), lambda b, h: (b, 0, h)),
        out_shape=jax.ShapeDtypeStruct((bsz, s, d), bf),
        scratch_shapes=[pltpu.VMEM((nt, t, HEAD_DIM + LANES), bf),
                        pltpu.VMEM((nt, t, HEAD_DIM + LANES), bf),
                        pltpu.VMEM((nt, HEAD_DIM, t), bf),
                        pltpu.VMEM((t, t), f32),
                        pltpu.VMEM((t, t), f32),
                        pltpu.VMEM((t, t), bf),
                        pltpu.VMEM((t, t), bf),
                        pltpu.VMEM((1, t), f32),
                        pltpu.VMEM((1, t), f32),
                        pltpu.VMEM((1, t), f32),
                        pltpu.VMEM((1, t), f32),
                        pltpu.VMEM((1, t), f32),
                        pltpu.VMEM((1, t), f32),
                        pltpu.VMEM((HEAD_DIM, t), f32)],
        compiler_params=_params(("arbitrary", "arbitrary")),
        name="fox_attention",
    )(qkv, qx, qkv, kx, qkv)


def _fox_from_fz(qkv, fz, b_f):
    nh = qkv.shape[-1] // (3 * HEAD_DIM)
    hi, mid, lo = _logf_cumsum(fz, b_f)
    qx, kx = _bias_columns(hi, mid, lo, nh)
    return _fox_attention(qkv, qx, kx)


def _out_proj_kernel(h_ref, o_ref, w_ref, out_ref):
    out_ref[...] = h_ref[...] + jnp.dot(o_ref[...], w_ref[...],
                                        preferred_element_type=jnp.float32)


def _out_proj(h, o, w, *, tm=512, tn=1024):
    n, d = h.shape
    return pl.pallas_call(
        _out_proj_kernel,
        grid=(n // tm, d // tn),
        in_specs=[pl.BlockSpec((tm, tn), lambda i, j: (i, j)),
                  pl.BlockSpec((tm, d), lambda i, j: (i, 0)),
                  pl.BlockSpec((d, tn), lambda i, j: (0, j))],
        out_specs=pl.BlockSpec((tm, tn), lambda i, j: (i, j)),
        out_shape=jax.ShapeDtypeStruct((n, d), jnp.float32),
        compiler_params=_params(("arbitrary", "arbitrary")),
        name="out_proj",
    )(h, o, w)


def _mlp_kernel(h_ref, g_ref, w1_ref, w2_ref, out_ref, u_ref, acc_ref):
    f = pl.program_id(1)

    @pl.when(f == 0)
    def _():
        x = h_ref[...]
        r = lax.rsqrt(jnp.mean(x * x, axis=-1, keepdims=True) + EPS)
        u_ref[...] = (x * r * g_ref[...]).astype(u_ref.dtype)
        acc_ref[...] = jnp.zeros_like(acc_ref)

    a = jnp.dot(u_ref[...], w1_ref[...], preferred_element_type=jnp.float32)
    a = jnp.maximum(a, 0.0)
    a = (a * a).astype(w2_ref.dtype)
    acc_ref[...] += jnp.dot(a, w2_ref[...], preferred_element_type=jnp.float32)

    @pl.when(f == pl.num_programs(1) - 1)
    def _():
        out_ref[...] = h_ref[...] + acc_ref[...]


def _mlp(h, g, w1, w2, *, tm=512, tf=1024):
    n, d = h.shape
    dff = w1.shape[1]
    return pl.pallas_call(
        _mlp_kernel,
        grid=(n // tm, dff // tf),
        in_specs=[pl.BlockSpec((tm, d), lambda i, f: (i, 0)),
                  pl.BlockSpec((1, d), lambda i, f: (0, 0)),
                  pl.BlockSpec((d, tf), lambda i, f: (0, f)),
                  pl.BlockSpec((tf, d), lambda i, f: (f, 0))],
        out_specs=pl.BlockSpec((tm, d), lambda i, f: (i, 0)),
        out_shape=jax.ShapeDtypeStruct((n, d), jnp.float32),
        scratch_shapes=[pltpu.VMEM((tm, d), jnp.bfloat16),
                        pltpu.VMEM((tm, d), jnp.float32)],
        compiler_params=_params(("arbitrary", "arbitrary")),
        name="sq_relu_mlp",
    )(h, g.reshape(1, d), w1, w2)


def _band_kernel(q_ref, k_ref, v_ref, bias_ref, o_ref, kp_ref, vp_ref):
    g = pl.program_id(2)

    @pl.when(g == 0)
    def _():
        kp_ref[0:_PAD, :] = jnp.zeros((_PAD, HEAD_DIM), kp_ref.dtype)
        vp_ref[0:_PAD, :] = jnp.zeros((_PAD, HEAD_DIM), vp_ref.dtype)
        kp_ref[_PAD:, :] = k_ref[0]
        vp_ref[_PAD:, :] = v_ref[0]

    start = pl.multiple_of(g * _GROUP, _GROUP)
    kw = kp_ref[pl.ds(start, _WIN), :]
    vw = vp_ref[pl.ds(start, _WIN), :]
    s = lax.dot_general(q_ref[0], kw, _NT, preferred_element_type=jnp.float32)
    s = s + bias_ref[0]
    col = lax.broadcasted_iota(jnp.int32, s.shape, 1)
    s = jnp.where(col + start >= _PAD, s, -jnp.inf)
    m = jnp.max(s, axis=-1, keepdims=True)
    p = jnp.exp(s - m)
    l = jnp.sum(p, axis=-1, keepdims=True)
    o = jnp.dot(p.astype(vw.dtype), vw, preferred_element_type=jnp.float32)
    o_ref[0] = (o / l).astype(o_ref.dtype)


def _band_attention(q, kv, bias_g):
    bsz, s, d = q.shape
    nh = d // HEAD_DIM
    return pl.pallas_call(
        _band_kernel,
        grid=(bsz, nh, s // _GROUP),
        in_specs=[
            pl.BlockSpec((1, _GROUP, HEAD_DIM), lambda b, h, g: (b, g, h)),
            pl.BlockSpec((1, s, HEAD_DIM), lambda b, h, g: (b, 0, h)),
            pl.BlockSpec((1, s, HEAD_DIM), lambda b, h, g: (b, 0, nh + h)),
            pl.BlockSpec((1, _GROUP, _WIN), lambda b, h, g: (h, 0, 0)),
        ],
        out_specs=pl.BlockSpec((1, _GROUP, HEAD_DIM), lambda b, h, g: (b, g, h)),
        out_shape=jax.ShapeDtypeStruct((bsz, s, d), jnp.bfloat16),
        scratch_shapes=[pltpu.VMEM((_PAD + s, HEAD_DIM), jnp.bfloat16),
                        pltpu.VMEM((_PAD + s, HEAD_DIM), jnp.bfloat16)],
        compiler_params=_params(("arbitrary", "arbitrary", "arbitrary")),
        name="band_attention",
    )(q, kv, kv, bias_g)


def _group_bias(rel_table):
    qi = jnp.arange(CHUNK)[:, None]
    km = jnp.arange(_BAND)[None, :]
    idx = jnp.clip(_PAD + qi - km, -(CHUNK - 1), REL_CLIP) + (CHUNK - 1)
    bias = rel_table[:, idx].astype(jnp.float32)
    out = jnp.full((rel_table.shape[0], _GROUP, _WIN), -jnp.inf, jnp.float32)
    for c in range(_GROUP // CHUNK):
        out = out.at[:, c * CHUNK:(c + 1) * CHUNK, c * CHUNK:c * CHUNK + _BAND].set(bias)
    return out


def kernel(x, a_norm_g, a_w_in, a_b_f, a_q_g, a_k_g, a_w_out, mlp_norm_g, mlp_w1, mlp_w2,
           kv_norm_g, kv_w, kv_k_g, b_norm_g, b_w_q, b_q_g, b_rel, b_w_out):
    bsz, s, d = x.shape
    n = bsz * s
    nh = N_HEADS
    bf = jnp.bfloat16
    scale = HEAD_DIM ** -0.5
    ones = jnp.ones((HEAD_DIM,), jnp.float32)

    h = x.reshape(n, d)

    w_in = a_w_in[0]
    w_qkv = w_in[:, :3 * d].astype(bf)
    w_f = jnp.pad(w_in[:, 3 * d:], ((0, 0), (0, LANES - nh))).astype(bf)
    qkv, fz = _proj(h, a_norm_g[0], w_qkv, a_q_g[0] * (scale * LOG2E), a_k_g[0], w_f,
                    n_q_cols=d, n_k_cols=d)
    b_f = jnp.pad(a_b_f[0], (0, LANES - nh)).reshape(1, LANES)
    o = _fox_from_fz(qkv.reshape(bsz, s, 3 * d), fz.reshape(bsz, s, LANES), b_f)
    h = _out_proj(h, o.reshape(n, d), a_w_out[0].astype(bf))
    h = _mlp(h, mlp_norm_g[0], mlp_w1[0].astype(bf), mlp_w2[0].astype(bf))

    kv = _proj(h, kv_norm_g, kv_w.astype(bf), ones, kv_k_g, n_q_cols=0, n_k_cols=d)

    qb = _proj(h, b_norm_g[0], b_w_q[0].astype(bf), b_q_g[0] * scale, ones,
               n_q_cols=d, n_k_cols=0)
    o = _band_attention(qb.reshape(bsz, s, d), kv.reshape(bsz, s, 2 * d),
                        _group_bias(b_rel[0]))
    h = _out_proj(h, o.reshape(n, d), b_w_out[0].astype(bf))
    h = _mlp(h, mlp_norm_g[1], mlp_w1[1].astype(bf), mlp_w2[1].astype(bf))
    return h.reshape(bsz, s, d)
```

```python
import functools
import math

import numpy as np

import jax
import jax.numpy as jnp
from jax import lax
from jax.experimental import pallas as pl
from jax.experimental.pallas import tpu as pltpu

N_HEADS = 16
HEAD_DIM = 128
CHUNK = 64
N_PREV_CHUNKS = 8
REL_CLIP = 256
EPS = 1e-6
LOG2E = math.log2(math.e)

LANES = 128
VMEM_LIMIT = 56 * 1024 * 1024

_BAND = (N_PREV_CHUNKS + 1) * CHUNK
_PAD = N_PREV_CHUNKS * CHUNK
_GROUP = 4 * CHUNK
_WIN = _PAD + _GROUP


def _params(sem):
    return pltpu.CompilerParams(dimension_semantics=sem, vmem_limit_bytes=VMEM_LIMIT)


def _proj_kernel(*refs, n_q, n_k, has_f, tn):
    if has_f:
        x_ref, g_ref, w_ref, gq_ref, gk_ref, wf_ref, o_ref, f_ref, u_ref = refs
    else:
        x_ref, g_ref, w_ref, gq_ref, gk_ref, o_ref, u_ref = refs
    j = pl.program_id(1)

    @pl.when(j == 0)
    def _():
        x = x_ref[...]
        r = lax.rsqrt(jnp.mean(x * x, axis=-1, keepdims=True) + EPS)
        u_ref[...] = (x * r * g_ref[...]).astype(u_ref.dtype)
        if has_f:
            f_ref[...] = jnp.dot(u_ref[...], wf_ref[...],
                                 preferred_element_type=jnp.float32)

    y = jnp.dot(u_ref[...], w_ref[...], preferred_element_type=jnp.float32)

    def head_norm(gain_ref):
        for c in range(tn // HEAD_DIM):
            sl = slice(c * HEAD_DIM, (c + 1) * HEAD_DIM)
            ys = y[:, sl]
            r = lax.rsqrt(jnp.mean(ys * ys, axis=-1, keepdims=True) + EPS)
            o_ref[:, sl] = (ys * r * gain_ref[...]).astype(o_ref.dtype)

    if n_q > 0:
        @pl.when(j < n_q)
        def _():
            head_norm(gq_ref)
    if n_k > 0:
        @pl.when((j >= n_q) & (j < n_q + n_k))
        def _():
            head_norm(gk_ref)

    @pl.when(j >= n_q + n_k)
    def _():
        o_ref[...] = y.astype(o_ref.dtype)


def _proj(x, g, w, gq, gk, wf=None, *, n_q_cols, n_k_cols, tm=512, tn=1024):
    n, d = x.shape
    nout = w.shape[1]
    has_f = wf is not None
    in_specs = [
        pl.BlockSpec((tm, d), lambda i, j: (i, 0)),
        pl.BlockSpec((1, d), lambda i, j: (0, 0)),
        pl.BlockSpec((d, tn), lambda i, j: (0, j)),
        pl.BlockSpec((1, HEAD_DIM), lambda i, j: (0, 0)),
        pl.BlockSpec((1, HEAD_DIM), lambda i, j: (0, 0)),
    ]
    args = [x, g.reshape(1, d), w, gq.reshape(1, HEAD_DIM), gk.reshape(1, HEAD_DIM)]
    out_shape = [jax.ShapeDtypeStruct((n, nout), jnp.bfloat16)]
    out_specs = [pl.BlockSpec((tm, tn), lambda i, j: (i, j))]
    if has_f:
        in_specs.append(pl.BlockSpec((d, LANES), lambda i, j: (0, 0)))
        args.append(wf)
        out_shape.append(jax.ShapeDtypeStruct((n, LANES), jnp.float32))
        out_specs.append(pl.BlockSpec((tm, LANES), lambda i, j: (i, 0)))
    res = pl.pallas_call(
        functools.partial(_proj_kernel, n_q=n_q_cols // tn, n_k=n_k_cols // tn,
                          has_f=has_f, tn=tn),
        grid=(n // tm, nout // tn),
        in_specs=in_specs,
        out_specs=out_specs,
        out_shape=out_shape,
        scratch_shapes=[pltpu.VMEM((tm, d), jnp.bfloat16)],
        compiler_params=_params(("arbitrary", "arbitrary")),
        name="norm_proj",
    )(*args)
    return res if has_f else res[0]


_HEAD_LANES = LANES // N_HEADS
_N_TERMS = 3


def _bias_column_constants():
    sel = np.zeros((_N_TERMS * LANES, 2 * LANES), np.float32)
    one = np.zeros((1, 2 * LANES), np.float32)
    for h in range(N_HEADS):
        for term in range(_N_TERMS):
            sel[term * LANES + h, h * _HEAD_LANES + term] = -1.0
            sel[term * LANES + h, LANES + h * _HEAD_LANES + _N_TERMS + term] = 1.0
            one[0, h * _HEAD_LANES + _N_TERMS + term] = 1.0
            one[0, LANES + h * _HEAD_LANES + term] = 1.0
    return jnp.asarray(sel, jnp.bfloat16), jnp.asarray(one, jnp.float32)


def _logf_cumsum_kernel(f_ref, b_ref, sel_ref, one_ref, qx_ref, kx_ref):
    z = f_ref[0] + b_ref[...]
    c = jnp.minimum(z, 0.0) - jnp.log1p(jnp.exp(-jnp.abs(z)))
    s = c.shape[0]
    row = lax.broadcasted_iota(jnp.int32, c.shape, 0)
    d = 1
    while d < s:
        c = c + jnp.where(row >= d, pltpu.roll(c, d, axis=0), 0.0)
        d *= 2
    c = c * LOG2E
    hi = c.astype(jnp.bfloat16)
    r = c - hi.astype(jnp.float32)
    mid = r.astype(jnp.bfloat16)
    lo = (r - mid.astype(jnp.float32)).astype(jnp.bfloat16)
    terms = jnp.concatenate([hi, mid, lo], axis=1)
    y = jnp.dot(terms, sel_ref[...], preferred_element_type=jnp.float32) + one_ref[...]
    kx_ref[0] = y[:, :LANES].astype(kx_ref.dtype)
    qx_ref[0] = y[:, LANES:].astype(qx_ref.dtype)


def _logf_cumsum(fz, b):
    bsz, s, _ = fz.shape
    sel, one = _bias_column_constants()
    spec = pl.BlockSpec((1, s, LANES), lambda i: (i, 0, 0))
    out = jax.ShapeDtypeStruct((bsz, s, LANES), jnp.bfloat16)
    return pl.pallas_call(
        _logf_cumsum_kernel,
        grid=(bsz,),
        in_specs=[spec, pl.BlockSpec((1, LANES), lambda i: (0, 0)),
                  pl.BlockSpec(sel.shape, lambda i: (0, 0)),
                  pl.BlockSpec(one.shape, lambda i: (0, 0))],
        out_specs=[spec, spec],
        out_shape=[out, out],
        compiler_params=_params(("arbitrary",)),
        name="logf_cumsum",
    )(fz, b, sel, one)


_NT = (((1,), (1,)), ((), ()))


def _fox_kernel(q_ref, qx_ref, k_ref, kx_ref, v_ref, o_ref,
                qa_ref, ka_ref, vt_ref, s0_ref, s1_ref, p0_ref, p1_ref, mc0_ref, mc1_ref,
                m_ref, l_ref, lp_ref, al_ref, acc_ref, *, t, tc, rb):
    nt = ka_ref.shape[0]
    n_steps = nt * (nt + 1) // 2
    nc = t // tc
    nr = t // rb
    f32 = jnp.float32
    s_refs, p_refs, mc_refs = (s0_ref, s1_ref), (p0_ref, p1_ref), (mc0_ref, mc1_ref)

    lane = lax.broadcasted_iota(jnp.int32, (t, LANES), 1)
    own_lanes = lane // _HEAD_LANES == pl.program_id(1)

    for j in range(nt):
        rows = slice(j * t, (j + 1) * t)
        ka_ref[j, :, 0:HEAD_DIM] = k_ref[0, rows, :]
        ka_ref[j, :, HEAD_DIM:] = kx_ref[0, rows, :]
        qa_ref[j, :, 0:HEAD_DIM] = q_ref[0, rows, :]
        qx = qx_ref[0, rows, :]
        qa_ref[j, :, HEAD_DIM:] = jnp.where(own_lanes, qx, jnp.zeros_like(qx))
        vt_ref[j] = v_ref[0, rows, :].T
    p1_ref[...] = jnp.zeros_like(p1_ref)
    l_ref[...] = jnp.zeros_like(l_ref)
    al_ref[...] = jnp.zeros_like(al_ref)
    acc_ref[...] = jnp.zeros_like(acc_ref)
    m_ref[...] = jnp.full_like(m_ref, -jnp.inf)

    def logits(qi_n, kv_n, r, c, masked):
        s = lax.dot_general(ka_ref[kv_n, r * rb:(r + 1) * rb, :],
                            qa_ref[qi_n, c * tc:(c + 1) * tc, :], _NT,
                            preferred_element_type=f32)
        if masked:
            key = lax.broadcasted_iota(jnp.int32, s.shape, 0) + r * rb
            qry = lax.broadcasted_iota(jnp.int32, s.shape, 1) + c * tc
            s = jnp.where(key <= qry, s, -jnp.inf)
        return s

    def fold8(x, op):
        return op(x.reshape(x.shape[0] // 8, 8, x.shape[1]), axis=0)

    def finalize(qb, l):
        o_ref[0, pl.ds(pl.multiple_of(qb * t, t), t), :] = (
            acc_ref[...] / l).T.astype(o_ref.dtype)

    for c in range(nc):
        mx = None
        for r in range(nr):
            s = logits(0, 0, r, c, True)
            s0_ref[r * rb:(r + 1) * rb, c * tc:(c + 1) * tc] = s
            pm = fold8(s, jnp.max)
            mx = pm if mx is None else jnp.maximum(mx, pm)
        mc0_ref[:, c * tc:(c + 1) * tc] = jnp.max(mx, axis=0, keepdims=True)

    def body(cur, qi, kv, qi_p, kv_p, qi_n, kv_n, masked_next):
        nxt = 1 - cur
        s_cur, s_nxt = s_refs[cur], s_refs[nxt]
        p_cur, p_prev = p_refs[cur], p_refs[nxt]
        kvp = jnp.maximum(kv_p, 0)
        m_old = jnp.where(kv == 0, -jnp.inf, m_ref[...])
        m_new = jnp.maximum(m_old, mc_refs[cur][...])
        alpha = jnp.exp2(m_old - m_new)
        alpha_prev = al_ref[...]
        mx = [None] * nc
        ls = [None] * nc
        pv = [None] * nc
        for r in range(nr):
            rows = slice(r * rb, (r + 1) * rb)
            for c in range(nc):
                cols = slice(c * tc, (c + 1) * tc)
                s = logits(qi_n, kv_n, r, c, masked_next)
                s_nxt[rows, cols] = s
                pm = fold8(s, jnp.max)
                mx[c] = pm if r == 0 else jnp.maximum(mx[c], pm)
                p = jnp.exp2(s_cur[rows, cols] - m_new[:, cols])
                ps = fold8(p, jnp.sum)
                ls[c] = ps if r == 0 else ls[c] + ps
                p_cur[rows, cols] = p.astype(p_cur.dtype)
                d = jnp.dot(vt_ref[kvp, :, rows], p_prev[rows, cols],
                            preferred_element_type=f32)
                pv[c] = d if r == 0 else pv[c] + d
        for c in range(nc):
            cols = slice(c * tc, (c + 1) * tc)
            mc_refs[nxt][:, cols] = jnp.max(mx[c], axis=0, keepdims=True)
            l_old = l_ref[:, cols]
            lp_ref[:, cols] = l_old
            l_ref[:, cols] = alpha[:, cols] * l_old + jnp.sum(ls[c], axis=0, keepdims=True)
            acc_ref[:, cols] = alpha_prev[:, cols] * acc_ref[:, cols] + pv[c]
        m_ref[...] = m_new
        al_ref[...] = alpha

        @pl.when(kv_p == qi_p)
        def _():
            finalize(qi_p, lp_ref[...])

    def one_step(cur, qi, kv, qi_p, kv_p):
        wrap = kv + 1 > qi
        qi_n = jnp.where(wrap, qi + 1, qi)
        kv_n = jnp.where(wrap, 0, kv + 1)
        qi_c = jnp.minimum(qi_n, nt - 1)
        diag = kv_n == qi_n

        @pl.when(diag)
        def _():
            body(cur, qi, kv, qi_p, kv_p, qi_c, kv_n, True)

        @pl.when(jnp.logical_not(diag))
        def _():
            body(cur, qi, kv, qi_p, kv_p, qi_c, kv_n, False)

        return qi_n, kv_n, qi, kv

    def loop_body(_, carry):
        return one_step(1, *one_step(0, *carry))

    zero = jnp.int32(0)
    lax.fori_loop(0, n_steps // 2, loop_body, (zero, zero, zero, jnp.int32(-1)))

    alpha_prev = al_ref[...]
    for c in range(nc):
        cols = slice(c * tc, (c + 1) * tc)
        d = jnp.dot(vt_ref[nt - 1], p_refs[(n_steps - 1) % 2][:, cols],
                    preferred_element_type=f32)
        acc_ref[:, cols] = alpha_prev[:, cols] * acc_ref[:, cols] + d
    finalize(nt - 1, l_ref[...])


def _fox_attention(qkv, qx, kx, *, t=512, tc=256, rb=256):
    bsz, s, d3 = qkv.shape
    d = d3 // 3
    nh = d // HEAD_DIM
    nt = s // t
    assert (nt * (nt + 1) // 2) % 2 == 0
    f32 = jnp.float32
    bf = jnp.bfloat16
    return pl.pallas_call(
        functools.partial(_fox_kernel, t=t, tc=tc, rb=rb),
        grid=(bsz, nh),
        in_specs=[
            pl.BlockSpec((1, s, HEAD_DIM), lambda b, h: (b, 0, h)),
            pl.BlockSpec((1, s, LANES), lambda b, h: (b, 0, 0)),
            pl.BlockSpec((1, s, HEAD_DIM), lambda b, h: (b, 0, nh + h)),
            pl.BlockSpec((1, s, LANES), lambda b, h: (b, 0, 0)),
            pl.BlockSpec((1, s, HEAD_DIM), lambda b, h: (b, 0, 2 * nh + h)),
        ],
        out_specs=pl.BlockSpec((1, s, HEAD_DIM), lambda b, h: (b, 0, h)),
        out_shape=jax.ShapeDtypeStruct((bsz, s, d), bf),
        scratch_shapes=[pltpu.VMEM((nt, t, HEAD_DIM + LANES), bf),
                        pltpu.VMEM((nt, t, HEAD_DIM + LANES), bf),
                        pltpu.VMEM((nt, HEAD_DIM, t), bf),
                        pltpu.VMEM((t, t), f32),
                        pltpu.VMEM((t, t), f32),
                        pltpu.VMEM((t, t), bf),
                        pltpu.VMEM((t, t), bf),
                        pltpu.VMEM((1, t), f32),
                        pltpu.VMEM((1, t), f32),
                        pltpu.VMEM((1, t), f32),
                        pltpu.VMEM((1, t), f32),
                        pltpu.VMEM((1, t), f32),
                        pltpu.VMEM((1, t), f32),
                        pltpu.VMEM((HEAD_DIM, t), f32)],
        compiler_params=_params(("arbitrary", "arbitrary")),
        name="fox_attention",
    )(qkv, qx, qkv, kx, qkv)


def _fox_from_fz(qkv, fz, b_f):
    qx, kx = _logf_cumsum(fz, b_f)
    return _fox_attention(qkv, qx, kx)


def _out_proj_kernel(h_ref, o_ref, w_ref, out_ref):
    out_ref[...] = h_ref[...] + jnp.dot(o_ref[...], w_ref[...],
                                        preferred_element_type=jnp.float32)


def _out_proj(h, o, w, *, tm=512, tn=1024):
    n, d = h.shape
    return pl.pallas_call(
        _out_proj_kernel,
        grid=(n // tm, d // tn),
        in_specs=[pl.BlockSpec((tm, tn), lambda i, j: (i, j)),
                  pl.BlockSpec((tm, d), lambda i, j: (i, 0)),
                  pl.BlockSpec((d, tn), lambda i, j: (0, j))],
        out_specs=pl.BlockSpec((tm, tn), lambda i, j: (i, j)),
        out_shape=jax.ShapeDtypeStruct((n, d), jnp.float32),
        compiler_params=_params(("arbitrary", "arbitrary")),
        name="out_proj",
    )(h, o, w)


def _mlp_kernel(h_ref, g_ref, w1_ref, w2_ref, out_ref, u_ref, acc_ref):
    f = pl.program_id(1)

    @pl.when(f == 0)
    def _():
        x = h_ref[...]
        r = lax.rsqrt(jnp.mean(x * x, axis=-1, keepdims=True) + EPS)
        u_ref[...] = (x * r * g_ref[...]).astype(u_ref.dtype)
        acc_ref[...] = jnp.zeros_like(acc_ref)

    a = jnp.dot(u_ref[...], w1_ref[...], preferred_element_type=jnp.float32)
    a = jnp.maximum(a, 0.0)
    a = (a * a).astype(w2_ref.dtype)
    acc_ref[...] += jnp.dot(a, w2_ref[...], preferred_element_type=jnp.float32)

    @pl.when(f == pl.num_programs(1) - 1)
    def _():
        out_ref[...] = h_ref[...] + acc_ref[...]


def _mlp(h, g, w1, w2, *, tm=512, tf=1024):
    n, d = h.shape
    dff = w1.shape[1]
    return pl.pallas_call(
        _mlp_kernel,
        grid=(n // tm, dff // tf),
        in_specs=[pl.BlockSpec((tm, d), lambda i, f: (i, 0)),
                  pl.BlockSpec((1, d), lambda i, f: (0, 0)),
                  pl.BlockSpec((d, tf), lambda i, f: (0, f)),
                  pl.BlockSpec((tf, d), lambda i, f: (f, 0))],
        out_specs=pl.BlockSpec((tm, d), lambda i, f: (i, 0)),
        out_shape=jax.ShapeDtypeStruct((n, d), jnp.float32),
        scratch_shapes=[pltpu.VMEM((tm, d), jnp.bfloat16),
                        pltpu.VMEM((tm, d), jnp.float32)],
        compiler_params=_params(("arbitrary", "arbitrary")),
        name="sq_relu_mlp",
    )(h, g.reshape(1, d), w1, w2)


_KB = _GROUP
_NBLK = _WIN // _KB


def _band_kernel(q_ref, k_ref, v_ref, bias_ref, o_ref,
                 vt_ref, s0_ref, s1_ref, p0_ref, p1_ref, mc0_ref, mc1_ref, l0_ref, l1_ref):
    ng = vt_ref.shape[0]
    f32 = jnp.float32
    s_refs, p_refs = (s0_ref, s1_ref), (p0_ref, p1_ref)
    mc_refs, l_refs = (mc0_ref, mc1_ref), (l0_ref, l1_ref)

    for j in range(ng):
        vt_ref[j] = v_ref[0, j * _KB:(j + 1) * _KB, :].T
    p1_ref[...] = jnp.zeros_like(p1_ref)
    l1_ref[...] = jnp.ones_like(l1_ref)

    def fold8(x, op):
        return op(x.reshape(x.shape[0] // 8, 8, x.shape[1]), axis=0)

    def rows_of(g):
        return pl.ds(pl.multiple_of(g * _GROUP, _GROUP), _GROUP)

    def logits(g, blk):
        kb = g - (_NBLK - 1) + blk
        kbc = jnp.maximum(kb, 0)
        s = lax.dot_general(k_ref[0, rows_of(kbc), :], q_ref[0, rows_of(g), :], _NT,
                            preferred_element_type=f32)
        return s + jnp.where(kb >= 0, bias_ref[0, blk], -jnp.inf)

    def values(g, blk, p_ref):
        kbc = jnp.maximum(g - (_NBLK - 1) + blk, 0)
        return jnp.dot(vt_ref[kbc], p_ref[blk], preferred_element_type=f32)

    mx = None
    for blk in range(_NBLK):
        s = logits(0, blk)
        s0_ref[blk] = s
        pm = fold8(s, jnp.max)
        mx = pm if mx is None else jnp.maximum(mx, pm)
    mc0_ref[...] = jnp.max(mx, axis=0, keepdims=True)

    def step(cur, g):
        nxt = 1 - cur
        g_n = jnp.minimum(g + 1, ng - 1)
        g_p = jnp.maximum(g - 1, 0)
        m = mc_refs[cur][...]
        mx = ls = pv = None
        for blk in range(_NBLK):
            s = logits(g_n, blk)
            s_refs[nxt][blk] = s
            pm = fold8(s, jnp.max)
            mx = pm if blk == 0 else jnp.maximum(mx, pm)
            p = jnp.exp2(s_refs[cur][blk] - m)
            ps = fold8(p, jnp.sum)
            ls = ps if blk == 0 else ls + ps
            p_refs[cur][blk] = p.astype(p_refs[cur].dtype)
            d = values(g_p, blk, p_refs[nxt])
            pv = d if blk == 0 else pv + d
        mc_refs[nxt][...] = jnp.max(mx, axis=0, keepdims=True)
        l_refs[cur][...] = jnp.sum(ls, axis=0, keepdims=True)
        o_ref[0, rows_of(g_p), :] = (pv / l_refs[nxt][...]).T.astype(o_ref.dtype)

    def loop_body(i, carry):
        step(0, 2 * i)
        step(1, 2 * i + 1)
        return carry

    lax.fori_loop(0, ng // 2, loop_body, 0)

    last = (ng - 1) % 2
    pv = None
    for blk in range(_NBLK):
        d = values(ng - 1, blk, p_refs[last])
        pv = d if blk == 0 else pv + d
    o_ref[0, (ng - 1) * _GROUP:, :] = (pv / l_refs[last][...]).T.astype(o_ref.dtype)


def _band_attention(q, kv, bias_t):
    bsz, s, d = q.shape
    nh = d // HEAD_DIM
    ng = s // _GROUP
    assert ng % 2 == 0
    f32 = jnp.float32
    bf = jnp.bfloat16
    return pl.pallas_call(
        _band_kernel,
        grid=(bsz, nh),
        in_specs=[
            pl.BlockSpec((1, s, HEAD_DIM), lambda b, h: (b, 0, h)),
            pl.BlockSpec((1, s, HEAD_DIM), lambda b, h: (b, 0, h)),
            pl.BlockSpec((1, s, HEAD_DIM), lambda b, h: (b, 0, nh + h)),
            pl.BlockSpec((1, _NBLK, _KB, _GROUP), lambda b, h: (h, 0, 0, 0)),
        ],
        out_specs=pl.BlockSpec((1, s, HEAD_DIM), lambda b, h: (b, 0, h)),
        out_shape=jax.ShapeDtypeStruct((bsz, s, d), bf),
        scratch_shapes=[pltpu.VMEM((ng, HEAD_DIM, _KB), bf),
                        pltpu.VMEM((_NBLK, _KB, _GROUP), f32),
                        pltpu.VMEM((_NBLK, _KB, _GROUP), f32),
                        pltpu.VMEM((_NBLK, _KB, _GROUP), bf),
                        pltpu.VMEM((_NBLK, _KB, _GROUP), bf),
                        pltpu.VMEM((1, _GROUP), f32),
                        pltpu.VMEM((1, _GROUP), f32),
                        pltpu.VMEM((1, _GROUP), f32),
                        pltpu.VMEM((1, _GROUP), f32)],
        compiler_params=_params(("arbitrary", "arbitrary")),
        name="band_attention",
    )(q, kv, kv, bias_t)


def _band_bias(rel_table):
    j = jnp.arange(_WIN)[:, None]
    i = jnp.arange(_GROUP)[None, :]
    km = j - (i // CHUNK) * CHUNK
    dist = _PAD + (i % CHUNK) - km
    idx = jnp.clip(dist, -(CHUNK - 1), REL_CLIP) + (CHUNK - 1)
    bias = rel_table[:, idx].astype(jnp.float32) * LOG2E
    bias = jnp.where((km >= 0) & (km < _BAND), bias, -jnp.inf)
    return bias.reshape(rel_table.shape[0], _NBLK, _KB, _GROUP)


def kernel(x, a_norm_g, a_w_in, a_b_f, a_q_g, a_k_g, a_w_out, mlp_norm_g, mlp_w1, mlp_w2,
           kv_norm_g, kv_w, kv_k_g, b_norm_g, b_w_q, b_q_g, b_rel, b_w_out):
    bsz, s, d = x.shape
    n = bsz * s
    nh = N_HEADS
    bf = jnp.bfloat16
    scale = HEAD_DIM ** -0.5
    ones = jnp.ones((HEAD_DIM,), jnp.float32)

    h = x.reshape(n, d)

    w_in = a_w_in[0]
    w_qkv = w_in[:, :3 * d].astype(bf)
    w_f = jnp.pad(w_in[:, 3 * d:], ((0, 0), (0, LANES - nh))).astype(bf)
    qkv, fz = _proj(h, a_norm_g[0], w_qkv, a_q_g[0] * (scale * LOG2E), a_k_g[0], w_f,
                    n_q_cols=d, n_k_cols=d)
    b_f = jnp.pad(a_b_f[0], (0, LANES - nh)).reshape(1, LANES)
    o = _fox_from_fz(qkv.reshape(bsz, s, 3 * d), fz.reshape(bsz, s, LANES), b_f)
    h = _out_proj(h, o.reshape(n, d), a_w_out[0].astype(bf))
    h = _mlp(h, mlp_norm_g[0], mlp_w1[0].astype(bf), mlp_w2[0].astype(bf))

    kv = _proj(h, kv_norm_g, kv_w.astype(bf), ones, kv_k_g, n_q_cols=0, n_k_cols=d)

    qb = _proj(h, b_norm_g[0], b_w_q[0].astype(bf), b_q_g[0] * (scale * LOG2E), ones,
               n_q_cols=d, n_k_cols=0)
    o = _band_attention(qb.reshape(bsz, s, d), kv.reshape(bsz, s, 2 * d),
                        _band_bias(b_rel[0]))
    h = _out_proj(h, o.reshape(n, d), b_w_out[0].astype(bf))
    h = _mlp(h, mlp_norm_g[1], mlp_w1[1].astype(bf), mlp_w2[1].astype(bf))
    return h.reshape(bsz, s, d)
```

```python
import functools
import math

import numpy as np

import jax
import jax.numpy as jnp
from jax import lax
from jax.experimental import pallas as pl
from jax.experimental.pallas import tpu as pltpu

N_HEADS = 16
HEAD_DIM = 128
CHUNK = 64
N_PREV_CHUNKS = 8
REL_CLIP = 256
EPS = 1e-6
LOG2E = math.log2(math.e)

LANES = 128
VMEM_LIMIT = 56 * 1024 * 1024

_BAND = (N_PREV_CHUNKS + 1) * CHUNK
_PAD = N_PREV_CHUNKS * CHUNK
_GROUP = 4 * CHUNK
_WIN = _PAD + _GROUP


def _params(sem):
    return pltpu.CompilerParams(dimension_semantics=sem, vmem_limit_bytes=VMEM_LIMIT)


def _proj_kernel(*refs, n_norms, split, has_f, tn):
    if has_f:
        x_ref, g_ref, w_ref, gain_ref, flag_ref, wf_ref, o_ref, f_ref, u_ref = refs
    else:
        x_ref, g_ref, w_ref, gain_ref, flag_ref, o_ref, u_ref = refs
    j = pl.program_id(1)

    @pl.when(j == 0)
    def _():
        x = x_ref[...]
        xn = x * lax.rsqrt(jnp.mean(x * x, axis=-1, keepdims=True) + EPS)
        for k in range(n_norms):
            u_ref[k] = (xn * g_ref[k:k + 1, :]).astype(u_ref.dtype)
        if has_f:
            f_ref[...] = jnp.dot(u_ref[0], wf_ref[...], preferred_element_type=jnp.float32)

    u = u_ref[(j >= split).astype(jnp.int32)] if n_norms > 1 else u_ref[0]
    y = jnp.dot(u, w_ref[...], preferred_element_type=jnp.float32)
    normed = flag_ref[0] > 0.0
    gain = gain_ref[0]
    for c in range(tn // HEAD_DIM):
        sl = slice(c * HEAD_DIM, (c + 1) * HEAD_DIM)
        ys = y[:, sl]
        r = lax.rsqrt(jnp.mean(ys * ys, axis=-1, keepdims=True) + EPS)
        o_ref[:, sl] = (ys * jnp.where(normed, r * gain, 1.0)).astype(o_ref.dtype)


def _proj(x, norm_gains, w, head_gains, wf=None, *, split_col=None, tm=512, tn=1024):
    n, d = x.shape
    nout = w.shape[1]
    n_tiles = nout // tn
    assert len(head_gains) == n_tiles
    has_f = wf is not None
    n_norms = len(norm_gains)
    gain_tab = jnp.stack([jnp.ones((HEAD_DIM,), jnp.float32) if hg is None else hg
                          for hg in head_gains]).reshape(n_tiles, 1, HEAD_DIM)
    flag_tab = jnp.asarray(np.array([[0.0 if hg is None else 1.0] * HEAD_DIM
                                     for hg in head_gains], np.float32)
                           ).reshape(n_tiles, 1, HEAD_DIM)
    in_specs = [
        pl.BlockSpec((tm, d), lambda i, j: (i, 0)),
        pl.BlockSpec((n_norms, d), lambda i, j: (0, 0)),
        pl.BlockSpec((d, tn), lambda i, j: (0, j)),
        pl.BlockSpec((1, 1, HEAD_DIM), lambda i, j: (j, 0, 0)),
        pl.BlockSpec((1, 1, HEAD_DIM), lambda i, j: (j, 0, 0)),
    ]
    args = [x, jnp.stack(norm_gains), w, gain_tab, flag_tab]
    out_shape = [jax.ShapeDtypeStruct((n, nout), jnp.bfloat16)]
    out_specs = [pl.BlockSpec((tm, tn), lambda i, j: (i, j))]
    if has_f:
        in_specs.append(pl.BlockSpec((d, LANES), lambda i, j: (0, 0)))
        args.append(wf)
        out_shape.append(jax.ShapeDtypeStruct((n, LANES), jnp.float32))
        out_specs.append(pl.BlockSpec((tm, LANES), lambda i, j: (i, 0)))
    res = pl.pallas_call(
        functools.partial(_proj_kernel, n_norms=n_norms,
                          split=n_tiles if split_col is None else split_col // tn,
                          has_f=has_f, tn=tn),
        grid=(n // tm, n_tiles),
        in_specs=in_specs,
        out_specs=out_specs,
        out_shape=out_shape,
        scratch_shapes=[pltpu.VMEM((n_norms, tm, d), jnp.bfloat16)],
        compiler_params=_params(("arbitrary", "arbitrary")),
        name="norm_proj",
    )(*args)
    return res if has_f else res[0]


_HEAD_LANES = LANES // N_HEADS
_N_TERMS = 3


def _bias_column_constants():
    sel = np.zeros((_N_TERMS * LANES, 2 * LANES), np.float32)
    one = np.zeros((1, 2 * LANES), np.float32)
    for h in range(N_HEADS):
        for term in range(_N_TERMS):
            sel[term * LANES + h, h * _HEAD_LANES + term] = -1.0
            sel[term * LANES + h, LANES + h * _HEAD_LANES + _N_TERMS + term] = 1.0
            one[0, h * _HEAD_LANES + _N_TERMS + term] = 1.0
            one[0, LANES + h * _HEAD_LANES + term] = 1.0
    return jnp.asarray(sel, jnp.bfloat16), jnp.asarray(one, jnp.float32)


def _logf_cumsum_kernel(f_ref, b_ref, sel_ref, one_ref, qx_ref, kx_ref):
    z = f_ref[0] + b_ref[...]
    c = jnp.minimum(z, 0.0) - jnp.log1p(jnp.exp(-jnp.abs(z)))
    s = c.shape[0]
    row = lax.broadcasted_iota(jnp.int32, c.shape, 0)
    d = 1
    while d < s:
        c = c + jnp.where(row >= d, pltpu.roll(c, d, axis=0), 0.0)
        d *= 2
    c = c * LOG2E
    hi = c.astype(jnp.bfloat16)
    r = c - hi.astype(jnp.float32)
    mid = r.astype(jnp.bfloat16)
    lo = (r - mid.astype(jnp.float32)).astype(jnp.bfloat16)
    terms = jnp.concatenate([hi, mid, lo], axis=1)
    y = jnp.dot(terms, sel_ref[...], preferred_element_type=jnp.float32) + one_ref[...]
    kx_ref[0] = y[:, :LANES].astype(kx_ref.dtype)
    qx_ref[0] = y[:, LANES:].astype(qx_ref.dtype)


def _logf_cumsum(fz, b):
    bsz, s, _ = fz.shape
    sel, one = _bias_column_constants()
    spec = pl.BlockSpec((1, s, LANES), lambda i: (i, 0, 0))
    out = jax.ShapeDtypeStruct((bsz, s, LANES), jnp.bfloat16)
    return pl.pallas_call(
        _logf_cumsum_kernel,
        grid=(bsz,),
        in_specs=[spec, pl.BlockSpec((1, LANES), lambda i: (0, 0)),
                  pl.BlockSpec(sel.shape, lambda i: (0, 0)),
                  pl.BlockSpec(one.shape, lambda i: (0, 0))],
        out_specs=[spec, spec],
        out_shape=[out, out],
        compiler_params=_params(("arbitrary",)),
        name="logf_cumsum",
    )(fz, b, sel, one)


_NT = (((1,), (1,)), ((), ()))


def _fox_kernel(q_ref, qx_ref, k_ref, kx_ref, v_ref, o_ref,
                qa_ref, ka_ref, vt_ref, s0_ref, s1_ref, p0_ref, p1_ref, mc0_ref, mc1_ref,
                m_ref, l_ref, lp_ref, al_ref, acc_ref, *, t, tc, rb):
    nt = ka_ref.shape[0]
    n_steps = nt * (nt + 1) // 2
    nc = t // tc
    nr = t // rb
    f32 = jnp.float32
    s_refs, p_refs, mc_refs = (s0_ref, s1_ref), (p0_ref, p1_ref), (mc0_ref, mc1_ref)

    lane = lax.broadcasted_iota(jnp.int32, (t, LANES), 1)
    own_lanes = lane // _HEAD_LANES == pl.program_id(1)

    for j in range(nt):
        rows = slice(j * t, (j + 1) * t)
        ka_ref[j, :, 0:HEAD_DIM] = k_ref[0, rows, :]
        ka_ref[j, :, HEAD_DIM:] = kx_ref[0, rows, :]
        qa_ref[j, :, 0:HEAD_DIM] = q_ref[0, rows, :]
        qx = qx_ref[0, rows, :]
        qa_ref[j, :, HEAD_DIM:] = jnp.where(own_lanes, qx, jnp.zeros_like(qx))
        vt_ref[j] = v_ref[0, rows, :].T
    p1_ref[...] = jnp.zeros_like(p1_ref)
    l_ref[...] = jnp.zeros_like(l_ref)
    al_ref[...] = jnp.zeros_like(al_ref)
    acc_ref[...] = jnp.zeros_like(acc_ref)
    m_ref[...] = jnp.full_like(m_ref, -jnp.inf)

    def logits(qi_n, kv_n, r, c, masked):
        s = lax.dot_general(ka_ref[kv_n, r * rb:(r + 1) * rb, :],
                            qa_ref[qi_n, c * tc:(c + 1) * tc, :], _NT,
                            preferred_element_type=f32)
        if masked:
            key = lax.broadcasted_iota(jnp.int32, s.shape, 0) + r * rb
            qry = lax.broadcasted_iota(jnp.int32, s.shape, 1) + c * tc
            s = jnp.where(key <= qry, s, -jnp.inf)
        return s

    def fold8(x, op):
        return op(x.reshape(x.shape[0] // 8, 8, x.shape[1]), axis=0)

    def finalize(qb, l):
        o_ref[0, pl.ds(pl.multiple_of(qb * t, t), t), :] = (
            acc_ref[...] / l).T.astype(o_ref.dtype)

    for c in range(nc):
        mx = None
        for r in range(nr):
            s = logits(0, 0, r, c, True)
            s0_ref[r * rb:(r + 1) * rb, c * tc:(c + 1) * tc] = s
            pm = fold8(s, jnp.max)
            mx = pm if mx is None else jnp.maximum(mx, pm)
        mc0_ref[:, c * tc:(c + 1) * tc] = jnp.max(mx, axis=0, keepdims=True)

    def body(cur, qi, kv, qi_p, kv_p, qi_n, kv_n, masked_next):
        nxt = 1 - cur
        s_cur, s_nxt = s_refs[cur], s_refs[nxt]
        p_cur, p_prev = p_refs[cur], p_refs[nxt]
        kvp = jnp.maximum(kv_p, 0)
        m_old = jnp.where(kv == 0, -jnp.inf, m_ref[...])
        m_new = jnp.maximum(m_old, mc_refs[cur][...])
        alpha = jnp.exp2(m_old - m_new)
        alpha_prev = al_ref[...]
        mx = [None] * nc
        ls = [None] * nc
        pv = [None] * nc
        for r in range(nr):
            rows = slice(r * rb, (r + 1) * rb)
            for c in range(nc):
                cols = slice(c * tc, (c + 1) * tc)
                s = logits(qi_n, kv_n, r, c, masked_next)
                s_nxt[rows, cols] = s
                pm = fold8(s, jnp.max)
                mx[c] = pm if r == 0 else jnp.maximum(mx[c], pm)
                p = jnp.exp2(s_cur[rows, cols] - m_new[:, cols])
                ps = fold8(p, jnp.sum)
                ls[c] = ps if r == 0 else ls[c] + ps
                p_cur[rows, cols] = p.astype(p_cur.dtype)
                d = jnp.dot(vt_ref[kvp, :, rows], p_prev[rows, cols],
                            preferred_element_type=f32)
                pv[c] = d if r == 0 else pv[c] + d
        for c in range(nc):
            cols = slice(c * tc, (c + 1) * tc)
            mc_refs[nxt][:, cols] = jnp.max(mx[c], axis=0, keepdims=True)
            l_old = l_ref[:, cols]
            lp_ref[:, cols] = l_old
            l_ref[:, cols] = alpha[:, cols] * l_old + jnp.sum(ls[c], axis=0, keepdims=True)
            acc_ref[:, cols] = alpha_prev[:, cols] * acc_ref[:, cols] + pv[c]
        m_ref[...] = m_new
        al_ref[...] = alpha

        @pl.when(kv_p == qi_p)
        def _():
            finalize(qi_p, lp_ref[...])

    def one_step(cur, qi, kv, qi_p, kv_p):
        wrap = kv + 1 > qi
        qi_n = jnp.where(wrap, qi + 1, qi)
        kv_n = jnp.where(wrap, 0, kv + 1)
        qi_c = jnp.minimum(qi_n, nt - 1)
        diag = kv_n == qi_n

        @pl.when(diag)
        def _():
            body(cur, qi, kv, qi_p, kv_p, qi_c, kv_n, True)

        @pl.when(jnp.logical_not(diag))
        def _():
            body(cur, qi, kv, qi_p, kv_p, qi_c, kv_n, False)

        return qi_n, kv_n, qi, kv

    def loop_body(_, carry):
        return one_step(1, *one_step(0, *carry))

    zero = jnp.int32(0)
    lax.fori_loop(0, n_steps // 2, loop_body, (zero, zero, zero, jnp.int32(-1)))

    alpha_prev = al_ref[...]
    for c in range(nc):
        cols = slice(c * tc, (c + 1) * tc)
        d = jnp.dot(vt_ref[nt - 1], p_refs[(n_steps - 1) % 2][:, cols],
                    preferred_element_type=f32)
        acc_ref[:, cols] = alpha_prev[:, cols] * acc_ref[:, cols] + d
    finalize(nt - 1, l_ref[...])


def _fox_attention(qkv, qx, kx, *, t=512, tc=256, rb=256):
    bsz, s, d3 = qkv.shape
    d = d3 // 3
    nh = d // HEAD_DIM
    nt = s // t
    assert (nt * (nt + 1) // 2) % 2 == 0
    f32 = jnp.float32
    bf = jnp.bfloat16
    return pl.pallas_call(
        functools.partial(_fox_kernel, t=t, tc=tc, rb=rb),
        grid=(bsz, nh),
        in_specs=[
            pl.BlockSpec((1, s, HEAD_DIM), lambda b, h: (b, 0, h)),
            pl.BlockSpec((1, s, LANES), lambda b, h: (b, 0, 0)),
            pl.BlockSpec((1, s, HEAD_DIM), lambda b, h: (b, 0, nh + h)),
            pl.BlockSpec((1, s, LANES), lambda b, h: (b, 0, 0)),
            pl.BlockSpec((1, s, HEAD_DIM), lambda b, h: (b, 0, 2 * nh + h)),
        ],
        out_specs=pl.BlockSpec((1, s, HEAD_DIM), lambda b, h: (b, 0, h)),
        out_shape=jax.ShapeDtypeStruct((bsz, s, d), bf),
        scratch_shapes=[pltpu.VMEM((nt, t, HEAD_DIM + LANES), bf),
                        pltpu.VMEM((nt, t, HEAD_DIM + LANES), bf),
                        pltpu.VMEM((nt, HEAD_DIM, t), bf),
                        pltpu.VMEM((t, t), f32),
                        pltpu.VMEM((t, t), f32),
                        pltpu.VMEM((t, t), bf),
                        pltpu.VMEM((t, t), bf),
                        pltpu.VMEM((1, t), f32),
                        pltpu.VMEM((1, t), f32),
                        pltpu.VMEM((1, t), f32),
                        pltpu.VMEM((1, t), f32),
                        pltpu.VMEM((1, t), f32),
                        pltpu.VMEM((1, t), f32),
                        pltpu.VMEM((HEAD_DIM, t), f32)],
        compiler_params=_params(("arbitrary", "arbitrary")),
        name="fox_attention",
    )(qkv, qx, qkv, kx, qkv)


def _fox_from_fz(qkv, fz, b_f):
    qx, kx = _logf_cumsum(fz, b_f)
    return _fox_attention(qkv, qx, kx)


def _out_proj_kernel(h_ref, o_ref, w_ref, out_ref):
    out_ref[...] = h_ref[...] + jnp.dot(o_ref[...], w_ref[...],
                                        preferred_element_type=jnp.float32)


def _out_proj(h, o, w, *, tm=512):
    n, d = h.shape
    return pl.pallas_call(
        _out_proj_kernel,
        grid=(n // tm,),
        in_specs=[pl.BlockSpec((tm, d), lambda i: (i, 0)),
                  pl.BlockSpec((tm, d), lambda i: (i, 0)),
                  pl.BlockSpec((d, d), lambda i: (0, 0))],
        out_specs=pl.BlockSpec((tm, d), lambda i: (i, 0)),
        out_shape=jax.ShapeDtypeStruct((n, d), jnp.float32),
        compiler_params=_params(("arbitrary",)),
        name="out_proj",
    )(h, o, w)


def _mlp_kernel(h_ref, g_ref, w1_ref, w2_ref, out_ref, u_ref, acc_ref):
    f = pl.program_id(1)

    @pl.when(f == 0)
    def _():
        x = h_ref[...]
        r = lax.rsqrt(jnp.mean(x * x, axis=-1, keepdims=True) + EPS)
        u_ref[...] = (x * r * g_ref[...]).astype(u_ref.dtype)
        acc_ref[...] = jnp.zeros_like(acc_ref)

    a = jnp.dot(u_ref[...], w1_ref[...], preferred_element_type=jnp.float32)
    a = jnp.maximum(a, 0.0)
    a = (a * a).astype(w2_ref.dtype)
    acc_ref[...] += jnp.dot(a, w2_ref[...], preferred_element_type=jnp.float32)

    @pl.when(f == pl.num_programs(1) - 1)
    def _():
        out_ref[...] = h_ref[...] + acc_ref[...]


def _mlp(h, g, w1, w2, *, tm=512, tf=1024):
    n, d = h.shape
    dff = w1.shape[1]
    return pl.pallas_call(
        _mlp_kernel,
        grid=(n // tm, dff // tf),
        in_specs=[pl.BlockSpec((tm, d), lambda i, f: (i, 0)),
                  pl.BlockSpec((1, d), lambda i, f: (0, 0)),
                  pl.BlockSpec((d, tf), lambda i, f: (0, f)),
                  pl.BlockSpec((tf, d), lambda i, f: (f, 0))],
        out_specs=pl.BlockSpec((tm, d), lambda i, f: (i, 0)),
        out_shape=jax.ShapeDtypeStruct((n, d), jnp.float32),
        scratch_shapes=[pltpu.VMEM((tm, d), jnp.bfloat16),
                        pltpu.VMEM((tm, d), jnp.float32)],
        compiler_params=_params(("arbitrary", "arbitrary")),
        name="sq_relu_mlp",
    )(h, g.reshape(1, d), w1, w2)


_KB = _GROUP
_NBLK = _WIN // _KB


def _band_kernel(q_ref, k_ref, v_ref, bias_ref, o_ref,
                 vt_ref, s0_ref, s1_ref, p0_ref, p1_ref, mc0_ref, mc1_ref, l0_ref, l1_ref):
    ng = vt_ref.shape[0]
    f32 = jnp.float32
    s_refs, p_refs = (s0_ref, s1_ref), (p0_ref, p1_ref)
    mc_refs, l_refs = (mc0_ref, mc1_ref), (l0_ref, l1_ref)

    for j in range(ng):
        vt_ref[j] = v_ref[0, j * _KB:(j + 1) * _KB, :].T
    p1_ref[...] = jnp.zeros_like(p1_ref)
    l1_ref[...] = jnp.ones_like(l1_ref)

    def fold8(x, op):
        return op(x.reshape(x.shape[0] // 8, 8, x.shape[1]), axis=0)

    def rows_of(g):
        return pl.ds(pl.multiple_of(g * _GROUP, _GROUP), _GROUP)

    def logits(g, blk):
        kb = g - (_NBLK - 1) + blk
        kbc = jnp.maximum(kb, 0)
        s = lax.dot_general(k_ref[0, rows_of(kbc), :], q_ref[0, rows_of(g), :], _NT,
                            preferred_element_type=f32)
        return s + jnp.where(kb >= 0, bias_ref[0, blk], -jnp.inf)

    def values(g, blk, p_ref):
        kbc = jnp.maximum(g - (_NBLK - 1) + blk, 0)
        return jnp.dot(vt_ref[kbc], p_ref[blk], preferred_element_type=f32)

    mx = None
    for blk in range(_NBLK):
        s = logits(0, blk)
        s0_ref[blk] = s
        pm = fold8(s, jnp.max)
        mx = pm if mx is None else jnp.maximum(mx, pm)
    mc0_ref[...] = jnp.max(mx, axis=0, keepdims=True)

    def step(cur, g):
        nxt = 1 - cur
        g_n = jnp.minimum(g + 1, ng - 1)
        g_p = jnp.maximum(g - 1, 0)
        m = mc_refs[cur][...]
        mx = ls = pv = None
        for blk in range(_NBLK):
            s = logits(g_n, blk)
            s_refs[nxt][blk] = s
            pm = fold8(s, jnp.max)
            mx = pm if blk == 0 else jnp.maximum(mx, pm)
            p = jnp.exp2(s_refs[cur][blk] - m)
            ps = fold8(p, jnp.sum)
            ls = ps if blk == 0 else ls + ps
            p_refs[cur][blk] = p.astype(p_refs[cur].dtype)
            d = values(g_p, blk, p_refs[nxt])
            pv = d if blk == 0 else pv + d
        mc_refs[nxt][...] = jnp.max(mx, axis=0, keepdims=True)
        l_refs[cur][...] = jnp.sum(ls, axis=0, keepdims=True)
        o_ref[0, rows_of(g_p), :] = (pv / l_refs[nxt][...]).T.astype(o_ref.dtype)

    def loop_body(i, carry):
        step(0, 2 * i)
        step(1, 2 * i + 1)
        return carry

    lax.fori_loop(0, ng // 2, loop_body, 0)

    last = (ng - 1) % 2
    pv = None
    for blk in range(_NBLK):
        d = values(ng - 1, blk, p_refs[last])
        pv = d if blk == 0 else pv + d
    o_ref[0, (ng - 1) * _GROUP:, :] = (pv / l_refs[last][...]).T.astype(o_ref.dtype)


def _band_attention(kvq, bias_t):
    bsz, s, d3 = kvq.shape
    d = d3 // 3
    nh = d // HEAD_DIM
    ng = s // _GROUP
    assert ng % 2 == 0
    f32 = jnp.float32
    bf = jnp.bfloat16
    return pl.pallas_call(
        _band_kernel,
        grid=(bsz, nh),
        in_specs=[
            pl.BlockSpec((1, s, HEAD_DIM), lambda b, h: (b, 0, 2 * nh + h)),
            pl.BlockSpec((1, s, HEAD_DIM), lambda b, h: (b, 0, h)),
            pl.BlockSpec((1, s, HEAD_DIM), lambda b, h: (b, 0, nh + h)),
            pl.BlockSpec((1, _NBLK, _KB, _GROUP), lambda b, h: (h, 0, 0, 0)),
        ],
        out_specs=pl.BlockSpec((1, s, HEAD_DIM), lambda b, h: (b, 0, h)),
        out_shape=jax.ShapeDtypeStruct((bsz, s, d), bf),
        scratch_shapes=[pltpu.VMEM((ng, HEAD_DIM, _KB), bf),
                        pltpu.VMEM((_NBLK, _KB, _GROUP), f32),
                        pltpu.VMEM((_NBLK, _KB, _GROUP), f32),
                        pltpu.VMEM((_NBLK, _KB, _GROUP), bf),
                        pltpu.VMEM((_NBLK, _KB, _GROUP), bf),
                        pltpu.VMEM((1, _GROUP), f32),
                        pltpu.VMEM((1, _GROUP), f32),
                        pltpu.VMEM((1, _GROUP), f32),
                        pltpu.VMEM((1, _GROUP), f32)],
        compiler_params=_params(("arbitrary", "arbitrary")),
        name="band_attention",
    )(kvq, kvq, kvq, bias_t)


def _band_bias(rel_table):
    nh = rel_table.shape[0]
    span = _WIN + _GROUP
    k = jnp.arange(span)
    idx = jnp.clip(k - (_WIN - 1) + _PAD, -(CHUNK - 1), REL_CLIP) + (CHUNK - 1)
    g = rel_table[:, idx].astype(jnp.float32) * LOG2E
    bias = jnp.tile(g, (1, _WIN))[:, :_WIN * (span - 1)].reshape(nh, _WIN, span - 1)
    bias = bias[:, :, _WIN - 1:]
    j = jnp.arange(_WIN)[:, None]
    i = jnp.arange(_GROUP)[None, :]
    km = j - (i // CHUNK) * CHUNK
    bias = jnp.where((km >= 0) & (km < _BAND), bias, -jnp.inf)
    return bias.reshape(nh, _NBLK, _KB, _GROUP)


def kernel(x, a_norm_g, a_w_in, a_b_f, a_q_g, a_k_g, a_w_out, mlp_norm_g, mlp_w1, mlp_w2,
           kv_norm_g, kv_w, kv_k_g, b_norm_g, b_w_q, b_q_g, b_rel, b_w_out):
    bsz, s, d = x.shape
    n = bsz * s
    nh = N_HEADS
    bf = jnp.bfloat16
    q_scale = HEAD_DIM ** -0.5 * LOG2E
    tn = 2048
    per_part = d // tn

    h = x.reshape(n, d)

    w_in = a_w_in[0]
    w_qkv = w_in[:, :3 * d].astype(bf)
    w_f = jnp.pad(w_in[:, 3 * d:], ((0, 0), (0, LANES - nh))).astype(bf)
    qkv, fz = _proj(h, [a_norm_g[0]], w_qkv,
                    [a_q_g[0] * q_scale] * per_part + [a_k_g[0]] * per_part + [None] * per_part,
                    w_f, tn=tn)
    b_f = jnp.pad(a_b_f[0], (0, LANES - nh)).reshape(1, LANES)
    o = _fox_from_fz(qkv.reshape(bsz, s, 3 * d), fz.reshape(bsz, s, LANES), b_f)
    h = _out_proj(h, o.reshape(n, d), a_w_out[0].astype(bf))
    h = _mlp(h, mlp_norm_g[0], mlp_w1[0].astype(bf), mlp_w2[0].astype(bf))

    w_kvq = jnp.concatenate([kv_w.astype(bf), b_w_q[0].astype(bf)], axis=1)
    kvq = _proj(h, [kv_norm_g, b_norm_g[0]], w_kvq,
                [kv_k_g] * per_part + [None] * per_part + [b_q_g[0] * q_scale] * per_part,
                split_col=2 * d, tn=tn)

    o = _band_attention(kvq.reshape(bsz, s, 3 * d), _band_bias(b_rel[0]))
    h = _out_proj(h, o.reshape(n, d), b_w_out[0].astype(bf))
    h = _mlp(h, mlp_norm_g[1], mlp_w1[1].astype(bf), mlp_w2[1].astype(bf))
    return h.reshape(bsz, s, d)
```

```python
import functools
import math

import numpy as np

import jax
import jax.numpy as jnp
from jax import lax
from jax.experimental import pallas as pl
from jax.experimental.pallas import tpu as pltpu

N_HEADS = 16
HEAD_DIM = 128
CHUNK = 64
N_PREV_CHUNKS = 8
REL_CLIP = 256
EPS = 1e-6
LOG2E = math.log2(math.e)

LANES = 128
VMEM_LIMIT = 56 * 1024 * 1024

_BAND = (N_PREV_CHUNKS + 1) * CHUNK
_PAD = N_PREV_CHUNKS * CHUNK
_GROUP = 4 * CHUNK
_WIN = _PAD + _GROUP


def _params(sem):
    return pltpu.CompilerParams(dimension_semantics=sem, vmem_limit_bytes=VMEM_LIMIT)


def _proj_kernel(*refs, n_norms, split, has_f, tn):
    if has_f:
        x_ref, g_ref, w_ref, gain_ref, flag_ref, wf_ref, o_ref, f_ref, u_ref = refs
    else:
        x_ref, g_ref, w_ref, gain_ref, flag_ref, o_ref, u_ref = refs
    j = pl.program_id(1)

    @pl.when(j == 0)
    def _():
        x = x_ref[...]
        xn = x * lax.rsqrt(jnp.mean(x * x, axis=-1, keepdims=True) + EPS)
        for k in range(n_norms):
            u_ref[k] = (xn * g_ref[k:k + 1, :]).astype(u_ref.dtype)
        if has_f:
            f_ref[...] = jnp.dot(u_ref[0], wf_ref[...], preferred_element_type=jnp.float32)

    u = u_ref[(j >= split).astype(jnp.int32)] if n_norms > 1 else u_ref[0]
    y = jnp.dot(u, w_ref[...], preferred_element_type=jnp.float32)
    normed = flag_ref[0] > 0.0
    gain = gain_ref[0]
    for c in range(tn // HEAD_DIM):
        sl = slice(c * HEAD_DIM, (c + 1) * HEAD_DIM)
        ys = y[:, sl]
        r = lax.rsqrt(jnp.mean(ys * ys, axis=-1, keepdims=True) + EPS)
        o_ref[:, sl] = (ys * jnp.where(normed, r * gain, 1.0)).astype(o_ref.dtype)


def _proj(x, norm_gains, w, head_gains, wf=None, *, split_col=None, tm=512, tn=1024):
    n, d = x.shape
    nout = w.shape[1]
    n_tiles = nout // tn
    assert len(head_gains) == n_tiles
    has_f = wf is not None
    n_norms = len(norm_gains)
    gain_tab = jnp.stack([jnp.ones((HEAD_DIM,), jnp.float32) if hg is None else hg
                          for hg in head_gains]).reshape(n_tiles, 1, HEAD_DIM)
    flag_tab = jnp.asarray(np.array([[0.0 if hg is None else 1.0] * HEAD_DIM
                                     for hg in head_gains], np.float32)
                           ).reshape(n_tiles, 1, HEAD_DIM)
    in_specs = [
        pl.BlockSpec((tm, d), lambda i, j: (i, 0)),
        pl.BlockSpec((n_norms, d), lambda i, j: (0, 0)),
        pl.BlockSpec((d, tn), lambda i, j: (0, j)),
        pl.BlockSpec((1, 1, HEAD_DIM), lambda i, j: (j, 0, 0)),
        pl.BlockSpec((1, 1, HEAD_DIM), lambda i, j: (j, 0, 0)),
    ]
    args = [x, jnp.stack(norm_gains), w, gain_tab, flag_tab]
    out_shape = [jax.ShapeDtypeStruct((n, nout), jnp.bfloat16)]
    out_specs = [pl.BlockSpec((tm, tn), lambda i, j: (i, j))]
    if has_f:
        in_specs.append(pl.BlockSpec((d, LANES), lambda i, j: (0, 0)))
        args.append(wf)
        out_shape.append(jax.ShapeDtypeStruct((n, LANES), jnp.float32))
        out_specs.append(pl.BlockSpec((tm, LANES), lambda i, j: (i, 0)))
    res = pl.pallas_call(
        functools.partial(_proj_kernel, n_norms=n_norms,
                          split=n_tiles if split_col is None else split_col // tn,
                          has_f=has_f, tn=tn),
        grid=(n // tm, n_tiles),
        in_specs=in_specs,
        out_specs=out_specs,
        out_shape=out_shape,
        scratch_shapes=[pltpu.VMEM((n_norms, tm, d), jnp.bfloat16)],
        compiler_params=_params(("arbitrary", "arbitrary")),
        name="norm_proj",
    )(*args)
    return res if has_f else res[0]


_HEAD_LANES = LANES // N_HEADS
_N_TERMS = 3


def _bias_column_constants():
    sel = np.zeros((_N_TERMS * LANES, 2 * LANES), np.float32)
    one = np.zeros((1, 2 * LANES), np.float32)
    for h in range(N_HEADS):
        for term in range(_N_TERMS):
            sel[term * LANES + h, h * _HEAD_LANES + term] = -1.0
            sel[term * LANES + h, LANES + h * _HEAD_LANES + _N_TERMS + term] = 1.0
            one[0, h * _HEAD_LANES + _N_TERMS + term] = 1.0
            one[0, LANES + h * _HEAD_LANES + term] = 1.0
    return jnp.asarray(sel, jnp.bfloat16), jnp.asarray(one, jnp.float32)


def _logf_cumsum_kernel(f_ref, b_ref, sel_ref, one_ref, qx_ref, kx_ref):
    z = f_ref[0] + b_ref[...]
    c = jnp.minimum(z, 0.0) - jnp.log1p(jnp.exp(-jnp.abs(z)))
    s = c.shape[0]
    row = lax.broadcasted_iota(jnp.int32, c.shape, 0)
    d = 1
    while d < s:
        c = c + jnp.where(row >= d, pltpu.roll(c, d, axis=0), 0.0)
        d *= 2
    c = c * LOG2E
    hi = c.astype(jnp.bfloat16)
    r = c - hi.astype(jnp.float32)
    mid = r.astype(jnp.bfloat16)
    lo = (r - mid.astype(jnp.float32)).astype(jnp.bfloat16)
    terms = jnp.concatenate([hi, mid, lo], axis=1)
    y = jnp.dot(terms, sel_ref[...], preferred_element_type=jnp.float32) + one_ref[...]
    kx_ref[0] = y[:, :LANES].astype(kx_ref.dtype)
    qx_ref[0] = y[:, LANES:].astype(qx_ref.dtype)


def _logf_cumsum(fz, b):
    bsz, s, _ = fz.shape
    sel, one = _bias_column_constants()
    spec = pl.BlockSpec((1, s, LANES), lambda i: (i, 0, 0))
    out = jax.ShapeDtypeStruct((bsz, s, LANES), jnp.bfloat16)
    return pl.pallas_call(
        _logf_cumsum_kernel,
        grid=(bsz,),
        in_specs=[spec, pl.BlockSpec((1, LANES), lambda i: (0, 0)),
                  pl.BlockSpec(sel.shape, lambda i: (0, 0)),
                  pl.BlockSpec(one.shape, lambda i: (0, 0))],
        out_specs=[spec, spec],
        out_shape=[out, out],
        compiler_params=_params(("arbitrary",)),
        name="logf_cumsum",
    )(fz, b, sel, one)


_NT = (((1,), (1,)), ((), ()))


def _fox_kernel(q_ref, qx_ref, k_ref, kx_ref, v_ref, o_ref,
                qa_ref, ka_ref, vt_ref, s0_ref, s1_ref, p0_ref, p1_ref, acc_ref, *, t, tc, rb):
    nt = ka_ref.shape[0]
    nc = t // tc
    nr = t // rb
    f32 = jnp.float32
    s_refs, p_refs = (s0_ref, s1_ref), (p0_ref, p1_ref)
    tiles = [(qi, kv) for qi in range(nt) for kv in range(qi + 1)]

    lane = lax.broadcasted_iota(jnp.int32, (t, LANES), 1)
    own_lanes = lane // _HEAD_LANES == pl.program_id(1)

    for j in range(nt):
        rows = slice(j * t, (j + 1) * t)
        ka_ref[j, :, 0:HEAD_DIM] = k_ref[0, rows, :]
        ka_ref[j, :, HEAD_DIM:] = kx_ref[0, rows, :]
        qa_ref[j, :, 0:HEAD_DIM] = q_ref[0, rows, :]
        qx = qx_ref[0, rows, :]
        qa_ref[j, :, HEAD_DIM:] = jnp.where(own_lanes, qx, jnp.zeros_like(qx))
        vt_ref[j] = v_ref[0, rows, :].T

    def fold8(x, op):
        return op(x.reshape(x.shape[0] // 8, 8, x.shape[1]), axis=0)

    def logits_block(tile, r, c, s_out):
        qi, kv = tile
        s = lax.dot_general(ka_ref[kv, r * rb:(r + 1) * rb, :],
                            qa_ref[qi, c * tc:(c + 1) * tc, :], _NT,
                            preferred_element_type=f32)
        if kv == qi:
            key = lax.broadcasted_iota(jnp.int32, s.shape, 0) + r * rb
            qry = lax.broadcasted_iota(jnp.int32, s.shape, 1) + c * tc
            s = jnp.where(key <= qry, s, -jnp.inf)
        s_out[r * rb:(r + 1) * rb, c * tc:(c + 1) * tc] = s
        return fold8(s, jnp.max)

    blocks = [(r, c) for r in range(nr) for c in range(nc)]

    def tile_max(mx):
        return [jnp.max(mx[c], axis=0, keepdims=True) for c in range(nc)]

    mx = [None] * nc
    for r, c in blocks:
        pm = logits_block(tiles[0], r, c, s_refs[0])
        mx[c] = pm if mx[c] is None else jnp.maximum(mx[c], pm)
    mc = tile_max(mx)

    m_run = l_run = alpha_prev = None
    for n, (qi, kv) in enumerate(tiles):
        cur, nxt = n % 2, 1 - n % 2
        nxt_tile = tiles[n + 1] if n + 1 < len(tiles) else None
        prv_tile = tiles[n - 1] if n > 0 else None
        if kv == 0:
            m_new = mc
            alpha = None
        else:
            m_new = [jnp.maximum(m_run[c], mc[c]) for c in range(nc)]
            alpha = [jnp.exp2(m_run[c] - m_new[c]) for c in range(nc)]
        mx = [None] * nc
        ls = [None] * nc
        pv = [None] * nc
        for r, c in blocks:
            rows = slice(r * rb, (r + 1) * rb)
            cols = slice(c * tc, (c + 1) * tc)
            if nxt_tile is not None:
                pm = logits_block(nxt_tile, r, c, s_refs[nxt])
                mx[c] = pm if mx[c] is None else jnp.maximum(mx[c], pm)
            p = jnp.exp2(s_refs[cur][rows, cols] - m_new[c])
            ps = fold8(p, jnp.sum)
            ls[c] = ps if ls[c] is None else ls[c] + ps
            p_refs[cur][rows, cols] = p.astype(p_refs[cur].dtype)
            if prv_tile is not None:
                d = jnp.dot(vt_ref[prv_tile[1], :, rows], p_refs[nxt][rows, cols],
                            preferred_element_type=f32)
                pv[c] = d if pv[c] is None else pv[c] + d
        lsum = [jnp.sum(ls[c], axis=0, keepdims=True) for c in range(nc)]
        l_prev_block = l_run
        l_run = lsum if kv == 0 else [alpha[c] * l_run[c] + lsum[c] for c in range(nc)]
        if prv_tile is not None:
            for c in range(nc):
                cols = slice(c * tc, (c + 1) * tc)
                if prv_tile[1] == 0:
                    acc_ref[:, cols] = pv[c]
                else:
                    acc_ref[:, cols] = alpha_prev[c] * acc_ref[:, cols] + pv[c]
            if prv_tile[1] == prv_tile[0]:
                l_fin = jnp.concatenate(l_prev_block, axis=1)
                o_ref[0, prv_tile[0] * t:(prv_tile[0] + 1) * t, :] = (
                    acc_ref[...] / l_fin).T.astype(o_ref.dtype)
        m_run, alpha_prev = m_new, alpha
        if nxt_tile is not None:
            mc = tile_max(mx)

    last = len(tiles) - 1
    qi, kv = tiles[last]
    for c in range(nc):
        cols = slice(c * tc, (c + 1) * tc)
        d = jnp.dot(vt_ref[kv], p_refs[last % 2][:, cols], preferred_element_type=f32)
        acc_ref[:, cols] = d if kv == 0 else alpha_prev[c] * acc_ref[:, cols] + d
    o_ref[0, qi * t:(qi + 1) * t, :] = (
        acc_ref[...] / jnp.concatenate(l_run, axis=1)).T.astype(o_ref.dtype)


def _fox_attention(qkv, qx, kx, *, t=512, tc=256, rb=256):
    bsz, s, d3 = qkv.shape
    d = d3 // 3
    nh = d // HEAD_DIM
    nt = s // t
    f32 = jnp.float32
    bf = jnp.bfloat16
    return pl.pallas_call(
        functools.partial(_fox_kernel, t=t, tc=tc, rb=rb),
        grid=(bsz, nh),
        in_specs=[
            pl.BlockSpec((1, s, HEAD_DIM), lambda b, h: (b, 0, h)),
            pl.BlockSpec((1, s, LANES), lambda b, h: (b, 0, 0)),
            pl.BlockSpec((1, s, HEAD_DIM), lambda b, h: (b, 0, nh + h)),
            pl.BlockSpec((1, s, LANES), lambda b, h: (b, 0, 0)),
            pl.BlockSpec((1, s, HEAD_DIM), lambda b, h: (b, 0, 2 * nh + h)),
        ],
        out_specs=pl.BlockSpec((1, s, HEAD_DIM), lambda b, h: (b, 0, h)),
        out_shape=jax.ShapeDtypeStruct((bsz, s, d), bf),
        scratch_shapes=[pltpu.VMEM((nt, t, HEAD_DIM + LANES), bf),
                        pltpu.VMEM((nt, t, HEAD_DIM + LANES), bf),
                        pltpu.VMEM((nt, HEAD_DIM, t), bf),
                        pltpu.VMEM((t, t), f32),
                        pltpu.VMEM((t, t), f32),
                        pltpu.VMEM((t, t), bf),
                        pltpu.VMEM((t, t), bf),
                        pltpu.VMEM((HEAD_DIM, t), f32)],
        compiler_params=_params(("arbitrary", "arbitrary")),
        name="fox_attention",
    )(qkv, qx, qkv, kx, qkv)


def _fox_from_fz(qkv, fz, b_f):
    qx, kx = _logf_cumsum(fz, b_f)
    return _fox_attention(qkv, qx, kx)


def _out_proj_kernel(h_ref, o_ref, w_ref, out_ref):
    out_ref[...] = h_ref[...] + jnp.dot(o_ref[...], w_ref[...],
                                        preferred_element_type=jnp.float32)


def _out_proj(h, o, w, *, tm=512):
    n, d = h.shape
    return pl.pallas_call(
        _out_proj_kernel,
        grid=(n // tm,),
        in_specs=[pl.BlockSpec((tm, d), lambda i: (i, 0)),
                  pl.BlockSpec((tm, d), lambda i: (i, 0)),
                  pl.BlockSpec((d, d), lambda i: (0, 0))],
        out_specs=pl.BlockSpec((tm, d), lambda i: (i, 0)),
        out_shape=jax.ShapeDtypeStruct((n, d), jnp.float32),
        compiler_params=_params(("arbitrary",)),
        name="out_proj",
    )(h, o, w)


def _mlp_kernel(h_ref, g_ref, w1_ref, w2_ref, out_ref, u_ref, acc_ref):
    f = pl.program_id(1)

    @pl.when(f == 0)
    def _():
        x = h_ref[...]
        r = lax.rsqrt(jnp.mean(x * x, axis=-1, keepdims=True) + EPS)
        u_ref[...] = (x * r * g_ref[...]).astype(u_ref.dtype)
        acc_ref[...] = jnp.zeros_like(acc_ref)

    a = jnp.dot(u_ref[...], w1_ref[...], preferred_element_type=jnp.float32)
    a = jnp.maximum(a, 0.0)
    a = (a * a).astype(w2_ref.dtype)
    acc_ref[...] += jnp.dot(a, w2_ref[...], preferred_element_type=jnp.float32)

    @pl.when(f == pl.num_programs(1) - 1)
    def _():
        out_ref[...] = h_ref[...] + acc_ref[...]


def _mlp(h, g, w1, w2, *, tm=512, tf=1024):
    n, d = h.shape
    dff = w1.shape[1]
    return pl.pallas_call(
        _mlp_kernel,
        grid=(n // tm, dff // tf),
        in_specs=[pl.BlockSpec((tm, d), lambda i, f: (i, 0)),
                  pl.BlockSpec((1, d), lambda i, f: (0, 0)),
                  pl.BlockSpec((d, tf), lambda i, f: (0, f)),
                  pl.BlockSpec((tf, d), lambda i, f: (f, 0))],
        out_specs=pl.BlockSpec((tm, d), lambda i, f: (i, 0)),
        out_shape=jax.ShapeDtypeStruct((n, d), jnp.float32),
        scratch_shapes=[pltpu.VMEM((tm, d), jnp.bfloat16),
                        pltpu.VMEM((tm, d), jnp.float32)],
        compiler_params=_params(("arbitrary", "arbitrary")),
        name="sq_relu_mlp",
    )(h, g.reshape(1, d), w1, w2)


_KB = _GROUP
_NBLK = _WIN // _KB


def _band_kernel(q_ref, k_ref, v_ref, bias_ref, o_ref, vt_ref, s0_ref, s1_ref, p0_ref, p1_ref):
    ng = vt_ref.shape[0]
    f32 = jnp.float32
    s_refs, p_refs = (s0_ref, s1_ref), (p0_ref, p1_ref)

    for j in range(ng):
        vt_ref[j] = v_ref[0, j * _KB:(j + 1) * _KB, :].T

    def fold8(x, op):
        return op(x.reshape(x.shape[0] // 8, 8, x.shape[1]), axis=0)

    def rows_of(g):
        return slice(g * _GROUP, (g + 1) * _GROUP)

    def key_blocks(g):
        return [(blk, g - (_NBLK - 1) + blk) for blk in range(_NBLK) if g - (_NBLK - 1) + blk >= 0]

    def logits_block(g, blk, kb, s_out):
        s = lax.dot_general(k_ref[0, rows_of(kb), :], q_ref[0, rows_of(g), :], _NT,
                            preferred_element_type=f32) + bias_ref[0, blk]
        s_out[blk] = s
        return fold8(s, jnp.max)

    def write_out(g, pv, l):
        o_ref[0, rows_of(g), :] = (pv / l).T.astype(o_ref.dtype)

    mx = None
    for blk, kb in key_blocks(0):
        pm = logits_block(0, blk, kb, s_refs[0])
        mx = pm if mx is None else jnp.maximum(mx, pm)
    m = jnp.max(mx, axis=0, keepdims=True)

    l_prev = None
    for g in range(ng):
        cur, nxt = g % 2, 1 - g % 2
        nxt_blocks = key_blocks(g + 1) if g + 1 < ng else []
        cur_blocks = key_blocks(g)
        prv_blocks = key_blocks(g - 1) if g > 0 else []
        mx = ls = pv = None
        for i in range(_NBLK):
            if i < len(nxt_blocks):
                blk, kb = nxt_blocks[i]
                pm = logits_block(g + 1, blk, kb, s_refs[nxt])
                mx = pm if mx is None else jnp.maximum(mx, pm)
            if i < len(cur_blocks):
                blk, _ = cur_blocks[i]
                p = jnp.exp2(s_refs[cur][blk] - m)
                ps = fold8(p, jnp.sum)
                ls = ps if ls is None else ls + ps
                p_refs[cur][blk] = p.astype(p_refs[cur].dtype)
            if i < len(prv_blocks):
                blk, kb = prv_blocks[i]
                d = jnp.dot(vt_ref[kb], p_refs[nxt][blk], preferred_element_type=f32)
                pv = d if pv is None else pv + d
        if g > 0:
            write_out(g - 1, pv, l_prev)
        l_prev = jnp.sum(ls, axis=0, keepdims=True)
        if nxt_blocks:
            m = jnp.max(mx, axis=0, keepdims=True)

    pv = None
    for blk, kb in key_blocks(ng - 1):
        d = jnp.dot(vt_ref[kb], p_refs[(ng - 1) % 2][blk], preferred_element_type=f32)
        pv = d if pv is None else pv + d
    write_out(ng - 1, pv, l_prev)


def _band_attention(kvq, bias_t):
    bsz, s, d3 = kvq.shape
    d = d3 // 3
    nh = d // HEAD_DIM
    ng = s // _GROUP
    f32 = jnp.float32
    bf = jnp.bfloat16
    return pl.pallas_call(
        _band_kernel,
        grid=(bsz, nh),
        in_specs=[
            pl.BlockSpec((1, s, HEAD_DIM), lambda b, h: (b, 0, 2 * nh + h)),
            pl.BlockSpec((1, s, HEAD_DIM), lambda b, h: (b, 0, h)),
            pl.BlockSpec((1, s, HEAD_DIM), lambda b, h: (b, 0, nh + h)),
            pl.BlockSpec((1, _NBLK, _KB, _GROUP), lambda b, h: (h, 0, 0, 0)),
        ],
        out_specs=pl.BlockSpec((1, s, HEAD_DIM), lambda b, h: (b, 0, h)),
        out_shape=jax.ShapeDtypeStruct((bsz, s, d), bf),
        scratch_shapes=[pltpu.VMEM((ng, HEAD_DIM, _KB), bf),
                        pltpu.VMEM((_NBLK, _KB, _GROUP), f32),
                        pltpu.VMEM((_NBLK, _KB, _GROUP), f32),
                        pltpu.VMEM((_NBLK, _KB, _GROUP), bf),
                        pltpu.VMEM((_NBLK, _KB, _GROUP), bf)],
        compiler_params=_params(("arbitrary", "arbitrary")),
        name="band_attention",
    )(kvq, kvq, kvq, bias_t)


def _band_bias(rel_table):
    nh = rel_table.shape[0]
    span = _WIN + _GROUP
    k = jnp.arange(span)
    idx = jnp.clip(k - (_WIN - 1) + _PAD, -(CHUNK - 1), REL_CLIP) + (CHUNK - 1)
    g = rel_table[:, idx].astype(jnp.float32) * LOG2E
    bias = jnp.tile(g, (1, _WIN))[:, :_WIN * (span - 1)].reshape(nh, _WIN, span - 1)
    bias = bias[:, :, _WIN - 1:]
    j = jnp.arange(_WIN)[:, None]
    i = jnp.arange(_GROUP)[None, :]
    km = j - (i // CHUNK) * CHUNK
    bias = jnp.where((km >= 0) & (km < _BAND), bias, -jnp.inf)
    return bias.reshape(nh, _NBLK, _KB, _GROUP)


def kernel(x, a_norm_g, a_w_in, a_b_f, a_q_g, a_k_g, a_w_out, mlp_norm_g, mlp_w1, mlp_w2,
           kv_norm_g, kv_w, kv_k_g, b_norm_g, b_w_q, b_q_g, b_rel, b_w_out):
    bsz, s, d = x.shape
    n = bsz * s
    nh = N_HEADS
    bf = jnp.bfloat16
    q_scale = HEAD_DIM ** -0.5 * LOG2E
    tn = 2048
    per_part = d // tn

    h = x.reshape(n, d)

    w_in = a_w_in[0]
    w_qkv = w_in[:, :3 * d].astype(bf)
    w_f = jnp.pad(w_in[:, 3 * d:], ((0, 0), (0, LANES - nh))).astype(bf)
    qkv, fz = _proj(h, [a_norm_g[0]], w_qkv,
                    [a_q_g[0] * q_scale] * per_part + [a_k_g[0]] * per_part + [None] * per_part,
                    w_f, tn=tn)
    b_f = jnp.pad(a_b_f[0], (0, LANES - nh)).reshape(1, LANES)
    o = _fox_from_fz(qkv.reshape(bsz, s, 3 * d), fz.reshape(bsz, s, LANES), b_f)
    h = _out_proj(h, o.reshape(n, d), a_w_out[0].astype(bf))
    h = _mlp(h, mlp_norm_g[0], mlp_w1[0].astype(bf), mlp_w2[0].astype(bf))

    w_kvq = jnp.concatenate([kv_w.astype(bf), b_w_q[0].astype(bf)], axis=1)
    kvq = _proj(h, [kv_norm_g, b_norm_g[0]], w_kvq,
                [kv_k_g] * per_part + [None] * per_part + [b_q_g[0] * q_scale] * per_part,
                split_col=2 * d, tn=tn)

    o = _band_attention(kvq.reshape(bsz, s, 3 * d), _band_bias(b_rel[0]))
    h = _out_proj(h, o.reshape(n, d), b_w_out[0].astype(bf))
    h = _mlp(h, mlp_norm_g[1], mlp_w1[1].astype(bf), mlp_w2[1].astype(bf))
    return h.reshape(bsz, s, d)
```

```python
import functools
import math

import numpy as np

import jax
import jax.numpy as jnp
from jax import lax
from jax.experimental import pallas as pl
from jax.experimental.pallas import tpu as pltpu

N_HEADS = 16
HEAD_DIM = 128
CHUNK = 64
N_PREV_CHUNKS = 8
REL_CLIP = 256
EPS = 1e-6
LOG2E = math.log2(math.e)

LANES = 128
VMEM_LIMIT = 60 * 1024 * 1024

_BAND = (N_PREV_CHUNKS + 1) * CHUNK
_PAD = N_PREV_CHUNKS * CHUNK
_GROUP = 4 * CHUNK
_WIN = _PAD + _GROUP


def _params(sem):
    return pltpu.CompilerParams(dimension_semantics=sem, vmem_limit_bytes=VMEM_LIMIT)


def _proj_kernel(*refs, parts, has_f, tn):
    if has_f:
        x_ref, g_ref, w_ref, gain_ref, wf_ref, o_ref, f_ref, u_ref = refs
    else:
        x_ref, g_ref, w_ref, gain_ref, o_ref, u_ref = refs
    n_norms = u_ref.shape[0]
    x = x_ref[...]
    xn = x * lax.rsqrt(jnp.mean(x * x, axis=-1, keepdims=True) + EPS)
    for k in range(n_norms):
        u_ref[k] = (xn * g_ref[k:k + 1, :]).astype(u_ref.dtype)
    if has_f:
        f_ref[...] = jnp.dot(u_ref[0], wf_ref[...], preferred_element_type=jnp.float32)
    for j, (norm_idx, gain_idx) in enumerate(parts):
        y = jnp.dot(u_ref[norm_idx], w_ref[:, j * tn:(j + 1) * tn],
                    preferred_element_type=jnp.float32)
        if gain_idx is None:
            o_ref[:, j * tn:(j + 1) * tn] = y.astype(o_ref.dtype)
            continue
        gain = gain_ref[gain_idx:gain_idx + 1, :]
        for c in range(tn // HEAD_DIM):
            ys = y[:, c * HEAD_DIM:(c + 1) * HEAD_DIM]
            r = lax.rsqrt(jnp.mean(ys * ys, axis=-1, keepdims=True) + EPS)
            sl = slice(j * tn + c * HEAD_DIM, j * tn + (c + 1) * HEAD_DIM)
            o_ref[:, sl] = (ys * r * gain).astype(o_ref.dtype)


def _proj(x, norm_gains, w, head_gains, parts, wf=None, *, tm=512, tn=2048):
    n, d = x.shape
    nout = w.shape[1]
    assert len(parts) * tn == nout
    has_f = wf is not None
    n_norms = len(norm_gains)
    once = pl.Buffered(1)
    in_specs = [
        pl.BlockSpec((tm, d), lambda i: (i, 0)),
        pl.BlockSpec((n_norms, d), lambda i: (0, 0)),
        pl.BlockSpec((d, nout), lambda i: (0, 0), pipeline_mode=once),
        pl.BlockSpec((len(head_gains), HEAD_DIM), lambda i: (0, 0)),
    ]
    args = [x, jnp.stack(norm_gains), w, jnp.stack(head_gains)]
    out_shape = [jax.ShapeDtypeStruct((n, nout), jnp.bfloat16)]
    out_specs = [pl.BlockSpec((tm, nout), lambda i: (i, 0))]
    if has_f:
        in_specs.append(pl.BlockSpec((d, LANES), lambda i: (0, 0), pipeline_mode=once))
        args.append(wf)
        out_shape.append(jax.ShapeDtypeStruct((n, LANES), jnp.float32))
        out_specs.append(pl.BlockSpec((tm, LANES), lambda i: (i, 0)))
    res = pl.pallas_call(
        functools.partial(_proj_kernel, parts=tuple(parts), has_f=has_f, tn=tn),
        grid=(n // tm,),
        in_specs=in_specs,
        out_specs=out_specs,
        out_shape=out_shape,
        scratch_shapes=[pltpu.VMEM((n_norms, tm, d), jnp.bfloat16)],
        compiler_params=_params(("arbitrary",)),
        name="norm_proj",
    )(*args)
    return res if has_f else res[0]


_HEAD_LANES = LANES // N_HEADS
_N_TERMS = 3


def _bias_column_constants():
    sel = np.zeros((_N_TERMS * LANES, 2 * LANES), np.float32)
    one = np.zeros((1, 2 * LANES), np.float32)
    for h in range(N_HEADS):
        for term in range(_N_TERMS):
            sel[term * LANES + h, h * _HEAD_LANES + term] = -1.0
            sel[term * LANES + h, LANES + h * _HEAD_LANES + _N_TERMS + term] = 1.0
            one[0, h * _HEAD_LANES + _N_TERMS + term] = 1.0
            one[0, LANES + h * _HEAD_LANES + term] = 1.0
    return jnp.asarray(sel, jnp.bfloat16), jnp.asarray(one, jnp.float32)


def _logf_cumsum_kernel(f_ref, b_ref, sel_ref, one_ref, qx_ref, kx_ref):
    z = f_ref[0] + b_ref[...]
    c = jnp.minimum(z, 0.0) - jnp.log1p(jnp.exp(-jnp.abs(z)))
    s = c.shape[0]
    row = lax.broadcasted_iota(jnp.int32, c.shape, 0)
    d = 1
    while d < s:
        c = c + jnp.where(row >= d, pltpu.roll(c, d, axis=0), 0.0)
        d *= 2
    c = c * LOG2E
    hi = c.astype(jnp.bfloat16)
    r = c - hi.astype(jnp.float32)
    mid = r.astype(jnp.bfloat16)
    lo = (r - mid.astype(jnp.float32)).astype(jnp.bfloat16)
    terms = jnp.concatenate([hi, mid, lo], axis=1)
    y = jnp.dot(terms, sel_ref[...], preferred_element_type=jnp.float32) + one_ref[...]
    kx_ref[0] = y[:, :LANES].astype(kx_ref.dtype)
    qx_ref[0] = y[:, LANES:].astype(qx_ref.dtype)


def _logf_cumsum(fz, b):
    bsz, s, _ = fz.shape
    sel, one = _bias_column_constants()
    spec = pl.BlockSpec((1, s, LANES), lambda i: (i, 0, 0))
    out = jax.ShapeDtypeStruct((bsz, s, LANES), jnp.bfloat16)
    return pl.pallas_call(
        _logf_cumsum_kernel,
        grid=(bsz,),
        in_specs=[spec, pl.BlockSpec((1, LANES), lambda i: (0, 0)),
                  pl.BlockSpec(sel.shape, lambda i: (0, 0)),
                  pl.BlockSpec(one.shape, lambda i: (0, 0))],
        out_specs=[spec, spec],
        out_shape=[out, out],
        compiler_params=_params(("arbitrary",)),
        name="logf_cumsum",
    )(fz, b, sel, one)


_NT = (((1,), (1,)), ((), ()))


def _fox_kernel(q_ref, qx_ref, k_ref, kx_ref, v_ref, o_ref,
                qa_ref, ka_ref, vt_ref, s0_ref, s1_ref, p0_ref, p1_ref, acc_ref, *, t, tc, rb):
    nt = ka_ref.shape[0]
    nc = t // tc
    nr = t // rb
    f32 = jnp.float32
    s_refs, p_refs = (s0_ref, s1_ref), (p0_ref, p1_ref)
    tiles = [(qi, kv) for qi in range(nt) for kv in range(qi + 1)]

    lane = lax.broadcasted_iota(jnp.int32, (t, LANES), 1)
    own_lanes = lane // _HEAD_LANES == pl.program_id(1)

    for j in range(nt):
        rows = slice(j * t, (j + 1) * t)
        ka_ref[j, :, 0:HEAD_DIM] = k_ref[0, rows, :]
        ka_ref[j, :, HEAD_DIM:] = kx_ref[0, rows, :]
        qa_ref[j, :, 0:HEAD_DIM] = q_ref[0, rows, :]
        qx = qx_ref[0, rows, :]
        qa_ref[j, :, HEAD_DIM:] = jnp.where(own_lanes, qx, jnp.zeros_like(qx))
        vt_ref[:, rows] = v_ref[0, rows, :].T

    def fold8(x, op):
        return op(x.reshape(x.shape[0] // 8, 8, x.shape[1]), axis=0)

    def live(tile, r, c):
        return tile[1] < tile[0] or r * rb < (c + 1) * tc

    def key_rows(tile, c):
        return t if tile[1] < tile[0] else min(t, -(-(c + 1) * tc // rb) * rb)

    def logits_block(tile, r, c, s_out):
        qi, kv = tile
        s = lax.dot_general(ka_ref[kv, r * rb:(r + 1) * rb, :],
                            qa_ref[qi, c * tc:(c + 1) * tc, :], _NT,
                            preferred_element_type=f32)
        if kv == qi and (r + 1) * rb > c * tc:
            key = lax.broadcasted_iota(jnp.int32, s.shape, 0) + r * rb
            qry = lax.broadcasted_iota(jnp.int32, s.shape, 1) + c * tc
            s = jnp.where(key <= qry, s, -jnp.inf)
        s_out[r * rb:(r + 1) * rb, c * tc:(c + 1) * tc] = s
        return fold8(s, jnp.max)

    blocks = [(r, c) for r in range(nr) for c in range(nc)]

    def tile_max(mx):
        return [jnp.max(mx[c], axis=0, keepdims=True) for c in range(nc)]

    mx = [None] * nc
    for r, c in blocks:
        if not live(tiles[0], r, c):
            continue
        pm = logits_block(tiles[0], r, c, s_refs[0])
        mx[c] = pm if mx[c] is None else jnp.maximum(mx[c], pm)
    mc = tile_max(mx)

    m_run = l_run = alpha_prev = None
    for n, (qi, kv) in enumerate(tiles):
        cur, nxt = n % 2, 1 - n % 2
        nxt_tile = tiles[n + 1] if n + 1 < len(tiles) else None
        prv_tile = tiles[n - 1] if n > 0 else None
        if kv == 0:
            m_new = mc
            alpha = None
        else:
            m_new = [jnp.maximum(m_run[c], mc[c]) for c in range(nc)]
            alpha = [jnp.exp2(m_run[c] - m_new[c]) for c in range(nc)]
        mx = [None] * nc
        ls = [None] * nc
        pv = [None] * nc
        for r, c in blocks:
            rows = slice(r * rb, (r + 1) * rb)
            cols = slice(c * tc, (c + 1) * tc)
            if nxt_tile is not None and live(nxt_tile, r, c):
                pm = logits_block(nxt_tile, r, c, s_refs[nxt])
                mx[c] = pm if mx[c] is None else jnp.maximum(mx[c], pm)
            if live((qi, kv), r, c):
                p = jnp.exp2(s_refs[cur][rows, cols] - m_new[c])
                ps = fold8(p, jnp.sum)
                ls[c] = ps if ls[c] is None else ls[c] + ps
                p_refs[cur][rows, cols] = p.astype(p_refs[cur].dtype)
            if prv_tile is not None and r == nr - 1:
                nk = key_rows(prv_tile, c)
                k0 = prv_tile[1] * t
                pv[c] = jnp.dot(vt_ref[:, k0:k0 + nk], p_refs[nxt][0:nk, cols],
                                preferred_element_type=f32)
        lsum = [jnp.sum(ls[c], axis=0, keepdims=True) for c in range(nc)]
        l_prev_block = l_run
        l_run = lsum if kv == 0 else [alpha[c] * l_run[c] + lsum[c] for c in range(nc)]
        if prv_tile is not None:
            for c in range(nc):
                cols = slice(c * tc, (c + 1) * tc)
                if prv_tile[1] == 0:
                    acc_ref[:, cols] = pv[c]
                else:
                    acc_ref[:, cols] = alpha_prev[c] * acc_ref[:, cols] + pv[c]
            if prv_tile[1] == prv_tile[0]:
                l_fin = jnp.concatenate(l_prev_block, axis=1)
                o_ref[0, prv_tile[0] * t:(prv_tile[0] + 1) * t, :] = (
                    acc_ref[...] / l_fin).T.astype(o_ref.dtype)
        m_run, alpha_prev = m_new, alpha
        if nxt_tile is not None:
            mc = tile_max(mx)

    last = len(tiles) - 1
    qi, kv = tiles[last]
    for c in range(nc):
        cols = slice(c * tc, (c + 1) * tc)
        nk = key_rows(tiles[last], c)
        d = jnp.dot(vt_ref[:, kv * t:kv * t + nk], p_refs[last % 2][0:nk, cols],
                    preferred_element_type=f32)
        acc_ref[:, cols] = d if kv == 0 else alpha_prev[c] * acc_ref[:, cols] + d
    o_ref[0, qi * t:(qi + 1) * t, :] = (
        acc_ref[...] / jnp.concatenate(l_run, axis=1)).T.astype(o_ref.dtype)


def _fox_attention(qkv, qx, kx, *, t=512, tc=256, rb=256):
    bsz, s, d3 = qkv.shape
    d = d3 // 3
    nh = d // HEAD_DIM
    nt = s // t
    f32 = jnp.float32
    bf = jnp.bfloat16
    return pl.pallas_call(
        functools.partial(_fox_kernel, t=t, tc=tc, rb=rb),
        grid=(bsz, nh),
        in_specs=[
            pl.BlockSpec((1, s, HEAD_DIM), lambda b, h: (b, 0, h)),
            pl.BlockSpec((1, s, LANES), lambda b, h: (b, 0, 0)),
            pl.BlockSpec((1, s, HEAD_DIM), lambda b, h: (b, 0, nh + h)),
            pl.BlockSpec((1, s, LANES), lambda b, h: (b, 0, 0)),
            pl.BlockSpec((1, s, HEAD_DIM), lambda b, h: (b, 0, 2 * nh + h)),
        ],
        out_specs=pl.BlockSpec((1, s, HEAD_DIM), lambda b, h: (b, 0, h)),
        out_shape=jax.ShapeDtypeStruct((bsz, s, d), bf),
        scratch_shapes=[pltpu.VMEM((nt, t, HEAD_DIM + LANES), bf),
                        pltpu.VMEM((nt, t, HEAD_DIM + LANES), bf),
                        pltpu.VMEM((HEAD_DIM, s), bf),
                        pltpu.VMEM((t, t), f32),
                        pltpu.VMEM((t, t), f32),
                        pltpu.VMEM((t, t), bf),
                        pltpu.VMEM((t, t), bf),
                        pltpu.VMEM((HEAD_DIM, t), f32)],
        compiler_params=_params(("arbitrary", "arbitrary")),
        name="fox_attention",
    )(qkv, qx, qkv, kx, qkv)


def _fox_from_fz(qkv, fz, b_f):
    qx, kx = _logf_cumsum(fz, b_f)
    return _fox_attention(qkv, qx, kx)


def _out_proj_kernel(h_ref, o_ref, w_ref, out_ref):
    out_ref[...] = h_ref[...] + jnp.dot(o_ref[...], w_ref[...],
                                        preferred_element_type=jnp.float32)


def _out_proj(h, o, w, *, tm=512):
    n, d = h.shape
    return pl.pallas_call(
        _out_proj_kernel,
        grid=(n // tm,),
        in_specs=[pl.BlockSpec((tm, d), lambda i: (i, 0)),
                  pl.BlockSpec((tm, d), lambda i: (i, 0)),
                  pl.BlockSpec((d, d), lambda i: (0, 0))],
        out_specs=pl.BlockSpec((tm, d), lambda i: (i, 0)),
        out_shape=jax.ShapeDtypeStruct((n, d), jnp.float32),
        compiler_params=_params(("arbitrary",)),
        name="out_proj",
    )(h, o, w)


def _mlp_kernel(h_ref, g_ref, w1_ref, w2_ref, out_ref, u_ref, acc_ref):
    f = pl.program_id(1)

    @pl.when(f == 0)
    def _():
        x = h_ref[...]
        r = lax.rsqrt(jnp.mean(x * x, axis=-1, keepdims=True) + EPS)
        u_ref[...] = (x * r * g_ref[...]).astype(u_ref.dtype)
        acc_ref[...] = jnp.zeros_like(acc_ref)

    a = jnp.dot(u_ref[...], w1_ref[...], preferred_element_type=jnp.float32)
    a = jnp.maximum(a, 0.0)
    a = (a * a).astype(w2_ref.dtype)
    acc_ref[...] += jnp.dot(a, w2_ref[...], preferred_element_type=jnp.float32)

    @pl.when(f == pl.num_programs(1) - 1)
    def _():
        out_ref[...] = h_ref[...] + acc_ref[...]


def _mlp(h, g, w1, w2, *, tm=512, tf=1024):
    n, d = h.shape
    dff = w1.shape[1]
    return pl.pallas_call(
        _mlp_kernel,
        grid=(n // tm, dff // tf),
        in_specs=[pl.BlockSpec((tm, d), lambda i, f: (i, 0)),
                  pl.BlockSpec((1, d), lambda i, f: (0, 0)),
                  pl.BlockSpec((d, tf), lambda i, f: (0, f)),
                  pl.BlockSpec((tf, d), lambda i, f: (f, 0))],
        out_specs=pl.BlockSpec((tm, d), lambda i, f: (i, 0)),
        out_shape=jax.ShapeDtypeStruct((n, d), jnp.float32),
        scratch_shapes=[pltpu.VMEM((tm, d), jnp.bfloat16),
                        pltpu.VMEM((tm, d), jnp.float32)],
        compiler_params=_params(("arbitrary", "arbitrary")),
        name="sq_relu_mlp",
    )(h, g.reshape(1, d), w1, w2)


_KB = _GROUP
_NBLK = _WIN // _KB


def _band_kernel(q_ref, k_ref, v_ref, bias_ref, o_ref, vt_ref, s0_ref, s1_ref, p0_ref, p1_ref):
    ng = vt_ref.shape[1] // _GROUP
    f32 = jnp.float32
    s_refs, p_refs = (s0_ref, s1_ref), (p0_ref, p1_ref)

    for j in range(ng):
        vt_ref[:, j * _KB:(j + 1) * _KB] = v_ref[0, j * _KB:(j + 1) * _KB, :].T

    def fold8(x, op):
        return op(x.reshape(x.shape[0] // 8, 8, x.shape[1]), axis=0)

    def rows_of(g):
        return slice(g * _GROUP, (g + 1) * _GROUP)

    def key_blocks(g):
        return [(blk, g - (_NBLK - 1) + blk) for blk in range(_NBLK) if g - (_NBLK - 1) + blk >= 0]

    def logits_block(g, blk, kb, s_out):
        s = lax.dot_general(k_ref[0, rows_of(kb), :], q_ref[0, rows_of(g), :], _NT,
                            preferred_element_type=f32) + bias_ref[0, blk]
        s_out[blk * _KB:(blk + 1) * _KB, :] = s
        return fold8(s, jnp.max)

    def values(g, p_ref):
        blks = key_blocks(g)
        (b0, k0), nb = blks[0], len(blks)
        return jnp.dot(vt_ref[:, k0 * _KB:(k0 + nb) * _KB], p_ref[b0 * _KB:(b0 + nb) * _KB, :],
                       preferred_element_type=f32)

    def write_out(g, pv, l):
        o_ref[0, rows_of(g), :] = (pv / l).T.astype(o_ref.dtype)

    mx = None
    for blk, kb in key_blocks(0):
        pm = logits_block(0, blk, kb, s_refs[0])
        mx = pm if mx is None else jnp.maximum(mx, pm)
    m = jnp.max(mx, axis=0, keepdims=True)

    l_prev = None
    for g in range(ng):
        cur, nxt = g % 2, 1 - g % 2
        nxt_blocks = key_blocks(g + 1) if g + 1 < ng else []
        cur_blocks = key_blocks(g)
        mx = ls = None
        for i in range(_NBLK):
            if i < len(nxt_blocks):
                blk, kb = nxt_blocks[i]
                pm = logits_block(g + 1, blk, kb, s_refs[nxt])
                mx = pm if mx is None else jnp.maximum(mx, pm)
            if i < len(cur_blocks):
                blk, _ = cur_blocks[i]
                p = jnp.exp2(s_refs[cur][blk * _KB:(blk + 1) * _KB, :] - m)
                ps = fold8(p, jnp.sum)
                ls = ps if ls is None else ls + ps
                p_refs[cur][blk * _KB:(blk + 1) * _KB, :] = p.astype(p_refs[cur].dtype)
        if g > 0:
            write_out(g - 1, values(g - 1, p_refs[nxt]), l_prev)
        l_prev = jnp.sum(ls, axis=0, keepdims=True)
        if nxt_blocks:
            m = jnp.max(mx, axis=0, keepdims=True)

    write_out(ng - 1, values(ng - 1, p_refs[(ng - 1) % 2]), l_prev)


def _band_attention(kvq, bias_t):
    bsz, s, d3 = kvq.shape
    d = d3 // 3
    nh = d // HEAD_DIM
    ng = s // _GROUP
    f32 = jnp.float32
    bf = jnp.bfloat16
    return pl.pallas_call(
        _band_kernel,
        grid=(bsz, nh),
        in_specs=[
            pl.BlockSpec((1, s, HEAD_DIM), lambda b, h: (b, 0, 2 * nh + h)),
            pl.BlockSpec((1, s, HEAD_DIM), lambda b, h: (b, 0, h)),
            pl.BlockSpec((1, s, HEAD_DIM), lambda b, h: (b, 0, nh + h)),
            pl.BlockSpec((1, _NBLK, _KB, _GROUP), lambda b, h: (h, 0, 0, 0)),
        ],
        out_specs=pl.BlockSpec((1, s, HEAD_DIM), lambda b, h: (b, 0, h)),
        out_shape=jax.ShapeDtypeStruct((bsz, s, d), bf),
        scratch_shapes=[pltpu.VMEM((HEAD_DIM, s), bf),
                        pltpu.VMEM((_WIN, _GROUP), f32),
                        pltpu.VMEM((_WIN, _GROUP), f32),
                        pltpu.VMEM((_WIN, _GROUP), bf),
                        pltpu.VMEM((_WIN, _GROUP), bf)],
        compiler_params=_params(("arbitrary", "arbitrary")),
        name="band_attention",
    )(kvq, kvq, kvq, bias_t)


def _band_bias(rel_table):
    nh = rel_table.shape[0]
    span = _WIN + _GROUP
    k = jnp.arange(span)
    idx = jnp.clip(k - (_WIN - 1) + _PAD, -(CHUNK - 1), REL_CLIP) + (CHUNK - 1)
    g = rel_table[:, idx].astype(jnp.float32) * LOG2E
    bias = jnp.tile(g, (1, _WIN))[:, :_WIN * (span - 1)].reshape(nh, _WIN, span - 1)
    bias = bias[:, :, _WIN - 1:]
    j = jnp.arange(_WIN)[:, None]
    i = jnp.arange(_GROUP)[None, :]
    km = j - (i // CHUNK) * CHUNK
    bias = jnp.where((km >= 0) & (km < _BAND), bias, -jnp.inf)
    return bias.reshape(nh, _NBLK, _KB, _GROUP)


def kernel(x, a_norm_g, a_w_in, a_b_f, a_q_g, a_k_g, a_w_out, mlp_norm_g, mlp_w1, mlp_w2,
           kv_norm_g, kv_w, kv_k_g, b_norm_g, b_w_q, b_q_g, b_rel, b_w_out):
    bsz, s, d = x.shape
    n = bsz * s
    nh = N_HEADS
    bf = jnp.bfloat16
    q_scale = HEAD_DIM ** -0.5 * LOG2E

    h = x.reshape(n, d)

    w_in = a_w_in[0]
    w_qkv = w_in[:, :3 * d].astype(bf)
    w_f = jnp.pad(w_in[:, 3 * d:], ((0, 0), (0, LANES - nh))).astype(bf)
    qkv, fz = _proj(h, [a_norm_g[0]], w_qkv, [a_q_g[0] * q_scale, a_k_g[0]],
                    [(0, 0), (0, 1), (0, None)], w_f)
    b_f = jnp.pad(a_b_f[0], (0, LANES - nh)).reshape(1, LANES)
    o = _fox_from_fz(qkv.reshape(bsz, s, 3 * d), fz.reshape(bsz, s, LANES), b_f)
    h = _out_proj(h, o.reshape(n, d), a_w_out[0].astype(bf))
    h = _mlp(h, mlp_norm_g[0], mlp_w1[0].astype(bf), mlp_w2[0].astype(bf))

    w_kvq = jnp.concatenate([kv_w.astype(bf), b_w_q[0].astype(bf)], axis=1)
    kvq = _proj(h, [kv_norm_g, b_norm_g[0]], w_kvq, [kv_k_g, b_q_g[0] * q_scale],
                [(0, 0), (0, None), (1, 1)])

    o = _band_attention(kvq.reshape(bsz, s, 3 * d), _band_bias(b_rel[0]))
    h = _out_proj(h, o.reshape(n, d), b_w_out[0].astype(bf))
    h = _mlp(h, mlp_norm_g[1], mlp_w1[1].astype(bf), mlp_w2[1].astype(bf))
    return h.reshape(bsz, s, d)
```

```python
import functools
import math

import numpy as np

import jax
import jax.numpy as jnp
from jax import lax
from jax.experimental import pallas as pl
from jax.experimental.pallas import tpu as pltpu

N_HEADS = 16
HEAD_DIM = 128
CHUNK = 64
N_PREV_CHUNKS = 8
REL_CLIP = 256
EPS = 1e-6
LOG2E = math.log2(math.e)

LANES = 128
VMEM_LIMIT = 60 * 1024 * 1024

_BAND = (N_PREV_CHUNKS + 1) * CHUNK
_PAD = N_PREV_CHUNKS * CHUNK
_GROUP = 4 * CHUNK
_WIN = _PAD + _GROUP


def _params(sem):
    return pltpu.CompilerParams(dimension_semantics=sem, vmem_limit_bytes=VMEM_LIMIT)


def _proj_kernel(*refs, parts, has_f, tn):
    if has_f:
        x_ref, g_ref, w_ref, gain_ref, wf_ref, o_ref, f_ref, u_ref = refs
    else:
        x_ref, g_ref, w_ref, gain_ref, o_ref, u_ref = refs
    n_norms = u_ref.shape[0]
    x = x_ref[...]
    xn = x * lax.rsqrt(jnp.mean(x * x, axis=-1, keepdims=True) + EPS)
    for k in range(n_norms):
        u_ref[k] = (xn * g_ref[k:k + 1, :]).astype(u_ref.dtype)
    if has_f:
        f_ref[...] = jnp.dot(u_ref[0], wf_ref[...], preferred_element_type=jnp.float32)
    for j, (norm_idx, gain_idx) in enumerate(parts):
        y = jnp.dot(u_ref[norm_idx], w_ref[:, j * tn:(j + 1) * tn],
                    preferred_element_type=jnp.float32)
        if gain_idx is None:
            o_ref[:, j * tn:(j + 1) * tn] = y.astype(o_ref.dtype)
            continue
        gain = gain_ref[gain_idx:gain_idx + 1, :]
        for c in range(tn // HEAD_DIM):
            ys = y[:, c * HEAD_DIM:(c + 1) * HEAD_DIM]
            r = lax.rsqrt(jnp.mean(ys * ys, axis=-1, keepdims=True) + EPS)
            sl = slice(j * tn + c * HEAD_DIM, j * tn + (c + 1) * HEAD_DIM)
            o_ref[:, sl] = (ys * r * gain).astype(o_ref.dtype)


def _proj(x, norm_gains, w, head_gains, parts, wf=None, *, tm=512, tn=2048):
    n, d = x.shape
    nout = w.shape[1]
    assert len(parts) * tn == nout
    has_f = wf is not None
    n_norms = len(norm_gains)
    once = pl.Buffered(1)
    in_specs = [
        pl.BlockSpec((tm, d), lambda i: (i, 0)),
        pl.BlockSpec((n_norms, d), lambda i: (0, 0)),
        pl.BlockSpec((d, nout), lambda i: (0, 0), pipeline_mode=once),
        pl.BlockSpec((len(head_gains), HEAD_DIM), lambda i: (0, 0)),
    ]
    args = [x, jnp.stack(norm_gains), w, jnp.stack(head_gains)]
    out_shape = [jax.ShapeDtypeStruct((n, nout), jnp.bfloat16)]
    out_specs = [pl.BlockSpec((tm, nout), lambda i: (i, 0))]
    if has_f:
        in_specs.append(pl.BlockSpec((d, LANES), lambda i: (0, 0), pipeline_mode=once))
        args.append(wf)
        out_shape.append(jax.ShapeDtypeStruct((n, LANES), jnp.float32))
        out_specs.append(pl.BlockSpec((tm, LANES), lambda i: (i, 0)))
    res = pl.pallas_call(
        functools.partial(_proj_kernel, parts=tuple(parts), has_f=has_f, tn=tn),
        grid=(n // tm,),
        in_specs=in_specs,
        out_specs=out_specs,
        out_shape=out_shape,
        scratch_shapes=[pltpu.VMEM((n_norms, tm, d), jnp.bfloat16)],
        compiler_params=_params(("arbitrary",)),
        name="norm_proj",
    )(*args)
    return res if has_f else res[0]


_HEAD_LANES = LANES // N_HEADS
_N_TERMS = 3


def _bias_column_constants():
    sel = np.zeros((_N_TERMS * LANES, 2 * LANES), np.float32)
    one = np.zeros((1, 2 * LANES), np.float32)
    for h in range(N_HEADS):
        for term in range(_N_TERMS):
            sel[term * LANES + h, h * _HEAD_LANES + term] = -1.0
            sel[term * LANES + h, LANES + h * _HEAD_LANES + _N_TERMS + term] = 1.0
            one[0, h * _HEAD_LANES + _N_TERMS + term] = 1.0
            one[0, LANES + h * _HEAD_LANES + term] = 1.0
    return jnp.asarray(sel, jnp.bfloat16), jnp.asarray(one, jnp.float32)


def _logf_cumsum_kernel(f_ref, b_ref, sel_ref, one_ref, qx_ref, kx_ref):
    z = f_ref[0] + b_ref[...]
    c = jnp.minimum(z, 0.0) - jnp.log1p(jnp.exp(-jnp.abs(z)))
    s = c.shape[0]
    row = lax.broadcasted_iota(jnp.int32, c.shape, 0)
    d = 1
    while d < s:
        c = c + jnp.where(row >= d, pltpu.roll(c, d, axis=0), 0.0)
        d *= 2
    c = c * LOG2E
    hi = c.astype(jnp.bfloat16)
    r = c - hi.astype(jnp.float32)
    mid = r.astype(jnp.bfloat16)
    lo = (r - mid.astype(jnp.float32)).astype(jnp.bfloat16)
    terms = jnp.concatenate([hi, mid, lo], axis=1)
    y = jnp.dot(terms, sel_ref[...], preferred_element_type=jnp.float32) + one_ref[...]
    kx_ref[0] = y[:, :LANES].astype(kx_ref.dtype)
    qx_ref[0] = y[:, LANES:].astype(qx_ref.dtype)


def _logf_cumsum(fz, b):
    bsz, s, _ = fz.shape
    sel, one = _bias_column_constants()
    spec = pl.BlockSpec((1, s, LANES), lambda i: (i, 0, 0))
    out = jax.ShapeDtypeStruct((bsz, s, LANES), jnp.bfloat16)
    return pl.pallas_call(
        _logf_cumsum_kernel,
        grid=(bsz,),
        in_specs=[spec, pl.BlockSpec((1, LANES), lambda i: (0, 0)),
                  pl.BlockSpec(sel.shape, lambda i: (0, 0)),
                  pl.BlockSpec(one.shape, lambda i: (0, 0))],
        out_specs=[spec, spec],
        out_shape=[out, out],
        compiler_params=_params(("arbitrary",)),
        name="logf_cumsum",
    )(fz, b, sel, one)


_NT = (((1,), (1,)), ((), ()))


def _fox_kernel(q_ref, qx_ref, k_ref, kx_ref, v_ref, o_ref,
                qa_ref, ka_ref, vt_ref, s0_ref, s1_ref, p0_ref, p1_ref, acc_ref, *, t, tc, rb):
    nt = ka_ref.shape[0]
    nc = t // tc
    nr = t // rb
    f32 = jnp.float32
    s_refs, p_refs = (s0_ref, s1_ref), (p0_ref, p1_ref)
    tiles = [(qi, kv) for qi in range(nt) for kv in range(qi + 1)]

    lane = lax.broadcasted_iota(jnp.int32, (t, LANES), 1)
    own_lanes = lane // _HEAD_LANES == pl.program_id(1)

    for j in range(nt):
        rows = slice(j * t, (j + 1) * t)
        ka_ref[j, :, 0:HEAD_DIM] = k_ref[0, rows, :]
        ka_ref[j, :, HEAD_DIM:] = kx_ref[0, rows, :]
        qa_ref[j, :, 0:HEAD_DIM] = q_ref[0, rows, :]
        qx = qx_ref[0, rows, :]
        qa_ref[j, :, HEAD_DIM:] = jnp.where(own_lanes, qx, jnp.zeros_like(qx))
        vt_ref[:, rows] = v_ref[0, rows, :].T

    def fold8(x, op):
        return op(x.reshape(x.shape[0] // 8, 8, x.shape[1]), axis=0)

    def live(tile, r, c):
        return tile[1] < tile[0] or r * rb < (c + 1) * tc

    def key_rows(tile, c):
        return t if tile[1] < tile[0] else min(t, -(-(c + 1) * tc // rb) * rb)

    def logits_block(tile, r, c, s_out):
        qi, kv = tile
        s = lax.dot_general(ka_ref[kv, r * rb:(r + 1) * rb, :],
                            qa_ref[qi, c * tc:(c + 1) * tc, :], _NT,
                            preferred_element_type=f32)
        if kv == qi and (r + 1) * rb > c * tc:
            key = lax.broadcasted_iota(jnp.int32, s.shape, 0) + r * rb
            qry = lax.broadcasted_iota(jnp.int32, s.shape, 1) + c * tc
            s = jnp.where(key <= qry, s, -jnp.inf)
        s_out[r * rb:(r + 1) * rb, c * tc:(c + 1) * tc] = s
        return fold8(s, jnp.max)

    blocks = [(r, c) for r in range(nr) for c in range(nc)]

    def tile_max(mx):
        return [jnp.max(mx[c], axis=0, keepdims=True) for c in range(nc)]

    mx = [None] * nc
    for r, c in blocks:
        if not live(tiles[0], r, c):
            continue
        pm = logits_block(tiles[0], r, c, s_refs[0])
        mx[c] = pm if mx[c] is None else jnp.maximum(mx[c], pm)
    mc = tile_max(mx)

    m_run = l_run = alpha_prev = None
    for n, (qi, kv) in enumerate(tiles):
        cur, nxt = n % 2, 1 - n % 2
        nxt_tile = tiles[n + 1] if n + 1 < len(tiles) else None
        prv_tile = tiles[n - 1] if n > 0 else None
        if kv == 0:
            m_new = mc
            alpha = None
        else:
            m_new = [jnp.maximum(m_run[c], mc[c]) for c in range(nc)]
            alpha = [jnp.exp2(m_run[c] - m_new[c]) for c in range(nc)]
        mx = [None] * nc
        ls = [None] * nc
        pv = [None] * nc
        for r, c in blocks:
            rows = slice(r * rb, (r + 1) * rb)
            cols = slice(c * tc, (c + 1) * tc)
            if nxt_tile is not None and live(nxt_tile, r, c):
                pm = logits_block(nxt_tile, r, c, s_refs[nxt])
                mx[c] = pm if mx[c] is None else jnp.maximum(mx[c], pm)
            if live((qi, kv), r, c):
                p = jnp.exp2(s_refs[cur][rows, cols] - m_new[c])
                ps = fold8(p, jnp.sum)
                ls[c] = ps if ls[c] is None else ls[c] + ps
                p_refs[cur][rows, cols] = p.astype(p_refs[cur].dtype)
            if prv_tile is not None and r == nr - 1:
                nk = key_rows(prv_tile, c)
                k0 = prv_tile[1] * t
                pv[c] = jnp.dot(vt_ref[:, k0:k0 + nk], p_refs[nxt][0:nk, cols],
                                preferred_element_type=f32)
        lsum = [jnp.sum(ls[c], axis=0, keepdims=True) for c in range(nc)]
        l_prev_block = l_run
        l_run = lsum if kv == 0 else [alpha[c] * l_run[c] + lsum[c] for c in range(nc)]
        if prv_tile is not None:
            for c in range(nc):
                cols = slice(c * tc, (c + 1) * tc)
                if prv_tile[1] == 0:
                    acc_ref[:, cols] = pv[c]
                else:
                    acc_ref[:, cols] = alpha_prev[c] * acc_ref[:, cols] + pv[c]
            if prv_tile[1] == prv_tile[0]:
                l_fin = jnp.concatenate(l_prev_block, axis=1)
                o_ref[0, prv_tile[0] * t:(prv_tile[0] + 1) * t, :] = (
                    acc_ref[...] / l_fin).T.astype(o_ref.dtype)
        m_run, alpha_prev = m_new, alpha
        if nxt_tile is not None:
            mc = tile_max(mx)

    last = len(tiles) - 1
    qi, kv = tiles[last]
    for c in range(nc):
        cols = slice(c * tc, (c + 1) * tc)
        nk = key_rows(tiles[last], c)
        d = jnp.dot(vt_ref[:, kv * t:kv * t + nk], p_refs[last % 2][0:nk, cols],
                    preferred_element_type=f32)
        acc_ref[:, cols] = d if kv == 0 else alpha_prev[c] * acc_ref[:, cols] + d
    o_ref[0, qi * t:(qi + 1) * t, :] = (
        acc_ref[...] / jnp.concatenate(l_run, axis=1)).T.astype(o_ref.dtype)


def _fox_attention(qkv, qx, kx, *, t=512, tc=256, rb=256):
    bsz, s, d3 = qkv.shape
    d = d3 // 3
    nh = d // HEAD_DIM
    nt = s // t
    f32 = jnp.float32
    bf = jnp.bfloat16
    return pl.pallas_call(
        functools.partial(_fox_kernel, t=t, tc=tc, rb=rb),
        grid=(bsz, nh),
        in_specs=[
            pl.BlockSpec((1, s, HEAD_DIM), lambda b, h: (b, 0, h)),
            pl.BlockSpec((1, s, LANES), lambda b, h: (b, 0, 0)),
            pl.BlockSpec((1, s, HEAD_DIM), lambda b, h: (b, 0, nh + h)),
            pl.BlockSpec((1, s, LANES), lambda b, h: (b, 0, 0)),
            pl.BlockSpec((1, s, HEAD_DIM), lambda b, h: (b, 0, 2 * nh + h)),
        ],
        out_specs=pl.BlockSpec((1, s, HEAD_DIM), lambda b, h: (b, 0, h)),
        out_shape=jax.ShapeDtypeStruct((bsz, s, d), bf),
        scratch_shapes=[pltpu.VMEM((nt, t, HEAD_DIM + LANES), bf),
                        pltpu.VMEM((nt, t, HEAD_DIM + LANES), bf),
                        pltpu.VMEM((HEAD_DIM, s), bf),
                        pltpu.VMEM((t, t), f32),
                        pltpu.VMEM((t, t), f32),
                        pltpu.VMEM((t, t), bf),
                        pltpu.VMEM((t, t), bf),
                        pltpu.VMEM((HEAD_DIM, t), f32)],
        compiler_params=_params(("arbitrary", "arbitrary")),
        name="fox_attention",
    )(qkv, qx, qkv, kx, qkv)


def _fox_from_fz(qkv, fz, b_f):
    qx, kx = _logf_cumsum(fz, b_f)
    return _fox_attention(qkv, qx, kx)


def _out_proj_kernel(h_ref, o_ref, w_ref, out_ref):
    out_ref[...] = h_ref[...] + jnp.dot(o_ref[...], w_ref[...],
                                        preferred_element_type=jnp.float32)


def _out_proj(h, o, w, *, tm=512):
    n, d = h.shape
    return pl.pallas_call(
        _out_proj_kernel,
        grid=(n // tm,),
        in_specs=[pl.BlockSpec((tm, d), lambda i: (i, 0)),
                  pl.BlockSpec((tm, d), lambda i: (i, 0)),
                  pl.BlockSpec((d, d), lambda i: (0, 0))],
        out_specs=pl.BlockSpec((tm, d), lambda i: (i, 0)),
        out_shape=jax.ShapeDtypeStruct((n, d), jnp.float32),
        compiler_params=_params(("arbitrary",)),
        name="out_proj",
    )(h, o, w)


def _mlp_kernel(h_ref, g_ref, w1_ref, w2_ref, out_ref, u_ref, acc_ref):
    f = pl.program_id(1)

    @pl.when(f == 0)
    def _():
        x = h_ref[...]
        r = lax.rsqrt(jnp.mean(x * x, axis=-1, keepdims=True) + EPS)
        u_ref[...] = (x * r * g_ref[...]).astype(u_ref.dtype)
        acc_ref[...] = jnp.zeros_like(acc_ref)

    a = jnp.dot(u_ref[...], w1_ref[...], preferred_element_type=jnp.float32)
    a = jnp.maximum(a, 0.0)
    a = (a * a).astype(w2_ref.dtype)
    acc_ref[...] += jnp.dot(a, w2_ref[...], preferred_element_type=jnp.float32)

    @pl.when(f == pl.num_programs(1) - 1)
    def _():
        out_ref[...] = h_ref[...] + acc_ref[...]


def _mlp(h, g, w1, w2, *, tm=512, tf=1024):
    n, d = h.shape
    dff = w1.shape[1]
    return pl.pallas_call(
        _mlp_kernel,
        grid=(n // tm, dff // tf),
        in_specs=[pl.BlockSpec((tm, d), lambda i, f: (i, 0)),
                  pl.BlockSpec((1, d), lambda i, f: (0, 0)),
                  pl.BlockSpec((d, tf), lambda i, f: (0, f)),
                  pl.BlockSpec((tf, d), lambda i, f: (f, 0))],
        out_specs=pl.BlockSpec((tm, d), lambda i, f: (i, 0)),
        out_shape=jax.ShapeDtypeStruct((n, d), jnp.float32),
        scratch_shapes=[pltpu.VMEM((tm, d), jnp.bfloat16),
                        pltpu.VMEM((tm, d), jnp.float32)],
        compiler_params=_params(("arbitrary", "arbitrary")),
        name="sq_relu_mlp",
    )(h, g.reshape(1, d), w1, w2)


_KB = _GROUP
_NBLK = _WIN // _KB


def _band_kernel(q_ref, k_ref, v_ref, bias_ref, o_ref, vt_ref, s0_ref, s1_ref, p0_ref, p1_ref):
    ng = vt_ref.shape[1] // _GROUP
    f32 = jnp.float32
    s_refs, p_refs = (s0_ref, s1_ref), (p0_ref, p1_ref)

    for j in range(ng):
        vt_ref[:, j * _KB:(j + 1) * _KB] = v_ref[0, j * _KB:(j + 1) * _KB, :].T

    def fold8(x, op):
        return op(x.reshape(x.shape[0] // 8, 8, x.shape[1]), axis=0)

    def rows_of(g):
        return slice(g * _GROUP, (g + 1) * _GROUP)

    def key_blocks(g):
        return [(blk, g - (_NBLK - 1) + blk) for blk in range(_NBLK) if g - (_NBLK - 1) + blk >= 0]

    def logits_block(g, blk, kb, s_out):
        s = lax.dot_general(k_ref[0, rows_of(kb), :], q_ref[0, rows_of(g), :], _NT,
                            preferred_element_type=f32) + bias_ref[0, blk * _KB:(blk + 1) * _KB, :]
        s_out[blk * _KB:(blk + 1) * _KB, :] = s
        return fold8(s, jnp.max)

    def values(g, p_ref):
        blks = key_blocks(g)
        (b0, k0), nb = blks[0], len(blks)
        return jnp.dot(vt_ref[:, k0 * _KB:(k0 + nb) * _KB], p_ref[b0 * _KB:(b0 + nb) * _KB, :],
                       preferred_element_type=f32)

    def write_out(g, pv, l):
        o_ref[0, rows_of(g), :] = (pv / l).T.astype(o_ref.dtype)

    mx = None
    for blk, kb in key_blocks(0):
        pm = logits_block(0, blk, kb, s_refs[0])
        mx = pm if mx is None else jnp.maximum(mx, pm)
    m = jnp.max(mx, axis=0, keepdims=True)

    l_prev = None
    for g in range(ng):
        cur, nxt = g % 2, 1 - g % 2
        nxt_blocks = key_blocks(g + 1) if g + 1 < ng else []
        cur_blocks = key_blocks(g)
        mx = ls = None
        for i in range(_NBLK):
            if i < len(nxt_blocks):
                blk, kb = nxt_blocks[i]
                pm = logits_block(g + 1, blk, kb, s_refs[nxt])
                mx = pm if mx is None else jnp.maximum(mx, pm)
            if i < len(cur_blocks):
                blk, _ = cur_blocks[i]
                p = jnp.exp2(s_refs[cur][blk * _KB:(blk + 1) * _KB, :] - m)
                ps = fold8(p, jnp.sum)
                ls = ps if ls is None else ls + ps
                p_refs[cur][blk * _KB:(blk + 1) * _KB, :] = p.astype(p_refs[cur].dtype)
        if g > 0:
            write_out(g - 1, values(g - 1, p_refs[nxt]), l_prev)
        l_prev = jnp.sum(ls, axis=0, keepdims=True)
        if nxt_blocks:
            m = jnp.max(mx, axis=0, keepdims=True)

    write_out(ng - 1, values(ng - 1, p_refs[(ng - 1) % 2]), l_prev)


def _band_attention(kvq, bias_t):
    bsz, s, d3 = kvq.shape
    d = d3 // 3
    nh = d // HEAD_DIM
    ng = s // _GROUP
    f32 = jnp.float32
    bf = jnp.bfloat16
    return pl.pallas_call(
        _band_kernel,
        grid=(bsz, nh),
        in_specs=[
            pl.BlockSpec((1, s, HEAD_DIM), lambda b, h: (b, 0, 2 * nh + h)),
            pl.BlockSpec((1, s, HEAD_DIM), lambda b, h: (b, 0, h)),
            pl.BlockSpec((1, s, HEAD_DIM), lambda b, h: (b, 0, nh + h)),
            pl.BlockSpec((1, _WIN, _GROUP), lambda b, h: (h, 0, 0)),
        ],
        out_specs=pl.BlockSpec((1, s, HEAD_DIM), lambda b, h: (b, 0, h)),
        out_shape=jax.ShapeDtypeStruct((bsz, s, d), bf),
        scratch_shapes=[pltpu.VMEM((HEAD_DIM, s), bf),
                        pltpu.VMEM((_WIN, _GROUP), f32),
                        pltpu.VMEM((_WIN, _GROUP), f32),
                        pltpu.VMEM((_WIN, _GROUP), bf),
                        pltpu.VMEM((_WIN, _GROUP), bf)],
        compiler_params=_params(("arbitrary", "arbitrary")),
        name="band_attention",
    )(kvq, kvq, kvq, bias_t)


_BIAS_SPAN = _WIN + _GROUP


def _band_bias_kernel(g_ref, o_ref):
    row = lax.broadcasted_iota(jnp.int32, (_KB, 2 * _GROUP), 0)
    for blk in range(_NBLK):
        start = (_NBLK - 1 - blk) * _KB
        x = jnp.broadcast_to(g_ref[0, :, start:start + 2 * _GROUP], (_KB, 2 * _GROUP))
        shift = 1
        while shift < _KB:
            x = jnp.where((row & shift) != 0, pltpu.roll(x, shift, axis=1), x)
            shift *= 2
        b = x[:, _GROUP:]
        j = lax.broadcasted_iota(jnp.int32, (_KB, _GROUP), 0) + blk * _KB
        i = lax.broadcasted_iota(jnp.int32, (_KB, _GROUP), 1)
        km = j - (i // CHUNK) * CHUNK
        o_ref[0, blk * _KB:(blk + 1) * _KB, :] = jnp.where((km >= 0) & (km < _BAND), b, -jnp.inf)


def _band_bias(rel_table):
    nh = rel_table.shape[0]
    k = jnp.arange(_BIAS_SPAN)
    idx = jnp.clip(k - _WIN + _PAD, -(CHUNK - 1), REL_CLIP) + (CHUNK - 1)
    g = (rel_table[:, idx].astype(jnp.float32) * LOG2E).reshape(nh, 1, _BIAS_SPAN)
    return pl.pallas_call(
        _band_bias_kernel,
        grid=(nh,),
        in_specs=[pl.BlockSpec((1, 1, _BIAS_SPAN), lambda h: (h, 0, 0))],
        out_specs=pl.BlockSpec((1, _WIN, _GROUP), lambda h: (h, 0, 0)),
        out_shape=jax.ShapeDtypeStruct((nh, _WIN, _GROUP), jnp.float32),
        compiler_params=_params(("arbitrary",)),
        name="band_bias",
    )(g)


def kernel(x, a_norm_g, a_w_in, a_b_f, a_q_g, a_k_g, a_w_out, mlp_norm_g, mlp_w1, mlp_w2,
           kv_norm_g, kv_w, kv_k_g, b_norm_g, b_w_q, b_q_g, b_rel, b_w_out):
    bsz, s, d = x.shape
    n = bsz * s
    nh = N_HEADS
    bf = jnp.bfloat16
    q_scale = HEAD_DIM ** -0.5 * LOG2E

    h = x.reshape(n, d)

    w_in = a_w_in[0]
    w_qkv = w_in[:, :3 * d].astype(bf)
    w_f = jnp.pad(w_in[:, 3 * d:], ((0, 0), (0, LANES - nh))).astype(bf)
    qkv, fz = _proj(h, [a_norm_g[0]], w_qkv, [a_q_g[0] * q_scale, a_k_g[0]],
                    [(0, 0), (0, 1), (0, None)], w_f)
    b_f = jnp.pad(a_b_f[0], (0, LANES - nh)).reshape(1, LANES)
    o = _fox_from_fz(qkv.reshape(bsz, s, 3 * d), fz.reshape(bsz, s, LANES), b_f)
    h = _out_proj(h, o.reshape(n, d), a_w_out[0].astype(bf))
    h = _mlp(h, mlp_norm_g[0], mlp_w1[0].astype(bf), mlp_w2[0].astype(bf))

    w_kvq = jnp.concatenate([kv_w.astype(bf), b_w_q[0].astype(bf)], axis=1)
    kvq = _proj(h, [kv_norm_g, b_norm_g[0]], w_kvq, [kv_k_g, b_q_g[0] * q_scale],
                [(0, 0), (0, None), (1, 1)])

    o = _band_attention(kvq.reshape(bsz, s, 3 * d), _band_bias(b_rel[0]))
    h = _out_proj(h, o.reshape(n, d), b_w_out[0].astype(bf))
    h = _mlp(h, mlp_norm_g[1], mlp_w1[1].astype(bf), mlp_w2[1].astype(bf))
    return h.reshape(bsz, s, d)
```

```python
import functools
import math

import numpy as np

import jax
import jax.numpy as jnp
from jax import lax
from jax.experimental import pallas as pl
from jax.experimental.pallas import tpu as pltpu

N_HEADS = 16
HEAD_DIM = 128
CHUNK = 64
N_PREV_CHUNKS = 8
REL_CLIP = 256
EPS = 1e-6
LOG2E = math.log2(math.e)

LANES = 128
VMEM_LIMIT = 60 * 1024 * 1024

_BAND = (N_PREV_CHUNKS + 1) * CHUNK
_PAD = N_PREV_CHUNKS * CHUNK
_GROUP = 4 * CHUNK
_WIN = _PAD + _GROUP


def _params(sem):
    return pltpu.CompilerParams(dimension_semantics=sem, vmem_limit_bytes=VMEM_LIMIT)


def _proj_kernel(*refs, parts, n_w, has_f, tn):
    x_ref, g_ref, gain_ref = refs[:3]
    w_refs = refs[3:3 + n_w]
    if has_f:
        wf_ref, o_ref, f_ref, u_ref = refs[3 + n_w:]
    else:
        o_ref, u_ref = refs[3 + n_w:]
    w_parts = [(w_ref, c) for w_ref in w_refs for c in range(w_ref.shape[1] // tn)]
    n_norms = u_ref.shape[0]
    x = x_ref[...]
    xn = x * lax.rsqrt(jnp.mean(x * x, axis=-1, keepdims=True) + EPS)
    for k in range(n_norms):
        u_ref[k] = (xn * g_ref[k:k + 1, :]).astype(u_ref.dtype)
    if has_f:
        f_ref[...] = jnp.dot(u_ref[0], wf_ref[...], preferred_element_type=jnp.float32)
    for j, (norm_idx, gain_idx) in enumerate(parts):
        w_ref, c = w_parts[j]
        y = jnp.dot(u_ref[norm_idx], w_ref[:, c * tn:(c + 1) * tn],
                    preferred_element_type=jnp.float32)
        if gain_idx is None:
            o_ref[:, j * tn:(j + 1) * tn] = y.astype(o_ref.dtype)
            continue
        gain = gain_ref[gain_idx:gain_idx + 1, :]
        for c in range(tn // HEAD_DIM):
            ys = y[:, c * HEAD_DIM:(c + 1) * HEAD_DIM]
            r = lax.rsqrt(jnp.mean(ys * ys, axis=-1, keepdims=True) + EPS)
            sl = slice(j * tn + c * HEAD_DIM, j * tn + (c + 1) * HEAD_DIM)
            o_ref[:, sl] = (ys * r * gain).astype(o_ref.dtype)


def _proj(x, norm_gains, ws, head_gains, parts, wf=None, *, tm=512, tn=2048):
    n, d = x.shape
    nout = sum(w.shape[1] for w in ws)
    assert len(parts) * tn == nout and all(w.shape[1] % tn == 0 for w in ws)
    has_f = wf is not None
    n_norms = len(norm_gains)
    once = pl.Buffered(1)
    in_specs = [
        pl.BlockSpec((tm, d), lambda i: (i, 0)),
        pl.BlockSpec((n_norms, d), lambda i: (0, 0)),
        pl.BlockSpec((len(head_gains), HEAD_DIM), lambda i: (0, 0)),
    ] + [pl.BlockSpec(w.shape, lambda i: (0, 0), pipeline_mode=once) for w in ws]
    args = [x, jnp.stack(norm_gains), jnp.stack(head_gains), *ws]
    out_shape = [jax.ShapeDtypeStruct((n, nout), jnp.bfloat16)]
    out_specs = [pl.BlockSpec((tm, nout), lambda i: (i, 0))]
    if has_f:
        in_specs.append(pl.BlockSpec((d, LANES), lambda i: (0, 0), pipeline_mode=once))
        args.append(wf)
        out_shape.append(jax.ShapeDtypeStruct((n, LANES), jnp.float32))
        out_specs.append(pl.BlockSpec((tm, LANES), lambda i: (i, 0)))
    res = pl.pallas_call(
        functools.partial(_proj_kernel, parts=tuple(parts), n_w=len(ws), has_f=has_f, tn=tn),
        grid=(n // tm,),
        in_specs=in_specs,
        out_specs=out_specs,
        out_shape=out_shape,
        scratch_shapes=[pltpu.VMEM((n_norms, tm, d), jnp.bfloat16)],
        compiler_params=_params(("arbitrary",)),
        name="norm_proj",
    )(*args)
    return res if has_f else res[0]


_HEAD_LANES = LANES // N_HEADS
_N_TERMS = 3


def _bias_column_constants():
    sel = np.zeros((_N_TERMS * LANES, 2 * LANES), np.float32)
    one = np.zeros((1, 2 * LANES), np.float32)
    for h in range(N_HEADS):
        for term in range(_N_TERMS):
            sel[term * LANES + h, h * _HEAD_LANES + term] = -1.0
            sel[term * LANES + h, LANES + h * _HEAD_LANES + _N_TERMS + term] = 1.0
            one[0, h * _HEAD_LANES + _N_TERMS + term] = 1.0
            one[0, LANES + h * _HEAD_LANES + term] = 1.0
    return jnp.asarray(sel, jnp.bfloat16), jnp.asarray(one, jnp.float32)


def _logf_cumsum_kernel(f_ref, b_ref, sel_ref, one_ref, qx_ref, kx_ref):
    z = f_ref[0] + b_ref[...]
    c = jnp.minimum(z, 0.0) - jnp.log1p(jnp.exp(-jnp.abs(z)))
    s = c.shape[0]
    row = lax.broadcasted_iota(jnp.int32, c.shape, 0)
    d = 1
    while d < s:
        c = c + jnp.where(row >= d, pltpu.roll(c, d, axis=0), 0.0)
        d *= 2
    c = c * LOG2E
    hi = c.astype(jnp.bfloat16)
    r = c - hi.astype(jnp.float32)
    mid = r.astype(jnp.bfloat16)
    lo = (r - mid.astype(jnp.float32)).astype(jnp.bfloat16)
    terms = jnp.concatenate([hi, mid, lo], axis=1)
    y = jnp.dot(terms, sel_ref[...], preferred_element_type=jnp.float32) + one_ref[...]
    kx_ref[0] = y[:, :LANES].astype(kx_ref.dtype)
    qx_ref[0] = y[:, LANES:].astype(qx_ref.dtype)


def _logf_cumsum(fz, b):
    bsz, s, _ = fz.shape
    sel, one = _bias_column_constants()
    spec = pl.BlockSpec((1, s, LANES), lambda i: (i, 0, 0))
    out = jax.ShapeDtypeStruct((bsz, s, LANES), jnp.bfloat16)
    return pl.pallas_call(
        _logf_cumsum_kernel,
        grid=(bsz,),
        in_specs=[spec, pl.BlockSpec((1, LANES), lambda i: (0, 0)),
                  pl.BlockSpec(sel.shape, lambda i: (0, 0)),
                  pl.BlockSpec(one.shape, lambda i: (0, 0))],
        out_specs=[spec, spec],
        out_shape=[out, out],
        compiler_params=_params(("arbitrary",)),
        name="logf_cumsum",
    )(fz, b, sel, one)


_NT = (((1,), (1,)), ((), ()))


def _fox_kernel(q_ref, qx_ref, k_ref, kx_ref, v_ref, o_ref,
                qa_ref, ka_ref, vt_ref, s0_ref, s1_ref, p0_ref, p1_ref, acc_ref, *, t, tc, rb):
    nt = ka_ref.shape[0]
    nc = t // tc
    nr = t // rb
    f32 = jnp.float32
    s_refs, p_refs = (s0_ref, s1_ref), (p0_ref, p1_ref)
    tiles = [(qi, kv) for qi in range(nt) for kv in range(qi + 1)]

    lane = lax.broadcasted_iota(jnp.int32, (t, LANES), 1)
    own_lanes = lane // _HEAD_LANES == pl.program_id(1)

    for j in range(nt):
        rows = slice(j * t, (j + 1) * t)
        ka_ref[j, :, 0:HEAD_DIM] = k_ref[0, rows, :]
        ka_ref[j, :, HEAD_DIM:] = kx_ref[0, rows, :]
        qa_ref[j, :, 0:HEAD_DIM] = q_ref[0, rows, :]
        qx = qx_ref[0, rows, :]
        qa_ref[j, :, HEAD_DIM:] = jnp.where(own_lanes, qx, jnp.zeros_like(qx))
        vt_ref[:, rows] = v_ref[0, rows, :].T

    def fold8(x, op):
        return op(x.reshape(x.shape[0] // 8, 8, x.shape[1]), axis=0)

    def live(tile, r, c):
        return tile[1] < tile[0] or r * rb < (c + 1) * tc

    def key_rows(tile, c):
        return t if tile[1] < tile[0] else min(t, -(-(c + 1) * tc // rb) * rb)

    def logits_block(tile, r, c, s_out):
        qi, kv = tile
        s = lax.dot_general(ka_ref[kv, r * rb:(r + 1) * rb, :],
                            qa_ref[qi, c * tc:(c + 1) * tc, :], _NT,
                            preferred_element_type=f32)
        if kv == qi and (r + 1) * rb > c * tc:
            key = lax.broadcasted_iota(jnp.int32, s.shape, 0) + r * rb
            qry = lax.broadcasted_iota(jnp.int32, s.shape, 1) + c * tc
            s = jnp.where(key <= qry, s, -jnp.inf)
        s_out[r * rb:(r + 1) * rb, c * tc:(c + 1) * tc] = s
        return fold8(s, jnp.max)

    blocks = [(r, c) for r in range(nr) for c in range(nc)]

    def tile_max(mx):
        return [jnp.max(mx[c], axis=0, keepdims=True) for c in range(nc)]

    mx = [None] * nc
    for r, c in blocks:
        if not live(tiles[0], r, c):
            continue
        pm = logits_block(tiles[0], r, c, s_refs[0])
        mx[c] = pm if mx[c] is None else jnp.maximum(mx[c], pm)
    mc = tile_max(mx)

    m_run = l_run = alpha_prev = None
    for n, (qi, kv) in enumerate(tiles):
        cur, nxt = n % 2, 1 - n % 2
        nxt_tile = tiles[n + 1] if n + 1 < len(tiles) else None
        prv_tile = tiles[n - 1] if n > 0 else None
        if kv == 0:
            m_new = mc
            alpha = None
        else:
            m_new = [jnp.maximum(m_run[c], mc[c]) for c in range(nc)]
            alpha = [jnp.exp2(m_run[c] - m_new[c]) for c in range(nc)]
        mx = [None] * nc
        ls = [None] * nc
        pv = [None] * nc
        for r, c in blocks:
            rows = slice(r * rb, (r + 1) * rb)
            cols = slice(c * tc, (c + 1) * tc)
            if nxt_tile is not None and live(nxt_tile, r, c):
                pm = logits_block(nxt_tile, r, c, s_refs[nxt])
                mx[c] = pm if mx[c] is None else jnp.maximum(mx[c], pm)
            if live((qi, kv), r, c):
                p = jnp.exp2(s_refs[cur][rows, cols] - m_new[c])
                ps = fold8(p, jnp.sum)
                ls[c] = ps if ls[c] is None else ls[c] + ps
                p_refs[cur][rows, cols] = p.astype(p_refs[cur].dtype)
            if prv_tile is not None and r == nr - 1:
                nk = key_rows(prv_tile, c)
                k0 = prv_tile[1] * t
                pv[c] = jnp.dot(vt_ref[:, k0:k0 + nk], p_refs[nxt][0:nk, cols],
                                preferred_element_type=f32)
        lsum = [jnp.sum(ls[c], axis=0, keepdims=True) for c in range(nc)]
        l_prev_block = l_run
        l_run = lsum if kv == 0 else [alpha[c] * l_run[c] + lsum[c] for c in range(nc)]
        if prv_tile is not None:
            for c in range(nc):
                cols = slice(c * tc, (c + 1) * tc)
                if prv_tile[1] == 0:
                    acc_ref[:, cols] = pv[c]
                else:
                    acc_ref[:, cols] = alpha_prev[c] * acc_ref[:, cols] + pv[c]
            if prv_tile[1] == prv_tile[0]:
                l_fin = jnp.concatenate(l_prev_block, axis=1)
                o_ref[0, prv_tile[0] * t:(prv_tile[0] + 1) * t, :] = (
                    acc_ref[...] / l_fin).T.astype(o_ref.dtype)
        m_run, alpha_prev = m_new, alpha
        if nxt_tile is not None:
            mc = tile_max(mx)

    last = len(tiles) - 1
    qi, kv = tiles[last]
    for c in range(nc):
        cols = slice(c * tc, (c + 1) * tc)
        nk = key_rows(tiles[last], c)
        d = jnp.dot(vt_ref[:, kv * t:kv * t + nk], p_refs[last % 2][0:nk, cols],
                    preferred_element_type=f32)
        acc_ref[:, cols] = d if kv == 0 else alpha_prev[c] * acc_ref[:, cols] + d
    o_ref[0, qi * t:(qi + 1) * t, :] = (
        acc_ref[...] / jnp.concatenate(l_run, axis=1)).T.astype(o_ref.dtype)


def _fox_attention(qkv, qx, kx, *, t=512, tc=256, rb=256):
    bsz, s, d3 = qkv.shape
    d = d3 // 3
    nh = d // HEAD_DIM
    nt = s // t
    f32 = jnp.float32
    bf = jnp.bfloat16
    return pl.pallas_call(
        functools.partial(_fox_kernel, t=t, tc=tc, rb=rb),
        grid=(bsz, nh),
        in_specs=[
            pl.BlockSpec((1, s, HEAD_DIM), lambda b, h: (b, 0, h)),
            pl.BlockSpec((1, s, LANES), lambda b, h: (b, 0, 0)),
            pl.BlockSpec((1, s, HEAD_DIM), lambda b, h: (b, 0, nh + h)),
            pl.BlockSpec((1, s, LANES), lambda b, h: (b, 0, 0)),
            pl.BlockSpec((1, s, HEAD_DIM), lambda b, h: (b, 0, 2 * nh + h)),
        ],
        out_specs=pl.BlockSpec((1, s, HEAD_DIM), lambda b, h: (b, 0, h)),
        out_shape=jax.ShapeDtypeStruct((bsz, s, d), bf),
        scratch_shapes=[pltpu.VMEM((nt, t, HEAD_DIM + LANES), bf),
                        pltpu.VMEM((nt, t, HEAD_DIM + LANES), bf),
                        pltpu.VMEM((HEAD_DIM, s), bf),
                        pltpu.VMEM((t, t), f32),
                        pltpu.VMEM((t, t), f32),
                        pltpu.VMEM((t, t), bf),
                        pltpu.VMEM((t, t), bf),
                        pltpu.VMEM((HEAD_DIM, t), f32)],
        compiler_params=_params(("arbitrary", "arbitrary")),
        name="fox_attention",
    )(qkv, qx, qkv, kx, qkv)


def _fox_from_fz(qkv, fz, b_f):
    qx, kx = _logf_cumsum(fz, b_f)
    return _fox_attention(qkv, qx, kx)


def _out_proj_kernel(h_ref, o_ref, w_ref, out_ref):
    out_ref[...] = h_ref[...] + jnp.dot(o_ref[...], w_ref[...],
                                        preferred_element_type=jnp.float32)


def _out_proj(h, o, w, *, tm=512):
    n, d = h.shape
    return pl.pallas_call(
        _out_proj_kernel,
        grid=(n // tm,),
        in_specs=[pl.BlockSpec((tm, d), lambda i: (i, 0)),
                  pl.BlockSpec((tm, d), lambda i: (i, 0)),
                  pl.BlockSpec((d, d), lambda i: (0, 0))],
        out_specs=pl.BlockSpec((tm, d), lambda i: (i, 0)),
        out_shape=jax.ShapeDtypeStruct((n, d), jnp.float32),
        compiler_params=_params(("arbitrary",)),
        name="out_proj",
    )(h, o, w)


def _mlp_kernel(h_ref, g_ref, w1_ref, w2_ref, out_ref, u_ref):
    f = pl.program_id(1)

    @pl.when(f == 0)
    def _():
        x = h_ref[...]
        r = lax.rsqrt(jnp.mean(x * x, axis=-1, keepdims=True) + EPS)
        u_ref[...] = (x * r * g_ref[...]).astype(u_ref.dtype)
        out_ref[...] = x

    a = jnp.dot(u_ref[...], w1_ref[...], preferred_element_type=jnp.float32)
    a = jnp.maximum(a, 0.0)
    a = (a * a).astype(w2_ref.dtype)
    out_ref[...] += jnp.dot(a, w2_ref[...], preferred_element_type=jnp.float32)


def _mlp(h, g, w1, w2, *, tm=512, tf=2048):
    n, d = h.shape
    dff = w1.shape[1]
    return pl.pallas_call(
        _mlp_kernel,
        grid=(n // tm, dff // tf),
        in_specs=[pl.BlockSpec((tm, d), lambda i, f: (i, 0)),
                  pl.BlockSpec((1, d), lambda i, f: (0, 0)),
                  pl.BlockSpec((d, tf), lambda i, f: (0, f)),
                  pl.BlockSpec((tf, d), lambda i, f: (f, 0))],
        out_specs=pl.BlockSpec((tm, d), lambda i, f: (i, 0)),
        out_shape=jax.ShapeDtypeStruct((n, d), jnp.float32),
        scratch_shapes=[pltpu.VMEM((tm, d), jnp.bfloat16)],
        compiler_params=_params(("arbitrary", "arbitrary")),
        name="sq_relu_mlp",
    )(h, g.reshape(1, d), w1, w2)


_KB = _GROUP
_NBLK = _WIN // _KB


def _band_kernel(q_ref, k_ref, v_ref, bias_ref, o_ref, vt_ref, s0_ref, s1_ref, p0_ref, p1_ref):
    ng = vt_ref.shape[1] // _GROUP
    f32 = jnp.float32
    s_refs, p_refs = (s0_ref, s1_ref), (p0_ref, p1_ref)

    for j in range(ng):
        vt_ref[:, j * _KB:(j + 1) * _KB] = v_ref[0, j * _KB:(j + 1) * _KB, :].T

    def fold8(x, op):
        return op(x.reshape(x.shape[0] // 8, 8, x.shape[1]), axis=0)

    def rows_of(g):
        return slice(g * _GROUP, (g + 1) * _GROUP)

    def key_blocks(g):
        return [(blk, g - (_NBLK - 1) + blk) for blk in range(_NBLK) if g - (_NBLK - 1) + blk >= 0]

    def logits_block(g, blk, kb, s_out):
        s = lax.dot_general(k_ref[0, rows_of(kb), :], q_ref[0, rows_of(g), :], _NT,
                            preferred_element_type=f32) + bias_ref[0, blk * _KB:(blk + 1) * _KB, :]
        s_out[blk * _KB:(blk + 1) * _KB, :] = s
        return fold8(s, jnp.max)

    def values(g, p_ref):
        blks = key_blocks(g)
        (b0, k0), nb = blks[0], len(blks)
        return jnp.dot(vt_ref[:, k0 * _KB:(k0 + nb) * _KB], p_ref[b0 * _KB:(b0 + nb) * _KB, :],
                       preferred_element_type=f32)

    def write_out(g, pv, l):
        o_ref[0, rows_of(g), :] = (pv / l).T.astype(o_ref.dtype)

    mx = None
    for blk, kb in key_blocks(0):
        pm = logits_block(0, blk, kb, s_refs[0])
        mx = pm if mx is None else jnp.maximum(mx, pm)
    m = jnp.max(mx, axis=0, keepdims=True)

    l_prev = None
    for g in range(ng):
        cur, nxt = g % 2, 1 - g % 2
        nxt_blocks = key_blocks(g + 1) if g + 1 < ng else []
        cur_blocks = key_blocks(g)
        mx = ls = None
        for i in range(_NBLK):
            if i < len(nxt_blocks):
                blk, kb = nxt_blocks[i]
                pm = logits_block(g + 1, blk, kb, s_refs[nxt])
                mx = pm if mx is None else jnp.maximum(mx, pm)
            if i < len(cur_blocks):
                blk, _ = cur_blocks[i]
                p = jnp.exp2(s_refs[cur][blk * _KB:(blk + 1) * _KB, :] - m)
                ps = fold8(p, jnp.sum)
                ls = ps if ls is None else ls + ps
                p_refs[cur][blk * _KB:(blk + 1) * _KB, :] = p.astype(p_refs[cur].dtype)
        if g > 0:
            write_out(g - 1, values(g - 1, p_refs[nxt]), l_prev)
        l_prev = jnp.sum(ls, axis=0, keepdims=True)
        if nxt_blocks:
            m = jnp.max(mx, axis=0, keepdims=True)

    write_out(ng - 1, values(ng - 1, p_refs[(ng - 1) % 2]), l_prev)


def _band_attention(kvq, bias_t):
    bsz, s, d3 = kvq.shape
    d = d3 // 3
    nh = d // HEAD_DIM
    ng = s // _GROUP
    f32 = jnp.float32
    bf = jnp.bfloat16
    return pl.pallas_call(
        _band_kernel,
        grid=(bsz, nh),
        in_specs=[
            pl.BlockSpec((1, s, HEAD_DIM), lambda b, h: (b, 0, 2 * nh + h)),
            pl.BlockSpec((1, s, HEAD_DIM), lambda b, h: (b, 0, h)),
            pl.BlockSpec((1, s, HEAD_DIM), lambda b, h: (b, 0, nh + h)),
            pl.BlockSpec((1, _WIN, _GROUP), lambda b, h: (h, 0, 0)),
        ],
        out_specs=pl.BlockSpec((1, s, HEAD_DIM), lambda b, h: (b, 0, h)),
        out_shape=jax.ShapeDtypeStruct((bsz, s, d), bf),
        scratch_shapes=[pltpu.VMEM((HEAD_DIM, s), bf),
                        pltpu.VMEM((_WIN, _GROUP), f32),
                        pltpu.VMEM((_WIN, _GROUP), f32),
                        pltpu.VMEM((_WIN, _GROUP), bf),
                        pltpu.VMEM((_WIN, _GROUP), bf)],
        compiler_params=_params(("arbitrary", "arbitrary")),
        name="band_attention",
    )(kvq, kvq, kvq, bias_t)


_BIAS_SPAN = _WIN + _GROUP


def _band_bias_kernel(g_ref, o_ref):
    row = lax.broadcasted_iota(jnp.int32, (_KB, 2 * _GROUP), 0)
    for blk in range(_NBLK):
        start = (_NBLK - 1 - blk) * _KB
        x = jnp.broadcast_to(g_ref[0, :, start:start + 2 * _GROUP], (_KB, 2 * _GROUP))
        shift = 1
        while shift < _KB:
            x = jnp.where((row & shift) != 0, pltpu.roll(x, shift, axis=1), x)
            shift *= 2
        b = x[:, _GROUP:]
        j = lax.broadcasted_iota(jnp.int32, (_KB, _GROUP), 0) + blk * _KB
        i = lax.broadcasted_iota(jnp.int32, (_KB, _GROUP), 1)
        km = j - (i // CHUNK) * CHUNK
        o_ref[0, blk * _KB:(blk + 1) * _KB, :] = jnp.where((km >= 0) & (km < _BAND), b, -jnp.inf)


def _band_bias(rel_table):
    nh = rel_table.shape[0]
    k = jnp.arange(_BIAS_SPAN)
    idx = jnp.clip(k - _WIN + _PAD, -(CHUNK - 1), REL_CLIP) + (CHUNK - 1)
    g = (rel_table[:, idx].astype(jnp.float32) * LOG2E).reshape(nh, 1, _BIAS_SPAN)
    return pl.pallas_call(
        _band_bias_kernel,
        grid=(nh,),
        in_specs=[pl.BlockSpec((1, 1, _BIAS_SPAN), lambda h: (h, 0, 0))],
        out_specs=pl.BlockSpec((1, _WIN, _GROUP), lambda h: (h, 0, 0)),
        out_shape=jax.ShapeDtypeStruct((nh, _WIN, _GROUP), jnp.float32),
        compiler_params=_params(("arbitrary",)),
        name="band_bias",
    )(g)


def kernel(x, a_norm_g, a_w_in, a_b_f, a_q_g, a_k_g, a_w_out, mlp_norm_g, mlp_w1, mlp_w2,
           kv_norm_g, kv_w, kv_k_g, b_norm_g, b_w_q, b_q_g, b_rel, b_w_out):
    bsz, s, d = x.shape
    n = bsz * s
    nh = N_HEADS
    bf = jnp.bfloat16
    q_scale = HEAD_DIM ** -0.5 * LOG2E

    h = x.reshape(n, d)

    w_in = a_w_in[0]
    w_qkv = w_in[:, :3 * d].astype(bf)
    w_f = jnp.pad(w_in[:, 3 * d:], ((0, 0), (0, LANES - nh))).astype(bf)
    qkv, fz = _proj(h, [a_norm_g[0]], [w_qkv], [a_q_g[0] * q_scale, a_k_g[0]],
                    [(0, 0), (0, 1), (0, None)], w_f)
    b_f = jnp.pad(a_b_f[0], (0, LANES - nh)).reshape(1, LANES)
    o = _fox_from_fz(qkv.reshape(bsz, s, 3 * d), fz.reshape(bsz, s, LANES), b_f)
    h = _out_proj(h, o.reshape(n, d), a_w_out[0].astype(bf))
    h = _mlp(h, mlp_norm_g[0], mlp_w1[0].astype(bf), mlp_w2[0].astype(bf))

    kvq = _proj(h, [kv_norm_g, b_norm_g[0]], [kv_w.astype(bf), b_w_q[0].astype(bf)],
                [kv_k_g, b_q_g[0] * q_scale],
                [(0, 0), (0, None), (1, 1)])

    o = _band_attention(kvq.reshape(bsz, s, 3 * d), _band_bias(b_rel[0]))
    h = _out_proj(h, o.reshape(n, d), b_w_out[0].astype(bf))
    h = _mlp(h, mlp_norm_g[1], mlp_w1[1].astype(bf), mlp_w2[1].astype(bf))
    return h.reshape(bsz, s, d)
```

```python
import functools
import math

import numpy as np

import jax
import jax.numpy as jnp
from jax import lax
from jax.experimental import pallas as pl
from jax.experimental.pallas import tpu as pltpu

N_HEADS = 16
HEAD_DIM = 128
CHUNK = 64
N_PREV_CHUNKS = 8
REL_CLIP = 256
EPS = 1e-6
LOG2E = math.log2(math.e)

LANES = 128
VMEM_LIMIT = 60 * 1024 * 1024

_BAND = (N_PREV_CHUNKS + 1) * CHUNK
_PAD = N_PREV_CHUNKS * CHUNK
_GROUP = 4 * CHUNK
_WIN = _PAD + _GROUP


def _params(sem):
    return pltpu.CompilerParams(dimension_semantics=sem, vmem_limit_bytes=VMEM_LIMIT)


def _cast_kernel(w_ref, o_ref):
    o_ref[...] = w_ref[...].astype(o_ref.dtype)


def _cast_leading_cols(w, ncols, *, tr=256):
    rows = w.shape[0]
    assert rows % tr == 0 and ncols % LANES == 0 and ncols <= w.shape[1]
    return pl.pallas_call(
        _cast_kernel,
        grid=(rows // tr,),
        in_specs=[pl.BlockSpec((tr, ncols), lambda i: (i, 0))],
        out_specs=pl.BlockSpec((tr, ncols), lambda i: (i, 0)),
        out_shape=jax.ShapeDtypeStruct((rows, ncols), jnp.bfloat16),
        compiler_params=_params(("arbitrary",)),
        name="cast_cols",
    )(w)


def _proj_kernel(*refs, parts, n_w, has_f, tn):
    x_ref, g_ref, gain_ref = refs[:3]
    w_refs = refs[3:3 + n_w]
    if has_f:
        wf_ref, o_ref, f_ref, u_ref = refs[3 + n_w:]
    else:
        o_ref, u_ref = refs[3 + n_w:]
    w_parts = [(w_ref, c) for w_ref in w_refs for c in range(w_ref.shape[1] // tn)]
    n_norms = u_ref.shape[0]
    x = x_ref[...]
    xn = x * lax.rsqrt(jnp.mean(x * x, axis=-1, keepdims=True) + EPS)
    for k in range(n_norms):
        u_ref[k] = (xn * g_ref[k:k + 1, :]).astype(u_ref.dtype)
    if has_f:
        f_ref[...] = jnp.dot(u_ref[0], wf_ref[...], preferred_element_type=jnp.float32)
    for j, (norm_idx, gain_idx) in enumerate(parts):
        w_ref, c = w_parts[j]
        y = jnp.dot(u_ref[norm_idx], w_ref[:, c * tn:(c + 1) * tn],
                    preferred_element_type=jnp.float32)
        if gain_idx is None:
            o_ref[:, j * tn:(j + 1) * tn] = y.astype(o_ref.dtype)
            continue
        gain = gain_ref[gain_idx:gain_idx + 1, :]
        for c in range(tn // HEAD_DIM):
            ys = y[:, c * HEAD_DIM:(c + 1) * HEAD_DIM]
            r = lax.rsqrt(jnp.mean(ys * ys, axis=-1, keepdims=True) + EPS)
            sl = slice(j * tn + c * HEAD_DIM, j * tn + (c + 1) * HEAD_DIM)
            o_ref[:, sl] = (ys * r * gain).astype(o_ref.dtype)


def _proj(x, norm_gains, ws, head_gains, parts, wf=None, *, tm=512, tn=2048):
    n, d = x.shape
    nout = sum(w.shape[1] for w in ws)
    assert len(parts) * tn == nout and all(w.shape[1] % tn == 0 for w in ws)
    has_f = wf is not None
    n_norms = len(norm_gains)
    once = pl.Buffered(1)
    in_specs = [
        pl.BlockSpec((tm, d), lambda i: (i, 0)),
        pl.BlockSpec((n_norms, d), lambda i: (0, 0)),
        pl.BlockSpec((len(head_gains), HEAD_DIM), lambda i: (0, 0)),
    ] + [pl.BlockSpec(w.shape, lambda i: (0, 0), pipeline_mode=once) for w in ws]
    args = [x, jnp.stack(norm_gains), jnp.stack(head_gains), *ws]
    out_shape = [jax.ShapeDtypeStruct((n, nout), jnp.bfloat16)]
    out_specs = [pl.BlockSpec((tm, nout), lambda i: (i, 0))]
    if has_f:
        in_specs.append(pl.BlockSpec((d, LANES), lambda i: (0, 0), pipeline_mode=once))
        args.append(wf)
        out_shape.append(jax.ShapeDtypeStruct((n, LANES), jnp.float32))
        out_specs.append(pl.BlockSpec((tm, LANES), lambda i: (i, 0)))
    res = pl.pallas_call(
        functools.partial(_proj_kernel, parts=tuple(parts), n_w=len(ws), has_f=has_f, tn=tn),
        grid=(n // tm,),
        in_specs=in_specs,
        out_specs=out_specs,
        out_shape=out_shape,
        scratch_shapes=[pltpu.VMEM((n_norms, tm, d), jnp.bfloat16)],
        compiler_params=_params(("arbitrary",)),
        name="norm_proj",
    )(*args)
    return res if has_f else res[0]


_HEAD_LANES = LANES // N_HEADS
_N_TERMS = 3


def _bias_column_constants():
    sel = np.zeros((_N_TERMS * LANES, 2 * LANES), np.float32)
    one = np.zeros((1, 2 * LANES), np.float32)
    for h in range(N_HEADS):
        for term in range(_N_TERMS):
            sel[term * LANES + h, h * _HEAD_LANES + term] = -1.0
            sel[term * LANES + h, LANES + h * _HEAD_LANES + _N_TERMS + term] = 1.0
            one[0, h * _HEAD_LANES + _N_TERMS + term] = 1.0
            one[0, LANES + h * _HEAD_LANES + term] = 1.0
    return jnp.asarray(sel, jnp.bfloat16), jnp.asarray(one, jnp.float32)


def _logf_cumsum_kernel(f_ref, b_ref, sel_ref, one_ref, qx_ref, kx_ref):
    z = f_ref[0] + b_ref[...]
    c = jnp.minimum(z, 0.0) - jnp.log1p(jnp.exp(-jnp.abs(z)))
    s = c.shape[0]
    row = lax.broadcasted_iota(jnp.int32, c.shape, 0)
    d = 1
    while d < s:
        c = c + jnp.where(row >= d, pltpu.roll(c, d, axis=0), 0.0)
        d *= 2
    c = c * LOG2E
    hi = c.astype(jnp.bfloat16)
    r = c - hi.astype(jnp.float32)
    mid = r.astype(jnp.bfloat16)
    lo = (r - mid.astype(jnp.float32)).astype(jnp.bfloat16)
    terms = jnp.concatenate([hi, mid, lo], axis=1)
    y = jnp.dot(terms, sel_ref[...], preferred_element_type=jnp.float32) + one_ref[...]
    kx_ref[0] = y[:, :LANES].astype(kx_ref.dtype)
    qx_ref[0] = y[:, LANES:].astype(qx_ref.dtype)


def _logf_cumsum(fz, b):
    bsz, s, _ = fz.shape
    sel, one = _bias_column_constants()
    spec = pl.BlockSpec((1, s, LANES), lambda i: (i, 0, 0))
    out = jax.ShapeDtypeStruct((bsz, s, LANES), jnp.bfloat16)
    return pl.pallas_call(
        _logf_cumsum_kernel,
        grid=(bsz,),
        in_specs=[spec, pl.BlockSpec((1, LANES), lambda i: (0, 0)),
                  pl.BlockSpec(sel.shape, lambda i: (0, 0)),
                  pl.BlockSpec(one.shape, lambda i: (0, 0))],
        out_specs=[spec, spec],
        out_shape=[out, out],
        compiler_params=_params(("arbitrary",)),
        name="logf_cumsum",
    )(fz, b, sel, one)


_NT = (((1,), (1,)), ((), ()))


def _fox_kernel(q_ref, qx_ref, k_ref, kx_ref, v_ref, o_ref,
                qa_ref, ka_ref, vt_ref, s0_ref, s1_ref, p0_ref, p1_ref, acc_ref, *, t, tc, rb):
    nt = ka_ref.shape[0]
    nc = t // tc
    nr = t // rb
    f32 = jnp.float32
    s_refs, p_refs = (s0_ref, s1_ref), (p0_ref, p1_ref)
    tiles = [(qi, kv) for qi in range(nt) for kv in range(qi + 1)]

    lane = lax.broadcasted_iota(jnp.int32, (t, LANES), 1)
    own_lanes = lane // _HEAD_LANES == pl.program_id(1)

    for j in range(nt):
        rows = slice(j * t, (j + 1) * t)
        ka_ref[j, :, 0:HEAD_DIM] = k_ref[0, rows, :]
        ka_ref[j, :, HEAD_DIM:] = kx_ref[0, rows, :]
        qa_ref[j, :, 0:HEAD_DIM] = q_ref[0, rows, :]
        qx = qx_ref[0, rows, :]
        qa_ref[j, :, HEAD_DIM:] = jnp.where(own_lanes, qx, jnp.zeros_like(qx))
        vt_ref[:, rows] = v_ref[0, rows, :].T

    def fold8(x, op):
        return op(x.reshape(x.shape[0] // 8, 8, x.shape[1]), axis=0)

    def live(tile, r, c):
        return tile[1] < tile[0] or r * rb < (c + 1) * tc

    def key_rows(tile, c):
        return t if tile[1] < tile[0] else min(t, -(-(c + 1) * tc // rb) * rb)

    def logits_block(tile, r, c, s_out):
        qi, kv = tile
        s = lax.dot_general(ka_ref[kv, r * rb:(r + 1) * rb, :],
                            qa_ref[qi, c * tc:(c + 1) * tc, :], _NT,
                            preferred_element_type=f32)
        if kv == qi and (r + 1) * rb > c * tc:
            key = lax.broadcasted_iota(jnp.int32, s.shape, 0) + r * rb
            qry = lax.broadcasted_iota(jnp.int32, s.shape, 1) + c * tc
            s = jnp.where(key <= qry, s, -jnp.inf)
        s_out[r * rb:(r + 1) * rb, c * tc:(c + 1) * tc] = s
        return fold8(s, jnp.max)

    blocks = [(r, c) for r in range(nr) for c in range(nc)]

    def tile_max(mx):
        return [jnp.max(mx[c], axis=0, keepdims=True) for c in range(nc)]

    mx = [None] * nc
    for r, c in blocks:
        if not live(tiles[0], r, c):
            continue
        pm = logits_block(tiles[0], r, c, s_refs[0])
        mx[c] = pm if mx[c] is None else jnp.maximum(mx[c], pm)
    mc = tile_max(mx)

    m_run = l_run = alpha_prev = None
    for n, (qi, kv) in enumerate(tiles):
        cur, nxt = n % 2, 1 - n % 2
        nxt_tile = tiles[n + 1] if n + 1 < len(tiles) else None
        prv_tile = tiles[n - 1] if n > 0 else None
        if kv == 0:
            m_new = mc
            alpha = None
        else:
            m_new = [jnp.maximum(m_run[c], mc[c]) for c in range(nc)]
            alpha = [jnp.exp2(m_run[c] - m_new[c]) for c in range(nc)]
        mx = [None] * nc
        ls = [None] * nc
        pv = [None] * nc
        for r, c in blocks:
            rows = slice(r * rb, (r + 1) * rb)
            cols = slice(c * tc, (c + 1) * tc)
            if nxt_tile is not None and live(nxt_tile, r, c):
                pm = logits_block(nxt_tile, r, c, s_refs[nxt])
                mx[c] = pm if mx[c] is None else jnp.maximum(mx[c], pm)
            if live((qi, kv), r, c):
                p = jnp.exp2(s_refs[cur][rows, cols] - m_new[c])
                ps = fold8(p, jnp.sum)
                ls[c] = ps if ls[c] is None else ls[c] + ps
                p_refs[cur][rows, cols] = p.astype(p_refs[cur].dtype)
            if prv_tile is not None and r == nr - 1:
                nk = key_rows(prv_tile, c)
                k0 = prv_tile[1] * t
                pv[c] = jnp.dot(vt_ref[:, k0:k0 + nk], p_refs[nxt][0:nk, cols],
                                preferred_element_type=f32)
        lsum = [jnp.sum(ls[c], axis=0, keepdims=True) for c in range(nc)]
        l_prev_block = l_run
        l_run = lsum if kv == 0 else [alpha[c] * l_run[c] + lsum[c] for c in range(nc)]
        if prv_tile is not None:
            for c in range(nc):
                cols = slice(c * tc, (c + 1) * tc)
                if prv_tile[1] == 0:
                    acc_ref[:, cols] = pv[c]
                else:
                    acc_ref[:, cols] = alpha_prev[c] * acc_ref[:, cols] + pv[c]
            if prv_tile[1] == prv_tile[0]:
                l_fin = jnp.concatenate(l_prev_block, axis=1)
                o_ref[0, prv_tile[0] * t:(prv_tile[0] + 1) * t, :] = (
                    acc_ref[...] / l_fin).T.astype(o_ref.dtype)
        m_run, alpha_prev = m_new, alpha
        if nxt_tile is not None:
            mc = tile_max(mx)

    last = len(tiles) - 1
    qi, kv = tiles[last]
    for c in range(nc):
        cols = slice(c * tc, (c + 1) * tc)
        nk = key_rows(tiles[last], c)
        d = jnp.dot(vt_ref[:, kv * t:kv * t + nk], p_refs[last % 2][0:nk, cols],
                    preferred_element_type=f32)
        acc_ref[:, cols] = d if kv == 0 else alpha_prev[c] * acc_ref[:, cols] + d
    o_ref[0, qi * t:(qi + 1) * t, :] = (
        acc_ref[...] / jnp.concatenate(l_run, axis=1)).T.astype(o_ref.dtype)


def _fox_attention(qkv, qx, kx, *, t=512, tc=256, rb=256):
    bsz, s, d3 = qkv.shape
    d = d3 // 3
    nh = d // HEAD_DIM
    nt = s // t
    f32 = jnp.float32
    bf = jnp.bfloat16
    return pl.pallas_call(
        functools.partial(_fox_kernel, t=t, tc=tc, rb=rb),
        grid=(bsz, nh),
        in_specs=[
            pl.BlockSpec((1, s, HEAD_DIM), lambda b, h: (b, 0, h)),
            pl.BlockSpec((1, s, LANES), lambda b, h: (b, 0, 0)),
            pl.BlockSpec((1, s, HEAD_DIM), lambda b, h: (b, 0, nh + h)),
            pl.BlockSpec((1, s, LANES), lambda b, h: (b, 0, 0)),
            pl.BlockSpec((1, s, HEAD_DIM), lambda b, h: (b, 0, 2 * nh + h)),
        ],
        out_specs=pl.BlockSpec((1, s, HEAD_DIM), lambda b, h: (b, 0, h)),
        out_shape=jax.ShapeDtypeStruct((bsz, s, d), bf),
        scratch_shapes=[pltpu.VMEM((nt, t, HEAD_DIM + LANES), bf),
                        pltpu.VMEM((nt, t, HEAD_DIM + LANES), bf),
                        pltpu.VMEM((HEAD_DIM, s), bf),
                        pltpu.VMEM((t, t), f32),
                        pltpu.VMEM((t, t), f32),
                        pltpu.VMEM((t, t), bf),
                        pltpu.VMEM((t, t), bf),
                        pltpu.VMEM((HEAD_DIM, t), f32)],
        compiler_params=_params(("arbitrary", "arbitrary")),
        name="fox_attention",
    )(qkv, qx, qkv, kx, qkv)


def _fox_from_fz(qkv, fz, b_f):
    qx, kx = _logf_cumsum(fz, b_f)
    return _fox_attention(qkv, qx, kx)


def _out_proj_kernel(h_ref, o_ref, w_ref, out_ref):
    out_ref[...] = h_ref[...] + jnp.dot(o_ref[...], w_ref[...],
                                        preferred_element_type=jnp.float32)


def _out_proj(h, o, w, *, tm=512):
    n, d = h.shape
    return pl.pallas_call(
        _out_proj_kernel,
        grid=(n // tm,),
        in_specs=[pl.BlockSpec((tm, d), lambda i: (i, 0)),
                  pl.BlockSpec((tm, d), lambda i: (i, 0)),
                  pl.BlockSpec((d, d), lambda i: (0, 0))],
        out_specs=pl.BlockSpec((tm, d), lambda i: (i, 0)),
        out_shape=jax.ShapeDtypeStruct((n, d), jnp.float32),
        compiler_params=_params(("arbitrary",)),
        name="out_proj",
    )(h, o, w)


def _mlp_kernel(h_ref, g_ref, w1_ref, w2_ref, out_ref, u_ref):
    f = pl.program_id(1)

    @pl.when(f == 0)
    def _():
        x = h_ref[...]
        r = lax.rsqrt(jnp.mean(x * x, axis=-1, keepdims=True) + EPS)
        u_ref[...] = (x * r * g_ref[...]).astype(u_ref.dtype)
        out_ref[...] = x

    a = jnp.dot(u_ref[...], w1_ref[...], preferred_element_type=jnp.float32)
    a = jnp.maximum(a, 0.0)
    a = (a * a).astype(w2_ref.dtype)
    out_ref[...] += jnp.dot(a, w2_ref[...], preferred_element_type=jnp.float32)


def _mlp(h, g, w1, w2, layer, *, tm=512, tf=2048):
    n, d = h.shape
    dff = w1.shape[2]
    return pl.pallas_call(
        _mlp_kernel,
        grid=(n // tm, dff // tf),
        in_specs=[pl.BlockSpec((tm, d), lambda i, f: (i, 0)),
                  pl.BlockSpec((1, d), lambda i, f: (0, 0)),
                  pl.BlockSpec((None, d, tf), lambda i, f: (layer, 0, f)),
                  pl.BlockSpec((None, tf, d), lambda i, f: (layer, f, 0))],
        out_specs=pl.BlockSpec((tm, d), lambda i, f: (i, 0)),
        out_shape=jax.ShapeDtypeStruct((n, d), jnp.float32),
        scratch_shapes=[pltpu.VMEM((tm, d), jnp.bfloat16)],
        compiler_params=_params(("arbitrary", "arbitrary")),
        name="sq_relu_mlp",
    )(h, g.reshape(1, d), w1, w2)


_KB = _GROUP
_NBLK = _WIN // _KB


def _band_kernel(q_ref, k_ref, v_ref, bias_ref, o_ref, vt_ref, s0_ref, s1_ref, p0_ref, p1_ref):
    ng = vt_ref.shape[1] // _GROUP
    f32 = jnp.float32
    s_refs, p_refs = (s0_ref, s1_ref), (p0_ref, p1_ref)

    for j in range(ng):
        vt_ref[:, j * _KB:(j + 1) * _KB] = v_ref[0, j * _KB:(j + 1) * _KB, :].T

    def fold8(x, op):
        return op(x.reshape(x.shape[0] // 8, 8, x.shape[1]), axis=0)

    def rows_of(g):
        return slice(g * _GROUP, (g + 1) * _GROUP)

    def key_blocks(g):
        return [(blk, g - (_NBLK - 1) + blk) for blk in range(_NBLK) if g - (_NBLK - 1) + blk >= 0]

    def logits_block(g, blk, kb, s_out):
        s = lax.dot_general(k_ref[0, rows_of(kb), :], q_ref[0, rows_of(g), :], _NT,
                            preferred_element_type=f32) + bias_ref[0, blk * _KB:(blk + 1) * _KB, :]
        s_out[blk * _KB:(blk + 1) * _KB, :] = s
        return fold8(s, jnp.max)

    def values(g, p_ref):
        blks = key_blocks(g)
        (b0, k0), nb = blks[0], len(blks)
        return jnp.dot(vt_ref[:, k0 * _KB:(k0 + nb) * _KB], p_ref[b0 * _KB:(b0 + nb) * _KB, :],
                       preferred_element_type=f32)

    def write_out(g, pv, l):
        o_ref[0, rows_of(g), :] = (pv / l).T.astype(o_ref.dtype)

    mx = None
    for blk, kb in key_blocks(0):
        pm = logits_block(0, blk, kb, s_refs[0])
        mx = pm if mx is None else jnp.maximum(mx, pm)
    m = jnp.max(mx, axis=0, keepdims=True)

    l_prev = None
    for g in range(ng):
        cur, nxt = g % 2, 1 - g % 2
        nxt_blocks = key_blocks(g + 1) if g + 1 < ng else []
        cur_blocks = key_blocks(g)
        mx = ls = None
        for i in range(_NBLK):
            if i < len(nxt_blocks):
                blk, kb = nxt_blocks[i]
                pm = logits_block(g + 1, blk, kb, s_refs[nxt])
                mx = pm if mx is None else jnp.maximum(mx, pm)
            if i < len(cur_blocks):
                blk, _ = cur_blocks[i]
                p = jnp.exp2(s_refs[cur][blk * _KB:(blk + 1) * _KB, :] - m)
                ps = fold8(p, jnp.sum)
                ls = ps if ls is None else ls + ps
                p_refs[cur][blk * _KB:(blk + 1) * _KB, :] = p.astype(p_refs[cur].dtype)
        if g > 0:
            write_out(g - 1, values(g - 1, p_refs[nxt]), l_prev)
        l_prev = jnp.sum(ls, axis=0, keepdims=True)
        if nxt_blocks:
            m = jnp.max(mx, axis=0, keepdims=True)

    write_out(ng - 1, values(ng - 1, p_refs[(ng - 1) % 2]), l_prev)


def _band_attention(kvq, bias_t):
    bsz, s, d3 = kvq.shape
    d = d3 // 3
    nh = d // HEAD_DIM
    ng = s // _GROUP
    f32 = jnp.float32
    bf = jnp.bfloat16
    return pl.pallas_call(
        _band_kernel,
        grid=(bsz, nh),
        in_specs=[
            pl.BlockSpec((1, s, HEAD_DIM), lambda b, h: (b, 0, 2 * nh + h)),
            pl.BlockSpec((1, s, HEAD_DIM), lambda b, h: (b, 0, h)),
            pl.BlockSpec((1, s, HEAD_DIM), lambda b, h: (b, 0, nh + h)),
            pl.BlockSpec((1, _WIN, _GROUP), lambda b, h: (h, 0, 0)),
        ],
        out_specs=pl.BlockSpec((1, s, HEAD_DIM), lambda b, h: (b, 0, h)),
        out_shape=jax.ShapeDtypeStruct((bsz, s, d), bf),
        scratch_shapes=[pltpu.VMEM((HEAD_DIM, s), bf),
                        pltpu.VMEM((_WIN, _GROUP), f32),
                        pltpu.VMEM((_WIN, _GROUP), f32),
                        pltpu.VMEM((_WIN, _GROUP), bf),
                        pltpu.VMEM((_WIN, _GROUP), bf)],
        compiler_params=_params(("arbitrary", "arbitrary")),
        name="band_attention",
    )(kvq, kvq, kvq, bias_t)


_BIAS_SPAN = _WIN + _GROUP


def _band_bias_kernel(g_ref, o_ref):
    row = lax.broadcasted_iota(jnp.int32, (_KB, 2 * _GROUP), 0)
    for blk in range(_NBLK):
        start = (_NBLK - 1 - blk) * _KB
        x = jnp.broadcast_to(g_ref[0, :, start:start + 2 * _GROUP], (_KB, 2 * _GROUP))
        shift = 1
        while shift < _KB:
            x = jnp.where((row & shift) != 0, pltpu.roll(x, shift, axis=1), x)
            shift *= 2
        b = x[:, _GROUP:]
        j = lax.broadcasted_iota(jnp.int32, (_KB, _GROUP), 0) + blk * _KB
        i = lax.broadcasted_iota(jnp.int32, (_KB, _GROUP), 1)
        km = j - (i // CHUNK) * CHUNK
        o_ref[0, blk * _KB:(blk + 1) * _KB, :] = jnp.where((km >= 0) & (km < _BAND), b, -jnp.inf)


def _band_bias(rel_table):
    nh = rel_table.shape[0]
    k = jnp.arange(_BIAS_SPAN)
    idx = jnp.clip(k - _WIN + _PAD, -(CHUNK - 1), REL_CLIP) + (CHUNK - 1)
    g = (rel_table[:, idx].astype(jnp.float32) * LOG2E).reshape(nh, 1, _BIAS_SPAN)
    return pl.pallas_call(
        _band_bias_kernel,
        grid=(nh,),
        in_specs=[pl.BlockSpec((1, 1, _BIAS_SPAN), lambda h: (h, 0, 0))],
        out_specs=pl.BlockSpec((1, _WIN, _GROUP), lambda h: (h, 0, 0)),
        out_shape=jax.ShapeDtypeStruct((nh, _WIN, _GROUP), jnp.float32),
        compiler_params=_params(("arbitrary",)),
        name="band_bias",
    )(g)


def kernel(x, a_norm_g, a_w_in, a_b_f, a_q_g, a_k_g, a_w_out, mlp_norm_g, mlp_w1, mlp_w2,
           kv_norm_g, kv_w, kv_k_g, b_norm_g, b_w_q, b_q_g, b_rel, b_w_out):
    bsz, s, d = x.shape
    n = bsz * s
    nh = N_HEADS
    bf = jnp.bfloat16
    q_scale = HEAD_DIM ** -0.5 * LOG2E

    h = x.reshape(n, d)

    w_in = a_w_in[0]
    w_qkv = _cast_leading_cols(w_in, 3 * d)
    w_f = jnp.pad(w_in[:, 3 * d:], ((0, 0), (0, LANES - nh))).astype(bf)
    w1_all = mlp_w1.astype(bf)
    w2_all = mlp_w2.astype(bf)
    qkv, fz = _proj(h, [a_norm_g[0]], [w_qkv], [a_q_g[0] * q_scale, a_k_g[0]],
                    [(0, 0), (0, 1), (0, None)], w_f)
    b_f = jnp.pad(a_b_f[0], (0, LANES - nh)).reshape(1, LANES)
    o = _fox_from_fz(qkv.reshape(bsz, s, 3 * d), fz.reshape(bsz, s, LANES), b_f)
    h = _out_proj(h, o.reshape(n, d), a_w_out[0].astype(bf))
    h = _mlp(h, mlp_norm_g[0], w1_all, w2_all, 0)

    kvq = _proj(h, [kv_norm_g, b_norm_g[0]], [kv_w.astype(bf), b_w_q[0].astype(bf)],
                [kv_k_g, b_q_g[0] * q_scale],
                [(0, 0), (0, None), (1, 1)])

    o = _band_attention(kvq.reshape(bsz, s, 3 * d), _band_bias(b_rel[0]))
    h = _out_proj(h, o.reshape(n, d), b_w_out[0].astype(bf))
    h = _mlp(h, mlp_norm_g[1], w1_all, w2_all, 1)
    return h.reshape(bsz, s, d)
```

```python
import functools
import math

import numpy as np

import jax
import jax.numpy as jnp
from jax import lax
from jax.experimental import pallas as pl
from jax.experimental.pallas import tpu as pltpu

N_HEADS = 16
HEAD_DIM = 128
CHUNK = 64
N_PREV_CHUNKS = 8
REL_CLIP = 256
EPS = 1e-6
LOG2E = math.log2(math.e)

LANES = 128
VMEM_LIMIT = 60 * 1024 * 1024

_BAND = (N_PREV_CHUNKS + 1) * CHUNK
_PAD = N_PREV_CHUNKS * CHUNK
_GROUP = 4 * CHUNK
_WIN = _PAD + _GROUP


def _params(sem):
    return pltpu.CompilerParams(dimension_semantics=sem, vmem_limit_bytes=VMEM_LIMIT)


def _cast_kernel(w_ref, tail_ref, o_ref, otail_ref, *, n_tail):
    o_ref[...] = w_ref[...].astype(o_ref.dtype)
    lane = lax.broadcasted_iota(jnp.int32, tail_ref.shape, 1)
    otail_ref[...] = jnp.where(lane < n_tail, tail_ref[...], 0.0).astype(otail_ref.dtype)


def _cast_split_cols(w, ncols, *, tr=256):
    rows, total = w.shape
    n_tail = total - ncols
    assert rows % tr == 0 and ncols % LANES == 0 and 0 < n_tail < LANES
    return pl.pallas_call(
        functools.partial(_cast_kernel, n_tail=n_tail),
        grid=(rows // tr,),
        in_specs=[pl.BlockSpec((tr, ncols), lambda i: (i, 0)),
                  pl.BlockSpec((tr, LANES), lambda i: (i, ncols // LANES))],
        out_specs=[pl.BlockSpec((tr, ncols), lambda i: (i, 0)),
                   pl.BlockSpec((tr, LANES), lambda i: (i, 0))],
        out_shape=[jax.ShapeDtypeStruct((rows, ncols), jnp.bfloat16),
                   jax.ShapeDtypeStruct((rows, LANES), jnp.bfloat16)],
        compiler_params=_params(("arbitrary",)),
        name="cast_cols",
    )(w, w)


def _proj_kernel(*refs, parts, n_w, has_f, tn):
    x_ref, g_ref, gain_ref = refs[:3]
    w_refs = refs[3:3 + n_w]
    if has_f:
        wf_ref, o_ref, f_ref, u_ref = refs[3 + n_w:]
    else:
        o_ref, u_ref = refs[3 + n_w:]
    w_parts = [(w_ref, c) for w_ref in w_refs for c in range(w_ref.shape[1] // tn)]
    n_norms = u_ref.shape[0]
    x = x_ref[...]
    xn = x * lax.rsqrt(jnp.mean(x * x, axis=-1, keepdims=True) + EPS)
    for k in range(n_norms):
        u_ref[k] = (xn * g_ref[k:k + 1, :]).astype(u_ref.dtype)
    if has_f:
        f_ref[...] = jnp.dot(u_ref[0], wf_ref[...], preferred_element_type=jnp.float32)
    for j, (norm_idx, gain_idx) in enumerate(parts):
        w_ref, c = w_parts[j]
        y = jnp.dot(u_ref[norm_idx], w_ref[:, c * tn:(c + 1) * tn],
                    preferred_element_type=jnp.float32)
        if gain_idx is None:
            o_ref[:, j * tn:(j + 1) * tn] = y.astype(o_ref.dtype)
            continue
        gain = gain_ref[gain_idx:gain_idx + 1, :]
        for c in range(tn // HEAD_DIM):
            ys = y[:, c * HEAD_DIM:(c + 1) * HEAD_DIM]
            r = lax.rsqrt(jnp.mean(ys * ys, axis=-1, keepdims=True) + EPS)
            sl = slice(j * tn + c * HEAD_DIM, j * tn + (c + 1) * HEAD_DIM)
            o_ref[:, sl] = (ys * r * gain).astype(o_ref.dtype)


def _proj(x, norm_gains, ws, head_gains, parts, wf=None, *, tm=512, tn=2048):
    n, d = x.shape
    nout = sum(w.shape[1] for w in ws)
    assert len(parts) * tn == nout and all(w.shape[1] % tn == 0 for w in ws)
    has_f = wf is not None
    n_norms = len(norm_gains)
    once = pl.Buffered(1)
    in_specs = [
        pl.BlockSpec((tm, d), lambda i: (i, 0)),
        pl.BlockSpec((n_norms, d), lambda i: (0, 0)),
        pl.BlockSpec((len(head_gains), HEAD_DIM), lambda i: (0, 0)),
    ] + [pl.BlockSpec(w.shape, lambda i: (0, 0), pipeline_mode=once) for w in ws]
    args = [x, jnp.stack(norm_gains), jnp.stack(head_gains), *ws]
    out_shape = [jax.ShapeDtypeStruct((n, nout), jnp.bfloat16)]
    out_specs = [pl.BlockSpec((tm, nout), lambda i: (i, 0))]
    if has_f:
        in_specs.append(pl.BlockSpec((d, LANES), lambda i: (0, 0), pipeline_mode=once))
        args.append(wf)
        out_shape.append(jax.ShapeDtypeStruct((n, LANES), jnp.float32))
        out_specs.append(pl.BlockSpec((tm, LANES), lambda i: (i, 0)))
    res = pl.pallas_call(
        functools.partial(_proj_kernel, parts=tuple(parts), n_w=len(ws), has_f=has_f, tn=tn),
        grid=(n // tm,),
        in_specs=in_specs,
        out_specs=out_specs,
        out_shape=out_shape,
        scratch_shapes=[pltpu.VMEM((n_norms, tm, d), jnp.bfloat16)],
        compiler_params=_params(("arbitrary",)),
        name="norm_proj",
    )(*args)
    return res if has_f else res[0]


_HEAD_LANES = LANES // N_HEADS
_N_TERMS = 3


def _bias_column_constants():
    sel = np.zeros((_N_TERMS * LANES, 2 * LANES), np.float32)
    one = np.zeros((1, 2 * LANES), np.float32)
    for h in range(N_HEADS):
        for term in range(_N_TERMS):
            sel[term * LANES + h, h * _HEAD_LANES + term] = -1.0
            sel[term * LANES + h, LANES + h * _HEAD_LANES + _N_TERMS + term] = 1.0
            one[0, h * _HEAD_LANES + _N_TERMS + term] = 1.0
            one[0, LANES + h * _HEAD_LANES + term] = 1.0
    return jnp.asarray(sel, jnp.bfloat16), jnp.asarray(one, jnp.float32)


def _logf_cumsum_kernel(f_ref, b_ref, sel_ref, one_ref, qx_ref, kx_ref):
    z = f_ref[0] + b_ref[...]
    c = jnp.minimum(z, 0.0) - jnp.log1p(jnp.exp(-jnp.abs(z)))
    s = c.shape[0]
    row = lax.broadcasted_iota(jnp.int32, c.shape, 0)
    d = 1
    while d < s:
        c = c + jnp.where(row >= d, pltpu.roll(c, d, axis=0), 0.0)
        d *= 2
    c = c * LOG2E
    hi = c.astype(jnp.bfloat16)
    r = c - hi.astype(jnp.float32)
    mid = r.astype(jnp.bfloat16)
    lo = (r - mid.astype(jnp.float32)).astype(jnp.bfloat16)
    terms = jnp.concatenate([hi, mid, lo], axis=1)
    y = jnp.dot(terms, sel_ref[...], preferred_element_type=jnp.float32) + one_ref[...]
    kx_ref[0] = y[:, :LANES].astype(kx_ref.dtype)
    qx_ref[0] = y[:, LANES:].astype(qx_ref.dtype)


def _logf_cumsum(fz, b):
    bsz, s, _ = fz.shape
    sel, one = _bias_column_constants()
    spec = pl.BlockSpec((1, s, LANES), lambda i: (i, 0, 0))
    out = jax.ShapeDtypeStruct((bsz, s, LANES), jnp.bfloat16)
    return pl.pallas_call(
        _logf_cumsum_kernel,
        grid=(bsz,),
        in_specs=[spec, pl.BlockSpec((1, LANES), lambda i: (0, 0)),
                  pl.BlockSpec(sel.shape, lambda i: (0, 0)),
                  pl.BlockSpec(one.shape, lambda i: (0, 0))],
        out_specs=[spec, spec],
        out_shape=[out, out],
        compiler_params=_params(("arbitrary",)),
        name="logf_cumsum",
    )(fz, b, sel, one)


_NT = (((1,), (1,)), ((), ()))


def _fox_kernel(q_ref, qx_ref, k_ref, kx_ref, v_ref, o_ref,
                qa_ref, ka_ref, vt_ref, s0_ref, s1_ref, p0_ref, p1_ref, acc_ref, *, t, tc, rb):
    nt = ka_ref.shape[0]
    nc = t // tc
    nr = t // rb
    f32 = jnp.float32
    s_refs, p_refs = (s0_ref, s1_ref), (p0_ref, p1_ref)
    tiles = [(qi, kv) for qi in range(nt) for kv in range(qi + 1)]

    lane = lax.broadcasted_iota(jnp.int32, (t, LANES), 1)
    own_lanes = lane // _HEAD_LANES == pl.program_id(1)

    for j in range(nt):
        rows = slice(j * t, (j + 1) * t)
        ka_ref[j, :, 0:HEAD_DIM] = k_ref[0, rows, :]
        ka_ref[j, :, HEAD_DIM:] = kx_ref[0, rows, :]
        qa_ref[j, :, 0:HEAD_DIM] = q_ref[0, rows, :]
        qx = qx_ref[0, rows, :]
        qa_ref[j, :, HEAD_DIM:] = jnp.where(own_lanes, qx, jnp.zeros_like(qx))
        vt_ref[:, rows] = v_ref[0, rows, :].T

    def fold8(x, op):
        return op(x.reshape(x.shape[0] // 8, 8, x.shape[1]), axis=0)

    def live(tile, r, c):
        return tile[1] < tile[0] or r * rb < (c + 1) * tc

    def key_rows(tile, c):
        return t if tile[1] < tile[0] else min(t, -(-(c + 1) * tc // rb) * rb)

    def logits_block(tile, r, c, s_out):
        qi, kv = tile
        s = lax.dot_general(ka_ref[kv, r * rb:(r + 1) * rb, :],
                            qa_ref[qi, c * tc:(c + 1) * tc, :], _NT,
                            preferred_element_type=f32)
        if kv == qi and (r + 1) * rb > c * tc:
            key = lax.broadcasted_iota(jnp.int32, s.shape, 0) + r * rb
            qry = lax.broadcasted_iota(jnp.int32, s.shape, 1) + c * tc
            s = jnp.where(key <= qry, s, -jnp.inf)
        s_out[r * rb:(r + 1) * rb, c * tc:(c + 1) * tc] = s
        return fold8(s, jnp.max)

    blocks = [(r, c) for r in range(nr) for c in range(nc)]

    def tile_max(mx):
        return [jnp.max(mx[c], axis=0, keepdims=True) for c in range(nc)]

    mx = [None] * nc
    for r, c in blocks:
        if not live(tiles[0], r, c):
            continue
        pm = logits_block(tiles[0], r, c, s_refs[0])
        mx[c] = pm if mx[c] is None else jnp.maximum(mx[c], pm)
    mc = tile_max(mx)

    m_run = l_run = alpha_prev = None
    for n, (qi, kv) in enumerate(tiles):
        cur, nxt = n % 2, 1 - n % 2
        nxt_tile = tiles[n + 1] if n + 1 < len(tiles) else None
        prv_tile = tiles[n - 1] if n > 0 else None
        if kv == 0:
            m_new = mc
            alpha = None
        else:
            m_new = [jnp.maximum(m_run[c], mc[c]) for c in range(nc)]
            alpha = [jnp.exp2(m_run[c] - m_new[c]) for c in range(nc)]
        mx = [None] * nc
        ls = [None] * nc
        pv = [None] * nc
        for r, c in blocks:
            rows = slice(r * rb, (r + 1) * rb)
            cols = slice(c * tc, (c + 1) * tc)
            if nxt_tile is not None and live(nxt_tile, r, c):
                pm = logits_block(nxt_tile, r, c, s_refs[nxt])
                mx[c] = pm if mx[c] is None else jnp.maximum(mx[c], pm)
            if live((qi, kv), r, c):
                p = jnp.exp2(s_refs[cur][rows, cols] - m_new[c])
                ps = fold8(p, jnp.sum)
                ls[c] = ps if ls[c] is None else ls[c] + ps
                p_refs[cur][rows, cols] = p.astype(p_refs[cur].dtype)
            if prv_tile is not None and r == nr - 1:
                nk = key_rows(prv_tile, c)
                k0 = prv_tile[1] * t
                pv[c] = jnp.dot(vt_ref[:, k0:k0 + nk], p_refs[nxt][0:nk, cols],
                                preferred_element_type=f32)
        lsum = [jnp.sum(ls[c], axis=0, keepdims=True) for c in range(nc)]
        l_prev_block = l_run
        l_run = lsum if kv == 0 else [alpha[c] * l_run[c] + lsum[c] for c in range(nc)]
        if prv_tile is not None:
            for c in range(nc):
                cols = slice(c * tc, (c + 1) * tc)
                if prv_tile[1] == 0:
                    acc_ref[:, cols] = pv[c]
                else:
                    acc_ref[:, cols] = alpha_prev[c] * acc_ref[:, cols] + pv[c]
            if prv_tile[1] == prv_tile[0]:
                l_fin = jnp.concatenate(l_prev_block, axis=1)
                o_ref[0, prv_tile[0] * t:(prv_tile[0] + 1) * t, :] = (
                    acc_ref[...] / l_fin).T.astype(o_ref.dtype)
        m_run, alpha_prev = m_new, alpha
        if nxt_tile is not None:
            mc = tile_max(mx)

    last = len(tiles) - 1
    qi, kv = tiles[last]
    for c in range(nc):
        cols = slice(c * tc, (c + 1) * tc)
        nk = key_rows(tiles[last], c)
        d = jnp.dot(vt_ref[:, kv * t:kv * t + nk], p_refs[last % 2][0:nk, cols],
                    preferred_element_type=f32)
        acc_ref[:, cols] = d if kv == 0 else alpha_prev[c] * acc_ref[:, cols] + d
    o_ref[0, qi * t:(qi + 1) * t, :] = (
        acc_ref[...] / jnp.concatenate(l_run, axis=1)).T.astype(o_ref.dtype)


def _fox_attention(qkv, qx, kx, *, t=512, tc=256, rb=256):
    bsz, s, d3 = qkv.shape
    d = d3 // 3
    nh = d // HEAD_DIM
    nt = s // t
    f32 = jnp.float32
    bf = jnp.bfloat16
    return pl.pallas_call(
        functools.partial(_fox_kernel, t=t, tc=tc, rb=rb),
        grid=(bsz, nh),
        in_specs=[
            pl.BlockSpec((1, s, HEAD_DIM), lambda b, h: (b, 0, h)),
            pl.BlockSpec((1, s, LANES), lambda b, h: (b, 0, 0)),
            pl.BlockSpec((1, s, HEAD_DIM), lambda b, h: (b, 0, nh + h)),
            pl.BlockSpec((1, s, LANES), lambda b, h: (b, 0, 0)),
            pl.BlockSpec((1, s, HEAD_DIM), lambda b, h: (b, 0, 2 * nh + h)),
        ],
        out_specs=pl.BlockSpec((1, s, HEAD_DIM), lambda b, h: (b, 0, h)),
        out_shape=jax.ShapeDtypeStruct((bsz, s, d), bf),
        scratch_shapes=[pltpu.VMEM((nt, t, HEAD_DIM + LANES), bf),
                        pltpu.VMEM((nt, t, HEAD_DIM + LANES), bf),
                        pltpu.VMEM((HEAD_DIM, s), bf),
                        pltpu.VMEM((t, t), f32),
                        pltpu.VMEM((t, t), f32),
                        pltpu.VMEM((t, t), bf),
                        pltpu.VMEM((t, t), bf),
                        pltpu.VMEM((HEAD_DIM, t), f32)],
        compiler_params=_params(("arbitrary", "arbitrary")),
        name="fox_attention",
    )(qkv, qx, qkv, kx, qkv)


def _fox_from_fz(qkv, fz, b_f):
    qx, kx = _logf_cumsum(fz, b_f)
    return _fox_attention(qkv, qx, kx)


def _out_proj_kernel(h_ref, o_ref, w_ref, out_ref):
    out_ref[...] = h_ref[...] + jnp.dot(o_ref[...], w_ref[...],
                                        preferred_element_type=jnp.float32)


def _out_proj(h, o, w, *, tm=512):
    n, d = h.shape
    return pl.pallas_call(
        _out_proj_kernel,
        grid=(n // tm,),
        in_specs=[pl.BlockSpec((tm, d), lambda i: (i, 0)),
                  pl.BlockSpec((tm, d), lambda i: (i, 0)),
                  pl.BlockSpec((d, d), lambda i: (0, 0))],
        out_specs=pl.BlockSpec((tm, d), lambda i: (i, 0)),
        out_shape=jax.ShapeDtypeStruct((n, d), jnp.float32),
        compiler_params=_params(("arbitrary",)),
        name="out_proj",
    )(h, o, w)


def _mlp_kernel(h_ref, g_ref, w1_ref, w2_ref, out_ref, u_ref):
    f = pl.program_id(1)

    @pl.when(f == 0)
    def _():
        x = h_ref[...]
        r = lax.rsqrt(jnp.mean(x * x, axis=-1, keepdims=True) + EPS)
        u_ref[...] = (x * r * g_ref[...]).astype(u_ref.dtype)
        out_ref[...] = x

    a = jnp.dot(u_ref[...], w1_ref[...], preferred_element_type=jnp.float32)
    a = jnp.maximum(a, 0.0)
    a = (a * a).astype(w2_ref.dtype)
    out_ref[...] += jnp.dot(a, w2_ref[...], preferred_element_type=jnp.float32)


def _mlp(h, g, w1, w2, layer, *, tm=512, tf=2048):
    n, d = h.shape
    dff = w1.shape[2]
    return pl.pallas_call(
        _mlp_kernel,
        grid=(n // tm, dff // tf),
        in_specs=[pl.BlockSpec((tm, d), lambda i, f: (i, 0)),
                  pl.BlockSpec((1, d), lambda i, f: (0, 0)),
                  pl.BlockSpec((None, d, tf), lambda i, f: (layer, 0, f)),
                  pl.BlockSpec((None, tf, d), lambda i, f: (layer, f, 0))],
        out_specs=pl.BlockSpec((tm, d), lambda i, f: (i, 0)),
        out_shape=jax.ShapeDtypeStruct((n, d), jnp.float32),
        scratch_shapes=[pltpu.VMEM((tm, d), jnp.bfloat16)],
        compiler_params=_params(("arbitrary", "arbitrary")),
        name="sq_relu_mlp",
    )(h, g.reshape(1, d), w1, w2)


_KB = _GROUP
_NBLK = _WIN // _KB


def _band_kernel(q_ref, k_ref, v_ref, bias_ref, o_ref, vt_ref, s0_ref, s1_ref, p0_ref, p1_ref):
    ng = vt_ref.shape[1] // _GROUP
    f32 = jnp.float32
    s_refs, p_refs = (s0_ref, s1_ref), (p0_ref, p1_ref)

    for j in range(ng):
        vt_ref[:, j * _KB:(j + 1) * _KB] = v_ref[0, j * _KB:(j + 1) * _KB, :].T

    def fold8(x, op):
        return op(x.reshape(x.shape[0] // 8, 8, x.shape[1]), axis=0)

    def rows_of(g):
        return slice(g * _GROUP, (g + 1) * _GROUP)

    def key_blocks(g):
        return [(blk, g - (_NBLK - 1) + blk) for blk in range(_NBLK) if g - (_NBLK - 1) + blk >= 0]

    def logits_block(g, blk, kb, s_out):
        s = lax.dot_general(k_ref[0, rows_of(kb), :], q_ref[0, rows_of(g), :], _NT,
                            preferred_element_type=f32) + bias_ref[0, blk * _KB:(blk + 1) * _KB, :]
        s_out[blk * _KB:(blk + 1) * _KB, :] = s
        return fold8(s, jnp.max)

    def values(g, p_ref):
        blks = key_blocks(g)
        (b0, k0), nb = blks[0], len(blks)
        return jnp.dot(vt_ref[:, k0 * _KB:(k0 + nb) * _KB], p_ref[b0 * _KB:(b0 + nb) * _KB, :],
                       preferred_element_type=f32)

    def write_out(g, pv, l):
        o_ref[0, rows_of(g), :] = (pv / l).T.astype(o_ref.dtype)

    mx = None
    for blk, kb in key_blocks(0):
        pm = logits_block(0, blk, kb, s_refs[0])
        mx = pm if mx is None else jnp.maximum(mx, pm)
    m = jnp.max(mx, axis=0, keepdims=True)

    l_prev = None
    for g in range(ng):
        cur, nxt = g % 2, 1 - g % 2
        nxt_blocks = key_blocks(g + 1) if g + 1 < ng else []
        cur_blocks = key_blocks(g)
        mx = ls = None
        for i in range(_NBLK):
            if i < len(nxt_blocks):
                blk, kb = nxt_blocks[i]
                pm = logits_block(g + 1, blk, kb, s_refs[nxt])
                mx = pm if mx is None else jnp.maximum(mx, pm)
            if i < len(cur_blocks):
                blk, _ = cur_blocks[i]
                p = jnp.exp2(s_refs[cur][blk * _KB:(blk + 1) * _KB, :] - m)
                ps = fold8(p, jnp.sum)
                ls = ps if ls is None else ls + ps
                p_refs[cur][blk * _KB:(blk + 1) * _KB, :] = p.astype(p_refs[cur].dtype)
        if g > 0:
            write_out(g - 1, values(g - 1, p_refs[nxt]), l_prev)
        l_prev = jnp.sum(ls, axis=0, keepdims=True)
        if nxt_blocks:
            m = jnp.max(mx, axis=0, keepdims=True)

    write_out(ng - 1, values(ng - 1, p_refs[(ng - 1) % 2]), l_prev)


def _band_attention(kvq, bias_t):
    bsz, s, d3 = kvq.shape
    d = d3 // 3
    nh = d // HEAD_DIM
    ng = s // _GROUP
    f32 = jnp.float32
    bf = jnp.bfloat16
    return pl.pallas_call(
        _band_kernel,
        grid=(bsz, nh),
        in_specs=[
            pl.BlockSpec((1, s, HEAD_DIM), lambda b, h: (b, 0, 2 * nh + h)),
            pl.BlockSpec((1, s, HEAD_DIM), lambda b, h: (b, 0, h)),
            pl.BlockSpec((1, s, HEAD_DIM), lambda b, h: (b, 0, nh + h)),
            pl.BlockSpec((1, _WIN, _GROUP), lambda b, h: (h, 0, 0)),
        ],
        out_specs=pl.BlockSpec((1, s, HEAD_DIM), lambda b, h: (b, 0, h)),
        out_shape=jax.ShapeDtypeStruct((bsz, s, d), bf),
        scratch_shapes=[pltpu.VMEM((HEAD_DIM, s), bf),
                        pltpu.VMEM((_WIN, _GROUP), f32),
                        pltpu.VMEM((_WIN, _GROUP), f32),
                        pltpu.VMEM((_WIN, _GROUP), bf),
                        pltpu.VMEM((_WIN, _GROUP), bf)],
        compiler_params=_params(("arbitrary", "arbitrary")),
        name="band_attention",
    )(kvq, kvq, kvq, bias_t)


_BIAS_SPAN = _WIN + _GROUP


def _band_bias_kernel(g_ref, o_ref):
    row = lax.broadcasted_iota(jnp.int32, (_KB, 2 * _GROUP), 0)
    for blk in range(_NBLK):
        start = (_NBLK - 1 - blk) * _KB
        x = jnp.broadcast_to(g_ref[0, :, start:start + 2 * _GROUP], (_KB, 2 * _GROUP))
        shift = 1
        while shift < _KB:
            x = jnp.where((row & shift) != 0, pltpu.roll(x, shift, axis=1), x)
            shift *= 2
        b = x[:, _GROUP:]
        j = lax.broadcasted_iota(jnp.int32, (_KB, _GROUP), 0) + blk * _KB
        i = lax.broadcasted_iota(jnp.int32, (_KB, _GROUP), 1)
        km = j - (i // CHUNK) * CHUNK
        o_ref[0, blk * _KB:(blk + 1) * _KB, :] = jnp.where((km >= 0) & (km < _BAND), b, -jnp.inf)


def _band_bias(rel_table):
    nh = rel_table.shape[0]
    k = jnp.arange(_BIAS_SPAN)
    idx = jnp.clip(k - _WIN + _PAD, -(CHUNK - 1), REL_CLIP) + (CHUNK - 1)
    g = (rel_table[:, idx].astype(jnp.float32) * LOG2E).reshape(nh, 1, _BIAS_SPAN)
    return pl.pallas_call(
        _band_bias_kernel,
        grid=(nh,),
        in_specs=[pl.BlockSpec((1, 1, _BIAS_SPAN), lambda h: (h, 0, 0))],
        out_specs=pl.BlockSpec((1, _WIN, _GROUP), lambda h: (h, 0, 0)),
        out_shape=jax.ShapeDtypeStruct((nh, _WIN, _GROUP), jnp.float32),
        compiler_params=_params(("arbitrary",)),
        name="band_bias",
    )(g)


def kernel(x, a_norm_g, a_w_in, a_b_f, a_q_g, a_k_g, a_w_out, mlp_norm_g, mlp_w1, mlp_w2,
           kv_norm_g, kv_w, kv_k_g, b_norm_g, b_w_q, b_q_g, b_rel, b_w_out):
    bsz, s, d = x.shape
    n = bsz * s
    nh = N_HEADS
    bf = jnp.bfloat16
    q_scale = HEAD_DIM ** -0.5 * LOG2E

    h = x.reshape(n, d)

    w_qkv, w_f = _cast_split_cols(a_w_in[0], 3 * d)
    w1_all = mlp_w1.astype(bf)
    w2_all = mlp_w2.astype(bf)
    qkv, fz = _proj(h, [a_norm_g[0]], [w_qkv], [a_q_g[0] * q_scale, a_k_g[0]],
                    [(0, 0), (0, 1), (0, None)], w_f)
    b_f = jnp.pad(a_b_f[0], (0, LANES - nh)).reshape(1, LANES)
    o = _fox_from_fz(qkv.reshape(bsz, s, 3 * d), fz.reshape(bsz, s, LANES), b_f)
    h = _out_proj(h, o.reshape(n, d), a_w_out[0].astype(bf))
    h = _mlp(h, mlp_norm_g[0], w1_all, w2_all, 0)

    kvq = _proj(h, [kv_norm_g, b_norm_g[0]], [kv_w.astype(bf), b_w_q[0].astype(bf)],
                [kv_k_g, b_q_g[0] * q_scale],
                [(0, 0), (0, None), (1, 1)])

    o = _band_attention(kvq.reshape(bsz, s, 3 * d), _band_bias(b_rel[0]))
    h = _out_proj(h, o.reshape(n, d), b_w_out[0].astype(bf))
    h = _mlp(h, mlp_norm_g[1], w1_all, w2_all, 1)
    return h.reshape(bsz, s, d)
```

```python
import functools
import math

import numpy as np

import jax
import jax.numpy as jnp
from jax import lax
from jax.experimental import pallas as pl
from jax.experimental.pallas import tpu as pltpu

N_HEADS = 16
HEAD_DIM = 128
CHUNK = 64
N_PREV_CHUNKS = 8
REL_CLIP = 256
EPS = 1e-6
LOG2E = math.log2(math.e)

LANES = 128
VMEM_LIMIT = 60 * 1024 * 1024

_BAND = (N_PREV_CHUNKS + 1) * CHUNK
_PAD = N_PREV_CHUNKS * CHUNK
_GROUP = 4 * CHUNK
_WIN = _PAD + _GROUP


def _params(sem):
    return pltpu.CompilerParams(dimension_semantics=sem, vmem_limit_bytes=VMEM_LIMIT)


def _cast_kernel(w_ref, tail_ref, o_ref, otail_ref, *, n_tail):
    o_ref[...] = w_ref[...].astype(o_ref.dtype)
    lane = lax.broadcasted_iota(jnp.int32, tail_ref.shape, 1)
    otail_ref[...] = jnp.where(lane < n_tail, tail_ref[...], 0.0).astype(otail_ref.dtype)


def _cast_split_cols(w, layer, ncols, *, tr=256):
    _, rows, total = w.shape
    n_tail = total - ncols
    assert rows % tr == 0 and ncols % LANES == 0 and 0 < n_tail < LANES
    return pl.pallas_call(
        functools.partial(_cast_kernel, n_tail=n_tail),
        grid=(rows // tr,),
        in_specs=[pl.BlockSpec((None, tr, ncols), lambda i: (layer, i, 0)),
                  pl.BlockSpec((None, tr, LANES), lambda i: (layer, i, ncols // LANES))],
        out_specs=[pl.BlockSpec((tr, ncols), lambda i: (i, 0)),
                   pl.BlockSpec((tr, LANES), lambda i: (i, 0))],
        out_shape=[jax.ShapeDtypeStruct((rows, ncols), jnp.bfloat16),
                   jax.ShapeDtypeStruct((rows, LANES), jnp.bfloat16)],
        compiler_params=_params(("arbitrary",)),
        name="cast_cols",
    )(w, w)


def _proj_kernel(*refs, parts, n_w, has_f, tn):
    x_ref, g_ref, gain_ref = refs[:3]
    w_refs = refs[3:3 + n_w]
    if has_f:
        wf_ref, o_ref, f_ref, u_ref = refs[3 + n_w:]
    else:
        o_ref, u_ref = refs[3 + n_w:]
    w_parts = [(w_ref, c) for w_ref in w_refs for c in range(w_ref.shape[1] // tn)]
    n_norms = u_ref.shape[0]
    x = x_ref[...]
    xn = x * lax.rsqrt(jnp.mean(x * x, axis=-1, keepdims=True) + EPS)
    for k in range(n_norms):
        u_ref[k] = (xn * g_ref[k:k + 1, :]).astype(u_ref.dtype)
    if has_f:
        f_ref[...] = jnp.dot(u_ref[0], wf_ref[...], preferred_element_type=jnp.float32)
    for j, (norm_idx, gain_idx) in enumerate(parts):
        w_ref, c = w_parts[j]
        y = jnp.dot(u_ref[norm_idx], w_ref[:, c * tn:(c + 1) * tn],
                    preferred_element_type=jnp.float32)
        if gain_idx is None:
            o_ref[:, j * tn:(j + 1) * tn] = y.astype(o_ref.dtype)
            continue
        gain = gain_ref[gain_idx:gain_idx + 1, :]
        for c in range(tn // HEAD_DIM):
            ys = y[:, c * HEAD_DIM:(c + 1) * HEAD_DIM]
            r = lax.rsqrt(jnp.mean(ys * ys, axis=-1, keepdims=True) + EPS)
            sl = slice(j * tn + c * HEAD_DIM, j * tn + (c + 1) * HEAD_DIM)
            o_ref[:, sl] = (ys * r * gain).astype(o_ref.dtype)


def _proj(x, norm_gains, ws, head_gains, parts, wf=None, *, tm=512, tn=2048):
    n, d = x.shape
    nout = sum(w.shape[1] for w in ws)
    assert len(parts) * tn == nout and all(w.shape[1] % tn == 0 for w in ws)
    has_f = wf is not None
    n_norms = len(norm_gains)
    once = pl.Buffered(1)
    in_specs = [
        pl.BlockSpec((tm, d), lambda i: (i, 0)),
        pl.BlockSpec((n_norms, d), lambda i: (0, 0)),
        pl.BlockSpec((len(head_gains), HEAD_DIM), lambda i: (0, 0)),
    ] + [pl.BlockSpec(w.shape, lambda i: (0, 0), pipeline_mode=once) for w in ws]
    args = [x, jnp.stack(norm_gains), jnp.stack(head_gains), *ws]
    out_shape = [jax.ShapeDtypeStruct((n, nout), jnp.bfloat16)]
    out_specs = [pl.BlockSpec((tm, nout), lambda i: (i, 0))]
    if has_f:
        in_specs.append(pl.BlockSpec((d, LANES), lambda i: (0, 0), pipeline_mode=once))
        args.append(wf)
        out_shape.append(jax.ShapeDtypeStruct((n, LANES), jnp.float32))
        out_specs.append(pl.BlockSpec((tm, LANES), lambda i: (i, 0)))
    res = pl.pallas_call(
        functools.partial(_proj_kernel, parts=tuple(parts), n_w=len(ws), has_f=has_f, tn=tn),
        grid=(n // tm,),
        in_specs=in_specs,
        out_specs=out_specs,
        out_shape=out_shape,
        scratch_shapes=[pltpu.VMEM((n_norms, tm, d), jnp.bfloat16)],
        compiler_params=_params(("arbitrary",)),
        name="norm_proj",
    )(*args)
    return res if has_f else res[0]


_HEAD_LANES = LANES // N_HEADS
_N_TERMS = 3


def _bias_column_constants():
    sel = np.zeros((_N_TERMS * LANES, 2 * LANES), np.float32)
    one = np.zeros((1, 2 * LANES), np.float32)
    for h in range(N_HEADS):
        for term in range(_N_TERMS):
            sel[term * LANES + h, h * _HEAD_LANES + term] = -1.0
            sel[term * LANES + h, LANES + h * _HEAD_LANES + _N_TERMS + term] = 1.0
            one[0, h * _HEAD_LANES + _N_TERMS + term] = 1.0
            one[0, LANES + h * _HEAD_LANES + term] = 1.0
    return jnp.asarray(sel, jnp.bfloat16), jnp.asarray(one, jnp.float32)


def _logf_cumsum_kernel(f_ref, b_ref, sel_ref, one_ref, qx_ref, kx_ref):
    z = f_ref[0] + b_ref[...]
    c = jnp.minimum(z, 0.0) - jnp.log1p(jnp.exp(-jnp.abs(z)))
    s = c.shape[0]
    row = lax.broadcasted_iota(jnp.int32, c.shape, 0)
    d = 1
    while d < s:
        c = c + jnp.where(row >= d, pltpu.roll(c, d, axis=0), 0.0)
        d *= 2
    c = c * LOG2E
    hi = c.astype(jnp.bfloat16)
    r = c - hi.astype(jnp.float32)
    mid = r.astype(jnp.bfloat16)
    lo = (r - mid.astype(jnp.float32)).astype(jnp.bfloat16)
    terms = jnp.concatenate([hi, mid, lo], axis=1)
    y = jnp.dot(terms, sel_ref[...], preferred_element_type=jnp.float32) + one_ref[...]
    kx_ref[0] = y[:, :LANES].astype(kx_ref.dtype)
    qx_ref[0] = y[:, LANES:].astype(qx_ref.dtype)


def _logf_cumsum(fz, b):
    bsz, s, _ = fz.shape
    sel, one = _bias_column_constants()
    spec = pl.BlockSpec((1, s, LANES), lambda i: (i, 0, 0))
    out = jax.ShapeDtypeStruct((bsz, s, LANES), jnp.bfloat16)
    return pl.pallas_call(
        _logf_cumsum_kernel,
        grid=(bsz,),
        in_specs=[spec, pl.BlockSpec((1, LANES), lambda i: (0, 0)),
                  pl.BlockSpec(sel.shape, lambda i: (0, 0)),
                  pl.BlockSpec(one.shape, lambda i: (0, 0))],
        out_specs=[spec, spec],
        out_shape=[out, out],
        compiler_params=_params(("arbitrary",)),
        name="logf_cumsum",
    )(fz, b, sel, one)


_NT = (((1,), (1,)), ((), ()))


def _fox_kernel(q_ref, qx_ref, k_ref, kx_ref, v_ref, o_ref,
                qa_ref, ka_ref, vt_ref, s0_ref, s1_ref, p0_ref, p1_ref, acc_ref, *, t, tc, rb):
    nt = ka_ref.shape[0]
    nc = t // tc
    nr = t // rb
    f32 = jnp.float32
    s_refs, p_refs = (s0_ref, s1_ref), (p0_ref, p1_ref)
    tiles = [(qi, kv) for qi in range(nt) for kv in range(qi + 1)]

    lane = lax.broadcasted_iota(jnp.int32, (t, LANES), 1)
    own_lanes = lane // _HEAD_LANES == pl.program_id(1)

    for j in range(nt):
        rows = slice(j * t, (j + 1) * t)
        ka_ref[j, :, 0:HEAD_DIM] = k_ref[0, rows, :]
        ka_ref[j, :, HEAD_DIM:] = kx_ref[0, rows, :]
        qa_ref[j, :, 0:HEAD_DIM] = q_ref[0, rows, :]
        qx = qx_ref[0, rows, :]
        qa_ref[j, :, HEAD_DIM:] = jnp.where(own_lanes, qx, jnp.zeros_like(qx))
        vt_ref[:, rows] = v_ref[0, rows, :].T

    def fold8(x, op):
        return op(x.reshape(x.shape[0] // 8, 8, x.shape[1]), axis=0)

    def live(tile, r, c):
        return tile[1] < tile[0] or r * rb < (c + 1) * tc

    def key_rows(tile, c):
        return t if tile[1] < tile[0] else min(t, -(-(c + 1) * tc // rb) * rb)

    def logits_block(tile, r, c, s_out):
        qi, kv = tile
        s = lax.dot_general(ka_ref[kv, r * rb:(r + 1) * rb, :],
                            qa_ref[qi, c * tc:(c + 1) * tc, :], _NT,
                            preferred_element_type=f32)
        if kv == qi and (r + 1) * rb > c * tc:
            key = lax.broadcasted_iota(jnp.int32, s.shape, 0) + r * rb
            qry = lax.broadcasted_iota(jnp.int32, s.shape, 1) + c * tc
            s = jnp.where(key <= qry, s, -jnp.inf)
        s_out[r * rb:(r + 1) * rb, c * tc:(c + 1) * tc] = s
        return fold8(s, jnp.max)

    blocks = [(r, c) for r in range(nr) for c in range(nc)]

    def tile_max(mx):
        return [jnp.max(mx[c], axis=0, keepdims=True) for c in range(nc)]

    mx = [None] * nc
    for r, c in blocks:
        if not live(tiles[0], r, c):
            continue
        pm = logits_block(tiles[0], r, c, s_refs[0])
        mx[c] = pm if mx[c] is None else jnp.maximum(mx[c], pm)
    mc = tile_max(mx)

    m_run = l_run = alpha_prev = None
    for n, (qi, kv) in enumerate(tiles):
        cur, nxt = n % 2, 1 - n % 2
        nxt_tile = tiles[n + 1] if n + 1 < len(tiles) else None
        prv_tile = tiles[n - 1] if n > 0 else None
        if kv == 0:
            m_new = mc
            alpha = None
        else:
            m_new = [jnp.maximum(m_run[c], mc[c]) for c in range(nc)]
            alpha = [jnp.exp2(m_run[c] - m_new[c]) for c in range(nc)]
        mx = [None] * nc
        ls = [None] * nc
        pv = [None] * nc
        for r, c in blocks:
            rows = slice(r * rb, (r + 1) * rb)
            cols = slice(c * tc, (c + 1) * tc)
            if nxt_tile is not None and live(nxt_tile, r, c):
                pm = logits_block(nxt_tile, r, c, s_refs[nxt])
                mx[c] = pm if mx[c] is None else jnp.maximum(mx[c], pm)
            if live((qi, kv), r, c):
                p = jnp.exp2(s_refs[cur][rows, cols] - m_new[c])
                ps = fold8(p, jnp.sum)
                ls[c] = ps if ls[c] is None else ls[c] + ps
                p_refs[cur][rows, cols] = p.astype(p_refs[cur].dtype)
            if prv_tile is not None and r == nr - 1:
                nk = key_rows(prv_tile, c)
                k0 = prv_tile[1] * t
                pv[c] = jnp.dot(vt_ref[:, k0:k0 + nk], p_refs[nxt][0:nk, cols],
                                preferred_element_type=f32)
        lsum = [jnp.sum(ls[c], axis=0, keepdims=True) for c in range(nc)]
        l_prev_block = l_run
        l_run = lsum if kv == 0 else [alpha[c] * l_run[c] + lsum[c] for c in range(nc)]
        if prv_tile is not None:
            for c in range(nc):
                cols = slice(c * tc, (c + 1) * tc)
                if prv_tile[1] == 0:
                    acc_ref[:, cols] = pv[c]
                else:
                    acc_ref[:, cols] = alpha_prev[c] * acc_ref[:, cols] + pv[c]
            if prv_tile[1] == prv_tile[0]:
                l_fin = jnp.concatenate(l_prev_block, axis=1)
                o_ref[0, prv_tile[0] * t:(prv_tile[0] + 1) * t, :] = (
                    acc_ref[...] / l_fin).T.astype(o_ref.dtype)
        m_run, alpha_prev = m_new, alpha
        if nxt_tile is not None:
            mc = tile_max(mx)

    last = len(tiles) - 1
    qi, kv = tiles[last]
    for c in range(nc):
        cols = slice(c * tc, (c + 1) * tc)
        nk = key_rows(tiles[last], c)
        d = jnp.dot(vt_ref[:, kv * t:kv * t + nk], p_refs[last % 2][0:nk, cols],
                    preferred_element_type=f32)
        acc_ref[:, cols] = d if kv == 0 else alpha_prev[c] * acc_ref[:, cols] + d
    o_ref[0, qi * t:(qi + 1) * t, :] = (
        acc_ref[...] / jnp.concatenate(l_run, axis=1)).T.astype(o_ref.dtype)


def _fox_attention(qkv, qx, kx, *, t=512, tc=256, rb=256):
    bsz, s, d3 = qkv.shape
    d = d3 // 3
    nh = d // HEAD_DIM
    nt = s // t
    f32 = jnp.float32
    bf = jnp.bfloat16
    return pl.pallas_call(
        functools.partial(_fox_kernel, t=t, tc=tc, rb=rb),
        grid=(bsz, nh),
        in_specs=[
            pl.BlockSpec((1, s, HEAD_DIM), lambda b, h: (b, 0, h)),
            pl.BlockSpec((1, s, LANES), lambda b, h: (b, 0, 0)),
            pl.BlockSpec((1, s, HEAD_DIM), lambda b, h: (b, 0, nh + h)),
            pl.BlockSpec((1, s, LANES), lambda b, h: (b, 0, 0)),
            pl.BlockSpec((1, s, HEAD_DIM), lambda b, h: (b, 0, 2 * nh + h)),
        ],
        out_specs=pl.BlockSpec((1, s, HEAD_DIM), lambda b, h: (b, 0, h)),
        out_shape=jax.ShapeDtypeStruct((bsz, s, d), bf),
        scratch_shapes=[pltpu.VMEM((nt, t, HEAD_DIM + LANES), bf),
                        pltpu.VMEM((nt, t, HEAD_DIM + LANES), bf),
                        pltpu.VMEM((HEAD_DIM, s), bf),
                        pltpu.VMEM((t, t), f32),
                        pltpu.VMEM((t, t), f32),
                        pltpu.VMEM((t, t), bf),
                        pltpu.VMEM((t, t), bf),
                        pltpu.VMEM((HEAD_DIM, t), f32)],
        compiler_params=_params(("arbitrary", "arbitrary")),
        name="fox_attention",
    )(qkv, qx, qkv, kx, qkv)


def _fox_from_fz(qkv, fz, b_f):
    qx, kx = _logf_cumsum(fz, b_f)
    return _fox_attention(qkv, qx, kx)


def _out_proj_kernel(h_ref, o_ref, w_ref, out_ref):
    out_ref[...] = h_ref[...] + jnp.dot(o_ref[...], w_ref[...],
                                        preferred_element_type=jnp.float32)


def _out_proj(h, o, w, *, tm=512):
    n, d = h.shape
    return pl.pallas_call(
        _out_proj_kernel,
        grid=(n // tm,),
        in_specs=[pl.BlockSpec((tm, d), lambda i: (i, 0)),
                  pl.BlockSpec((tm, d), lambda i: (i, 0)),
                  pl.BlockSpec((d, d), lambda i: (0, 0))],
        out_specs=pl.BlockSpec((tm, d), lambda i: (i, 0)),
        out_shape=jax.ShapeDtypeStruct((n, d), jnp.float32),
        compiler_params=_params(("arbitrary",)),
        name="out_proj",
    )(h, o, w)


def _mlp_kernel(h_ref, g_ref, w1_ref, w2_ref, out_ref, u_ref):
    f = pl.program_id(1)

    @pl.when(f == 0)
    def _():
        x = h_ref[...]
        r = lax.rsqrt(jnp.mean(x * x, axis=-1, keepdims=True) + EPS)
        u_ref[...] = (x * r * g_ref[...]).astype(u_ref.dtype)
        out_ref[...] = x

    a = jnp.dot(u_ref[...], w1_ref[...], preferred_element_type=jnp.float32)
    a = jnp.maximum(a, 0.0)
    a = (a * a).astype(w2_ref.dtype)
    out_ref[...] += jnp.dot(a, w2_ref[...], preferred_element_type=jnp.float32)


def _mlp(h, g, w1, w2, layer, *, tm=512, tf=2048):
    n, d = h.shape
    dff = w1.shape[2]
    return pl.pallas_call(
        _mlp_kernel,
        grid=(n // tm, dff // tf),
        in_specs=[pl.BlockSpec((tm, d), lambda i, f: (i, 0)),
                  pl.BlockSpec((1, d), lambda i, f: (0, 0)),
                  pl.BlockSpec((None, d, tf), lambda i, f: (layer, 0, f)),
                  pl.BlockSpec((None, tf, d), lambda i, f: (layer, f, 0))],
        out_specs=pl.BlockSpec((tm, d), lambda i, f: (i, 0)),
        out_shape=jax.ShapeDtypeStruct((n, d), jnp.float32),
        scratch_shapes=[pltpu.VMEM((tm, d), jnp.bfloat16)],
        compiler_params=_params(("arbitrary", "arbitrary")),
        name="sq_relu_mlp",
    )(h, g.reshape(1, d), w1, w2)


_KB = _GROUP
_NBLK = _WIN // _KB
_ONES_ROWS = 16


def _band_kernel(q_ref, k_ref, v_ref, bias_ref, o_ref, vt_ref, s0_ref, s1_ref, p0_ref, p1_ref):
    ng = vt_ref.shape[1] // _GROUP
    f32 = jnp.float32
    s_refs, p_refs = (s0_ref, s1_ref), (p0_ref, p1_ref)

    for j in range(ng):
        vt_ref[0:HEAD_DIM, j * _KB:(j + 1) * _KB] = v_ref[0, j * _KB:(j + 1) * _KB, :].T
    sub = lax.broadcasted_iota(jnp.int32, (_ONES_ROWS, vt_ref.shape[1]), 0)
    vt_ref[HEAD_DIM:, :] = jnp.where(sub == 0, 1.0, 0.0).astype(vt_ref.dtype)

    def fold8(x, op):
        return op(x.reshape(x.shape[0] // 8, 8, x.shape[1]), axis=0)

    def rows_of(g):
        return slice(g * _GROUP, (g + 1) * _GROUP)

    def key_blocks(g):
        return [(blk, g - (_NBLK - 1) + blk) for blk in range(_NBLK) if g - (_NBLK - 1) + blk >= 0]

    def logits_block(g, blk, kb, s_out):
        s = lax.dot_general(k_ref[0, rows_of(kb), :], q_ref[0, rows_of(g), :], _NT,
                            preferred_element_type=f32) + bias_ref[0, blk * _KB:(blk + 1) * _KB, :]
        s_out[blk * _KB:(blk + 1) * _KB, :] = s
        return fold8(s, jnp.max)

    def values(g, p_ref):
        blks = key_blocks(g)
        (b0, k0), nb = blks[0], len(blks)
        return jnp.dot(vt_ref[:, k0 * _KB:(k0 + nb) * _KB], p_ref[b0 * _KB:(b0 + nb) * _KB, :],
                       preferred_element_type=f32)

    def write_out(g, pv):
        o = pv[0:HEAD_DIM] / pv[HEAD_DIM:HEAD_DIM + 1]
        o_ref[0, rows_of(g), :] = o.T.astype(o_ref.dtype)

    mx = None
    for blk, kb in key_blocks(0):
        pm = logits_block(0, blk, kb, s_refs[0])
        mx = pm if mx is None else jnp.maximum(mx, pm)
    m = jnp.max(mx, axis=0, keepdims=True)

    for g in range(ng):
        cur, nxt = g % 2, 1 - g % 2
        nxt_blocks = key_blocks(g + 1) if g + 1 < ng else []
        cur_blocks = key_blocks(g)
        mx = None
        for i in range(_NBLK):
            if i < len(nxt_blocks):
                blk, kb = nxt_blocks[i]
                pm = logits_block(g + 1, blk, kb, s_refs[nxt])
                mx = pm if mx is None else jnp.maximum(mx, pm)
            if i < len(cur_blocks):
                blk, _ = cur_blocks[i]
                e = (s_refs[cur][blk * _KB:(blk + 1) * _KB, :] - m).astype(p_refs[cur].dtype)
                p_refs[cur][blk * _KB:(blk + 1) * _KB, :] = jnp.exp2(e)
        if g > 0:
            write_out(g - 1, values(g - 1, p_refs[nxt]))
        if nxt_blocks:
            m = jnp.max(mx, axis=0, keepdims=True)

    write_out(ng - 1, values(ng - 1, p_refs[(ng - 1) % 2]))


def _band_attention(kvq, bias_t):
    bsz, s, d3 = kvq.shape
    d = d3 // 3
    nh = d // HEAD_DIM
    assert s % _GROUP == 0
    f32 = jnp.float32
    bf = jnp.bfloat16
    return pl.pallas_call(
        _band_kernel,
        grid=(bsz, nh),
        in_specs=[
            pl.BlockSpec((1, s, HEAD_DIM), lambda b, h: (b, 0, 2 * nh + h)),
            pl.BlockSpec((1, s, HEAD_DIM), lambda b, h: (b, 0, h)),
            pl.BlockSpec((1, s, HEAD_DIM), lambda b, h: (b, 0, nh + h)),
            pl.BlockSpec((1, _WIN, _GROUP), lambda b, h: (h, 0, 0)),
        ],
        out_specs=pl.BlockSpec((1, s, HEAD_DIM), lambda b, h: (b, 0, h)),
        out_shape=jax.ShapeDtypeStruct((bsz, s, d), bf),
        scratch_shapes=[pltpu.VMEM((HEAD_DIM + _ONES_ROWS, s), bf),
                        pltpu.VMEM((_WIN, _GROUP), f32),
                        pltpu.VMEM((_WIN, _GROUP), f32),
                        pltpu.VMEM((_WIN, _GROUP), bf),
                        pltpu.VMEM((_WIN, _GROUP), bf)],
        compiler_params=_params(("arbitrary", "arbitrary")),
        name="band_attention",
    )(kvq, kvq, kvq, bias_t)


_BIAS_SPAN = _WIN + _GROUP


def _band_bias_kernel(g_ref, o_ref):
    row = lax.broadcasted_iota(jnp.int32, (_KB, 2 * _GROUP), 0)
    for blk in range(_NBLK):
        start = (_NBLK - 1 - blk) * _KB
        x = jnp.broadcast_to(g_ref[0, :, start:start + 2 * _GROUP], (_KB, 2 * _GROUP))
        shift = 1
        while shift < _KB:
            x = jnp.where((row & shift) != 0, pltpu.roll(x, shift, axis=1), x)
            shift *= 2
        b = x[:, _GROUP:]
        j = lax.broadcasted_iota(jnp.int32, (_KB, _GROUP), 0) + blk * _KB
        i = lax.broadcasted_iota(jnp.int32, (_KB, _GROUP), 1)
        km = j - (i // CHUNK) * CHUNK
        o_ref[0, blk * _KB:(blk + 1) * _KB, :] = jnp.where((km >= 0) & (km < _BAND), b, -jnp.inf)


def _band_bias(rel_table):
    nh = rel_table.shape[0]
    k = jnp.arange(_BIAS_SPAN)
    idx = jnp.clip(k - _WIN + _PAD, -(CHUNK - 1), REL_CLIP) + (CHUNK - 1)
    g = (rel_table[:, idx].astype(jnp.float32) * LOG2E).reshape(nh, 1, _BIAS_SPAN)
    return pl.pallas_call(
        _band_bias_kernel,
        grid=(nh,),
        in_specs=[pl.BlockSpec((1, 1, _BIAS_SPAN), lambda h: (h, 0, 0))],
        out_specs=pl.BlockSpec((1, _WIN, _GROUP), lambda h: (h, 0, 0)),
        out_shape=jax.ShapeDtypeStruct((nh, _WIN, _GROUP), jnp.float32),
        compiler_params=_params(("arbitrary",)),
        name="band_bias",
    )(g)


def kernel(x, a_norm_g, a_w_in, a_b_f, a_q_g, a_k_g, a_w_out, mlp_norm_g, mlp_w1, mlp_w2,
           kv_norm_g, kv_w, kv_k_g, b_norm_g, b_w_q, b_q_g, b_rel, b_w_out):
    bsz, s, d = x.shape
    n = bsz * s
    nh = N_HEADS
    bf = jnp.bfloat16
    q_scale = HEAD_DIM ** -0.5 * LOG2E

    h = x.reshape(n, d)

    w_qkv, w_f = _cast_split_cols(a_w_in, 0, 3 * d)
    w1_all = mlp_w1.astype(bf)
    w2_all = mlp_w2.astype(bf)
    qkv, fz = _proj(h, [a_norm_g[0]], [w_qkv], [a_q_g[0] * q_scale, a_k_g[0]],
                    [(0, 0), (0, 1), (0, None)], w_f)
    b_f = jnp.pad(a_b_f[0], (0, LANES - nh)).reshape(1, LANES)
    o = _fox_from_fz(qkv.reshape(bsz, s, 3 * d), fz.reshape(bsz, s, LANES), b_f)
    h = _out_proj(h, o.reshape(n, d), a_w_out[0].astype(bf))
    h = _mlp(h, mlp_norm_g[0], w1_all, w2_all, 0)

    kvq = _proj(h, [kv_norm_g, b_norm_g[0]], [kv_w.astype(bf), b_w_q[0].astype(bf)],
                [kv_k_g, b_q_g[0] * q_scale],
                [(0, 0), (0, None), (1, 1)])

    o = _band_attention(kvq.reshape(bsz, s, 3 * d), _band_bias(b_rel[0]))
    h = _out_proj(h, o.reshape(n, d), b_w_out[0].astype(bf))
    h = _mlp(h, mlp_norm_g[1], w1_all, w2_all, 1)
    return h.reshape(bsz, s, d)
```

```python
import functools
import math

import numpy as np

import jax
import jax.numpy as jnp
from jax import lax
from jax.experimental import pallas as pl
from jax.experimental.pallas import tpu as pltpu

N_HEADS = 16
HEAD_DIM = 128
CHUNK = 64
N_PREV_CHUNKS = 8
REL_CLIP = 256
EPS = 1e-6
LOG2E = math.log2(math.e)

LANES = 128
_ONES_ROWS = 16
VMEM_LIMIT = 60 * 1024 * 1024

_BAND = (N_PREV_CHUNKS + 1) * CHUNK
_PAD = N_PREV_CHUNKS * CHUNK
_GROUP = 4 * CHUNK
_WIN = _PAD + _GROUP


def _params(sem):
    return pltpu.CompilerParams(dimension_semantics=sem, vmem_limit_bytes=VMEM_LIMIT)


def _cast_kernel(w_ref, tail_ref, o_ref, otail_ref, *, n_tail):
    o_ref[...] = w_ref[...].astype(o_ref.dtype)
    lane = lax.broadcasted_iota(jnp.int32, tail_ref.shape, 1)
    otail_ref[...] = jnp.where(lane < n_tail, tail_ref[...], 0.0).astype(otail_ref.dtype)


def _cast_split_cols(w, layer, ncols, *, tr=256):
    _, rows, total = w.shape
    n_tail = total - ncols
    assert rows % tr == 0 and ncols % LANES == 0 and 0 < n_tail < LANES
    return pl.pallas_call(
        functools.partial(_cast_kernel, n_tail=n_tail),
        grid=(rows // tr,),
        in_specs=[pl.BlockSpec((None, tr, ncols), lambda i: (layer, i, 0)),
                  pl.BlockSpec((None, tr, LANES), lambda i: (layer, i, ncols // LANES))],
        out_specs=[pl.BlockSpec((tr, ncols), lambda i: (i, 0)),
                   pl.BlockSpec((tr, LANES), lambda i: (i, 0))],
        out_shape=[jax.ShapeDtypeStruct((rows, ncols), jnp.bfloat16),
                   jax.ShapeDtypeStruct((rows, LANES), jnp.bfloat16)],
        compiler_params=_params(("arbitrary",)),
        name="cast_cols",
    )(w, w)


def _proj_kernel(*refs, parts, n_w, has_f, tn):
    x_ref, g_ref, gain_ref = refs[:3]
    w_refs = refs[3:3 + n_w]
    if has_f:
        wf_ref, o_ref, f_ref, u_ref = refs[3 + n_w:]
    else:
        o_ref, u_ref = refs[3 + n_w:]
    w_parts = [(w_ref, c) for w_ref in w_refs for c in range(w_ref.shape[1] // tn)]
    n_norms = u_ref.shape[0]
    x = x_ref[...]
    xn = x * lax.rsqrt(jnp.mean(x * x, axis=-1, keepdims=True) + EPS)
    for k in range(n_norms):
        u_ref[k] = (xn * g_ref[k:k + 1, :]).astype(u_ref.dtype)
    if has_f:
        f_ref[...] = jnp.dot(u_ref[0], wf_ref[...], preferred_element_type=jnp.float32)
    for j, (norm_idx, gain_idx) in enumerate(parts):
        w_ref, c = w_parts[j]
        y = jnp.dot(u_ref[norm_idx], w_ref[:, c * tn:(c + 1) * tn],
                    preferred_element_type=jnp.float32)
        if gain_idx is None:
            o_ref[:, j * tn:(j + 1) * tn] = y.astype(o_ref.dtype)
            continue
        gain = gain_ref[gain_idx:gain_idx + 1, :]
        for c in range(tn // HEAD_DIM):
            ys = y[:, c * HEAD_DIM:(c + 1) * HEAD_DIM]
            r = lax.rsqrt(jnp.mean(ys * ys, axis=-1, keepdims=True) + EPS)
            sl = slice(j * tn + c * HEAD_DIM, j * tn + (c + 1) * HEAD_DIM)
            o_ref[:, sl] = (ys * r * gain).astype(o_ref.dtype)


def _proj(x, norm_gains, ws, head_gains, parts, wf=None, *, tm=512, tn=2048):
    n, d = x.shape
    nout = sum(w.shape[1] for w in ws)
    assert len(parts) * tn == nout and all(w.shape[1] % tn == 0 for w in ws)
    has_f = wf is not None
    n_norms = len(norm_gains)
    once = pl.Buffered(1)
    in_specs = [
        pl.BlockSpec((tm, d), lambda i: (i, 0)),
        pl.BlockSpec((n_norms, d), lambda i: (0, 0)),
        pl.BlockSpec((len(head_gains), HEAD_DIM), lambda i: (0, 0)),
    ] + [pl.BlockSpec(w.shape, lambda i: (0, 0), pipeline_mode=once) for w in ws]
    args = [x, jnp.stack(norm_gains), jnp.stack(head_gains), *ws]
    out_shape = [jax.ShapeDtypeStruct((n, nout), jnp.bfloat16)]
    out_specs = [pl.BlockSpec((tm, nout), lambda i: (i, 0))]
    if has_f:
        in_specs.append(pl.BlockSpec((d, LANES), lambda i: (0, 0), pipeline_mode=once))
        args.append(wf)
        out_shape.append(jax.ShapeDtypeStruct((n, LANES), jnp.float32))
        out_specs.append(pl.BlockSpec((tm, LANES), lambda i: (i, 0)))
    res = pl.pallas_call(
        functools.partial(_proj_kernel, parts=tuple(parts), n_w=len(ws), has_f=has_f, tn=tn),
        grid=(n // tm,),
        in_specs=in_specs,
        out_specs=out_specs,
        out_shape=out_shape,
        scratch_shapes=[pltpu.VMEM((n_norms, tm, d), jnp.bfloat16)],
        compiler_params=_params(("arbitrary",)),
        name="norm_proj",
    )(*args)
    return res if has_f else res[0]


_HEAD_LANES = LANES // N_HEADS
_N_TERMS = 3


def _bias_column_constants():
    sel = np.zeros((_N_TERMS * LANES, 2 * LANES), np.float32)
    one = np.zeros((1, 2 * LANES), np.float32)
    for h in range(N_HEADS):
        for term in range(_N_TERMS):
            sel[term * LANES + h, h * _HEAD_LANES + term] = -1.0
            sel[term * LANES + h, LANES + h * _HEAD_LANES + _N_TERMS + term] = 1.0
            one[0, h * _HEAD_LANES + _N_TERMS + term] = 1.0
            one[0, LANES + h * _HEAD_LANES + term] = 1.0
    return jnp.asarray(sel, jnp.bfloat16), jnp.asarray(one, jnp.float32)


def _logf_cumsum_kernel(f_ref, b_ref, sel_ref, one_ref, qx_ref, kx_ref):
    z = f_ref[0] + b_ref[...]
    c = jnp.minimum(z, 0.0) - jnp.log1p(jnp.exp(-jnp.abs(z)))
    s = c.shape[0]
    row = lax.broadcasted_iota(jnp.int32, c.shape, 0)
    d = 1
    while d < s:
        c = c + jnp.where(row >= d, pltpu.roll(c, d, axis=0), 0.0)
        d *= 2
    c = c * LOG2E
    hi = c.astype(jnp.bfloat16)
    r = c - hi.astype(jnp.float32)
    mid = r.astype(jnp.bfloat16)
    lo = (r - mid.astype(jnp.float32)).astype(jnp.bfloat16)
    terms = jnp.concatenate([hi, mid, lo], axis=1)
    y = jnp.dot(terms, sel_ref[...], preferred_element_type=jnp.float32) + one_ref[...]
    kx_ref[0] = y[:, :LANES].astype(kx_ref.dtype)
    qx_ref[0] = y[:, LANES:].astype(qx_ref.dtype)


def _logf_cumsum(fz, b):
    bsz, s, _ = fz.shape
    sel, one = _bias_column_constants()
    spec = pl.BlockSpec((1, s, LANES), lambda i: (i, 0, 0))
    out = jax.ShapeDtypeStruct((bsz, s, LANES), jnp.bfloat16)
    return pl.pallas_call(
        _logf_cumsum_kernel,
        grid=(bsz,),
        in_specs=[spec, pl.BlockSpec((1, LANES), lambda i: (0, 0)),
                  pl.BlockSpec(sel.shape, lambda i: (0, 0)),
                  pl.BlockSpec(one.shape, lambda i: (0, 0))],
        out_specs=[spec, spec],
        out_shape=[out, out],
        compiler_params=_params(("arbitrary",)),
        name="logf_cumsum",
    )(fz, b, sel, one)


_NT = (((1,), (1,)), ((), ()))


def _fox_kernel(q_ref, qx_ref, k_ref, kx_ref, v_ref, o_ref,
                qa_ref, ka_ref, vt_ref, s0_ref, s1_ref, p0_ref, p1_ref, acc_ref, *, t, tc, rb):
    nt = ka_ref.shape[0]
    nc = t // tc
    nr = t // rb
    f32 = jnp.float32
    s_refs, p_refs = (s0_ref, s1_ref), (p0_ref, p1_ref)
    tiles = [(qi, kv) for qi in range(nt) for kv in range(qi + 1)]

    lane = lax.broadcasted_iota(jnp.int32, (t, LANES), 1)
    own_lanes = lane // _HEAD_LANES == pl.program_id(1)

    for j in range(nt):
        rows = slice(j * t, (j + 1) * t)
        ka_ref[j, :, 0:HEAD_DIM] = k_ref[0, rows, :]
        ka_ref[j, :, HEAD_DIM:] = kx_ref[0, rows, :]
        qa_ref[j, :, 0:HEAD_DIM] = q_ref[0, rows, :]
        qx = qx_ref[0, rows, :]
        qa_ref[j, :, HEAD_DIM:] = jnp.where(own_lanes, qx, jnp.zeros_like(qx))
        vt_ref[0:HEAD_DIM, rows] = v_ref[0, rows, :].T
    sub = lax.broadcasted_iota(jnp.int32, (_ONES_ROWS, vt_ref.shape[1]), 0)
    vt_ref[HEAD_DIM:, :] = jnp.where(sub == 0, 1.0, 0.0).astype(vt_ref.dtype)

    def fold8(x, op):
        return op(x.reshape(x.shape[0] // 8, 8, x.shape[1]), axis=0)

    def live(tile, r, c):
        return tile[1] < tile[0] or r * rb < (c + 1) * tc

    def key_rows(tile, c):
        return t if tile[1] < tile[0] else min(t, -(-(c + 1) * tc // rb) * rb)

    def logits_block(tile, r, c, s_out):
        qi, kv = tile
        s = lax.dot_general(ka_ref[kv, r * rb:(r + 1) * rb, :],
                            qa_ref[qi, c * tc:(c + 1) * tc, :], _NT,
                            preferred_element_type=f32)
        if kv == qi and (r + 1) * rb > c * tc:
            key = lax.broadcasted_iota(jnp.int32, s.shape, 0) + r * rb
            qry = lax.broadcasted_iota(jnp.int32, s.shape, 1) + c * tc
            s = jnp.where(key <= qry, s, -jnp.inf)
        s_out[r * rb:(r + 1) * rb, c * tc:(c + 1) * tc] = s
        return fold8(s, jnp.max)

    blocks = [(r, c) for r in range(nr) for c in range(nc)]

    def write_out(qb):
        o = acc_ref[0:HEAD_DIM, :] / acc_ref[HEAD_DIM:HEAD_DIM + 1, :]
        o_ref[0, qb * t:(qb + 1) * t, :] = o.T.astype(o_ref.dtype)

    def tile_max(mx):
        return [jnp.max(mx[c], axis=0, keepdims=True) for c in range(nc)]

    mx = [None] * nc
    for r, c in blocks:
        if not live(tiles[0], r, c):
            continue
        pm = logits_block(tiles[0], r, c, s_refs[0])
        mx[c] = pm if mx[c] is None else jnp.maximum(mx[c], pm)
    mc = tile_max(mx)

    m_run = alpha_prev = None
    for n, (qi, kv) in enumerate(tiles):
        cur, nxt = n % 2, 1 - n % 2
        nxt_tile = tiles[n + 1] if n + 1 < len(tiles) else None
        prv_tile = tiles[n - 1] if n > 0 else None
        if kv == 0:
            m_new = mc
            alpha = None
        else:
            m_new = [jnp.maximum(m_run[c], mc[c]) for c in range(nc)]
            alpha = [jnp.exp2(m_run[c] - m_new[c]) for c in range(nc)]
        mx = [None] * nc
        pv = [None] * nc
        for r, c in blocks:
            rows = slice(r * rb, (r + 1) * rb)
            cols = slice(c * tc, (c + 1) * tc)
            if nxt_tile is not None and live(nxt_tile, r, c):
                pm = logits_block(nxt_tile, r, c, s_refs[nxt])
                mx[c] = pm if mx[c] is None else jnp.maximum(mx[c], pm)
            if live((qi, kv), r, c):
                e = (s_refs[cur][rows, cols] - m_new[c]).astype(p_refs[cur].dtype)
                p_refs[cur][rows, cols] = jnp.exp2(e)
            if prv_tile is not None and r == nr - 1:
                nk = key_rows(prv_tile, c)
                k0 = prv_tile[1] * t
                pv[c] = jnp.dot(vt_ref[:, k0:k0 + nk], p_refs[nxt][0:nk, cols],
                                preferred_element_type=f32)
        if prv_tile is not None:
            for c in range(nc):
                cols = slice(c * tc, (c + 1) * tc)
                if prv_tile[1] == 0:
                    acc_ref[:, cols] = pv[c]
                else:
                    acc_ref[:, cols] = alpha_prev[c] * acc_ref[:, cols] + pv[c]
            if prv_tile[1] == prv_tile[0]:
                write_out(prv_tile[0])
        m_run, alpha_prev = m_new, alpha
        if nxt_tile is not None:
            mc = tile_max(mx)

    last = len(tiles) - 1
    qi, kv = tiles[last]
    for c in range(nc):
        cols = slice(c * tc, (c + 1) * tc)
        nk = key_rows(tiles[last], c)
        d = jnp.dot(vt_ref[:, kv * t:kv * t + nk], p_refs[last % 2][0:nk, cols],
                    preferred_element_type=f32)
        acc_ref[:, cols] = d if kv == 0 else alpha_prev[c] * acc_ref[:, cols] + d
    write_out(qi)


def _fox_attention(qkv, qx, kx, *, t=512, tc=256, rb=256):
    bsz, s, d3 = qkv.shape
    d = d3 // 3
    nh = d // HEAD_DIM
    nt = s // t
    f32 = jnp.float32
    bf = jnp.bfloat16
    return pl.pallas_call(
        functools.partial(_fox_kernel, t=t, tc=tc, rb=rb),
        grid=(bsz, nh),
        in_specs=[
            pl.BlockSpec((1, s, HEAD_DIM), lambda b, h: (b, 0, h)),
            pl.BlockSpec((1, s, LANES), lambda b, h: (b, 0, 0)),
            pl.BlockSpec((1, s, HEAD_DIM), lambda b, h: (b, 0, nh + h)),
            pl.BlockSpec((1, s, LANES), lambda b, h: (b, 0, 0)),
            pl.BlockSpec((1, s, HEAD_DIM), lambda b, h: (b, 0, 2 * nh + h)),
        ],
        out_specs=pl.BlockSpec((1, s, HEAD_DIM), lambda b, h: (b, 0, h)),
        out_shape=jax.ShapeDtypeStruct((bsz, s, d), bf),
        scratch_shapes=[pltpu.VMEM((nt, t, HEAD_DIM + LANES), bf),
                        pltpu.VMEM((nt, t, HEAD_DIM + LANES), bf),
                        pltpu.VMEM((HEAD_DIM + _ONES_ROWS, s), bf),
                        pltpu.VMEM((t, t), f32),
                        pltpu.VMEM((t, t), f32),
                        pltpu.VMEM((t, t), bf),
                        pltpu.VMEM((t, t), bf),
                        pltpu.VMEM((HEAD_DIM + _ONES_ROWS, t), f32)],
        compiler_params=_params(("arbitrary", "arbitrary")),
        name="fox_attention",
    )(qkv, qx, qkv, kx, qkv)


def _fox_from_fz(qkv, fz, b_f):
    qx, kx = _logf_cumsum(fz, b_f)
    return _fox_attention(qkv, qx, kx)


def _out_proj_kernel(h_ref, o_ref, w_ref, out_ref):
    out_ref[...] = h_ref[...] + jnp.dot(o_ref[...], w_ref[...],
                                        preferred_element_type=jnp.float32)


def _out_proj(h, o, w, *, tm=512):
    n, d = h.shape
    return pl.pallas_call(
        _out_proj_kernel,
        grid=(n // tm,),
        in_specs=[pl.BlockSpec((tm, d), lambda i: (i, 0)),
                  pl.BlockSpec((tm, d), lambda i: (i, 0)),
                  pl.BlockSpec((d, d), lambda i: (0, 0))],
        out_specs=pl.BlockSpec((tm, d), lambda i: (i, 0)),
        out_shape=jax.ShapeDtypeStruct((n, d), jnp.float32),
        compiler_params=_params(("arbitrary",)),
        name="out_proj",
    )(h, o, w)


def _mlp_kernel(h_ref, g_ref, w1_ref, w2_ref, out_ref, u_ref):
    f = pl.program_id(1)

    @pl.when(f == 0)
    def _():
        x = h_ref[...]
        r = lax.rsqrt(jnp.mean(x * x, axis=-1, keepdims=True) + EPS)
        u_ref[...] = (x * r * g_ref[...]).astype(u_ref.dtype)
        out_ref[...] = x

    a = jnp.dot(u_ref[...], w1_ref[...], preferred_element_type=jnp.float32)
    a = jnp.maximum(a, 0.0)
    a = (a * a).astype(w2_ref.dtype)
    out_ref[...] += jnp.dot(a, w2_ref[...], preferred_element_type=jnp.float32)


def _mlp(h, g, w1, w2, layer, *, tm=512, tf=2048):
    n, d = h.shape
    dff = w1.shape[2]
    return pl.pallas_call(
        _mlp_kernel,
        grid=(n // tm, dff // tf),
        in_specs=[pl.BlockSpec((tm, d), lambda i, f: (i, 0)),
                  pl.BlockSpec((1, d), lambda i, f: (0, 0)),
                  pl.BlockSpec((None, d, tf), lambda i, f: (layer, 0, f)),
                  pl.BlockSpec((None, tf, d), lambda i, f: (layer, f, 0))],
        out_specs=pl.BlockSpec((tm, d), lambda i, f: (i, 0)),
        out_shape=jax.ShapeDtypeStruct((n, d), jnp.float32),
        scratch_shapes=[pltpu.VMEM((tm, d), jnp.bfloat16)],
        compiler_params=_params(("arbitrary", "arbitrary")),
        name="sq_relu_mlp",
    )(h, g.reshape(1, d), w1, w2)


_KB = _GROUP
_NBLK = _WIN // _KB


def _band_kernel(q_ref, k_ref, v_ref, bias_ref, o_ref, vt_ref, s0_ref, s1_ref, p0_ref, p1_ref):
    ng = vt_ref.shape[1] // _GROUP
    f32 = jnp.float32
    s_refs, p_refs = (s0_ref, s1_ref), (p0_ref, p1_ref)

    for j in range(ng):
        vt_ref[0:HEAD_DIM, j * _KB:(j + 1) * _KB] = v_ref[0, j * _KB:(j + 1) * _KB, :].T
    sub = lax.broadcasted_iota(jnp.int32, (_ONES_ROWS, vt_ref.shape[1]), 0)
    vt_ref[HEAD_DIM:, :] = jnp.where(sub == 0, 1.0, 0.0).astype(vt_ref.dtype)

    def fold8(x, op):
        return op(x.reshape(x.shape[0] // 8, 8, x.shape[1]), axis=0)

    def rows_of(g):
        return slice(g * _GROUP, (g + 1) * _GROUP)

    def key_blocks(g):
        return [(blk, g - (_NBLK - 1) + blk) for blk in range(_NBLK) if g - (_NBLK - 1) + blk >= 0]

    def logits_block(g, blk, kb, s_out):
        s = lax.dot_general(k_ref[0, rows_of(kb), :], q_ref[0, rows_of(g), :], _NT,
                            preferred_element_type=f32) + bias_ref[0, blk * _KB:(blk + 1) * _KB, :]
        s_out[blk * _KB:(blk + 1) * _KB, :] = s
        return fold8(s, jnp.max)

    def values(g, p_ref):
        blks = key_blocks(g)
        (b0, k0), nb = blks[0], len(blks)
        return jnp.dot(vt_ref[:, k0 * _KB:(k0 + nb) * _KB], p_ref[b0 * _KB:(b0 + nb) * _KB, :],
                       preferred_element_type=f32)

    def write_out(g, pv):
        o = pv[0:HEAD_DIM] / pv[HEAD_DIM:HEAD_DIM + 1]
        o_ref[0, rows_of(g), :] = o.T.astype(o_ref.dtype)

    mx = None
    for blk, kb in key_blocks(0):
        pm = logits_block(0, blk, kb, s_refs[0])
        mx = pm if mx is None else jnp.maximum(mx, pm)
    m = jnp.max(mx, axis=0, keepdims=True)

    for g in range(ng):
        cur, nxt = g % 2, 1 - g % 2
        nxt_blocks = key_blocks(g + 1) if g + 1 < ng else []
        cur_blocks = key_blocks(g)
        mx = None
        for i in range(_NBLK):
            if i < len(nxt_blocks):
                blk, kb = nxt_blocks[i]
                pm = logits_block(g + 1, blk, kb, s_refs[nxt])
                mx = pm if mx is None else jnp.maximum(mx, pm)
            if i < len(cur_blocks):
                blk, _ = cur_blocks[i]
                e = (s_refs[cur][blk * _KB:(blk + 1) * _KB, :] - m).astype(p_refs[cur].dtype)
                p_refs[cur][blk * _KB:(blk + 1) * _KB, :] = jnp.exp2(e)
        if g > 0:
            write_out(g - 1, values(g - 1, p_refs[nxt]))
        if nxt_blocks:
            m = jnp.max(mx, axis=0, keepdims=True)

    write_out(ng - 1, values(ng - 1, p_refs[(ng - 1) % 2]))


def _band_attention(kvq, bias_t):
    bsz, s, d3 = kvq.shape
    d = d3 // 3
    nh = d // HEAD_DIM
    assert s % _GROUP == 0
    f32 = jnp.float32
    bf = jnp.bfloat16
    return pl.pallas_call(
        _band_kernel,
        grid=(bsz, nh),
        in_specs=[
            pl.BlockSpec((1, s, HEAD_DIM), lambda b, h: (b, 0, 2 * nh + h)),
            pl.BlockSpec((1, s, HEAD_DIM), lambda b, h: (b, 0, h)),
            pl.BlockSpec((1, s, HEAD_DIM), lambda b, h: (b, 0, nh + h)),
            pl.BlockSpec((1, _WIN, _GROUP), lambda b, h: (h, 0, 0)),
        ],
        out_specs=pl.BlockSpec((1, s, HEAD_DIM), lambda b, h: (b, 0, h)),
        out_shape=jax.ShapeDtypeStruct((bsz, s, d), bf),
        scratch_shapes=[pltpu.VMEM((HEAD_DIM + _ONES_ROWS, s), bf),
                        pltpu.VMEM((_WIN, _GROUP), f32),
                        pltpu.VMEM((_WIN, _GROUP), f32),
                        pltpu.VMEM((_WIN, _GROUP), bf),
                        pltpu.VMEM((_WIN, _GROUP), bf)],
        compiler_params=_params(("arbitrary", "arbitrary")),
        name="band_attention",
    )(kvq, kvq, kvq, bias_t)


_BIAS_SPAN = _WIN + _GROUP


def _band_bias_kernel(g_ref, o_ref):
    row = lax.broadcasted_iota(jnp.int32, (_KB, 2 * _GROUP), 0)
    for blk in range(_NBLK):
        start = (_NBLK - 1 - blk) * _KB
        x = jnp.broadcast_to(g_ref[0, :, start:start + 2 * _GROUP], (_KB, 2 * _GROUP))
        shift = 1
        while shift < _KB:
            x = jnp.where((row & shift) != 0, pltpu.roll(x, shift, axis=1), x)
            shift *= 2
        b = x[:, _GROUP:]
        j = lax.broadcasted_iota(jnp.int32, (_KB, _GROUP), 0) + blk * _KB
        i = lax.broadcasted_iota(jnp.int32, (_KB, _GROUP), 1)
        km = j - (i // CHUNK) * CHUNK
        o_ref[0, blk * _KB:(blk + 1) * _KB, :] = jnp.where((km >= 0) & (km < _BAND), b, -jnp.inf)


def _band_bias(rel_table):
    nh = rel_table.shape[0]
    k = jnp.arange(_BIAS_SPAN)
    idx = jnp.clip(k - _WIN + _PAD, -(CHUNK - 1), REL_CLIP) + (CHUNK - 1)
    g = (rel_table[:, idx].astype(jnp.float32) * LOG2E).reshape(nh, 1, _BIAS_SPAN)
    return pl.pallas_call(
        _band_bias_kernel,
        grid=(nh,),
        in_specs=[pl.BlockSpec((1, 1, _BIAS_SPAN), lambda h: (h, 0, 0))],
        out_specs=pl.BlockSpec((1, _WIN, _GROUP), lambda h: (h, 0, 0)),
        out_shape=jax.ShapeDtypeStruct((nh, _WIN, _GROUP), jnp.float32),
        compiler_params=_params(("arbitrary",)),
        name="band_bias",
    )(g)


def kernel(x, a_norm_g, a_w_in, a_b_f, a_q_g, a_k_g, a_w_out, mlp_norm_g, mlp_w1, mlp_w2,
           kv_norm_g, kv_w, kv_k_g, b_norm_g, b_w_q, b_q_g, b_rel, b_w_out):
    bsz, s, d = x.shape
    n = bsz * s
    nh = N_HEADS
    bf = jnp.bfloat16
    q_scale = HEAD_DIM ** -0.5 * LOG2E

    h = x.reshape(n, d)

    w_qkv, w_f = _cast_split_cols(a_w_in, 0, 3 * d)
    w1_all = mlp_w1.astype(bf)
    w2_all = mlp_w2.astype(bf)
    qkv, fz = _proj(h, [a_norm_g[0]], [w_qkv], [a_q_g[0] * q_scale, a_k_g[0]],
                    [(0, 0), (0, 1), (0, None)], w_f)
    b_f = jnp.pad(a_b_f[0], (0, LANES - nh)).reshape(1, LANES)
    o = _fox_from_fz(qkv.reshape(bsz, s, 3 * d), fz.reshape(bsz, s, LANES), b_f)
    h = _out_proj(h, o.reshape(n, d), a_w_out[0].astype(bf))
    h = _mlp(h, mlp_norm_g[0], w1_all, w2_all, 0)

    kvq = _proj(h, [kv_norm_g, b_norm_g[0]], [kv_w.astype(bf), b_w_q[0].astype(bf)],
                [kv_k_g, b_q_g[0] * q_scale],
                [(0, 0), (0, None), (1, 1)])

    o = _band_attention(kvq.reshape(bsz, s, 3 * d), _band_bias(b_rel[0]))
    h = _out_proj(h, o.reshape(n, d), b_w_out[0].astype(bf))
    h = _mlp(h, mlp_norm_g[1], w1_all, w2_all, 1)
    return h.reshape(bsz, s, d)
```

```python
import functools
import math

import numpy as np

import jax
import jax.numpy as jnp
from jax import lax
from jax.experimental import pallas as pl
from jax.experimental.pallas import tpu as pltpu

N_HEADS = 16
HEAD_DIM = 128
CHUNK = 64
N_PREV_CHUNKS = 8
REL_CLIP = 256
EPS = 1e-6
LOG2E = math.log2(math.e)

LANES = 128
VMEM_LIMIT = 60 * 1024 * 1024

_BAND = (N_PREV_CHUNKS + 1) * CHUNK
_PAD = N_PREV_CHUNKS * CHUNK
_GROUP = 4 * CHUNK
_WIN = _PAD + _GROUP


def _params(sem):
    return pltpu.CompilerParams(dimension_semantics=sem, vmem_limit_bytes=VMEM_LIMIT)


def _cast_kernel(w_ref, tail_ref, o_ref, otail_ref, *, n_tail):
    o_ref[...] = w_ref[...].astype(o_ref.dtype)
    lane = lax.broadcasted_iota(jnp.int32, tail_ref.shape, 1)
    otail_ref[...] = jnp.where(lane < n_tail, tail_ref[...], 0.0).astype(otail_ref.dtype)


def _cast_split_cols(w, layer, ncols, *, tr=256):
    _, rows, total = w.shape
    n_tail = total - ncols
    assert rows % tr == 0 and ncols % LANES == 0 and 0 < n_tail < LANES
    return pl.pallas_call(
        functools.partial(_cast_kernel, n_tail=n_tail),
        grid=(rows // tr,),
        in_specs=[pl.BlockSpec((None, tr, ncols), lambda i: (layer, i, 0)),
                  pl.BlockSpec((None, tr, LANES), lambda i: (layer, i, ncols // LANES))],
        out_specs=[pl.BlockSpec((tr, ncols), lambda i: (i, 0)),
                   pl.BlockSpec((tr, LANES), lambda i: (i, 0))],
        out_shape=[jax.ShapeDtypeStruct((rows, ncols), jnp.bfloat16),
                   jax.ShapeDtypeStruct((rows, LANES), jnp.bfloat16)],
        compiler_params=_params(("arbitrary",)),
        name="cast_cols",
    )(w, w)


def _proj_kernel(*refs, parts, n_w, has_f, tn):
    x_ref, g_ref, gain_ref = refs[:3]
    w_refs = refs[3:3 + n_w]
    if has_f:
        wf_ref, o_ref, f_ref, u_ref = refs[3 + n_w:]
    else:
        o_ref, u_ref = refs[3 + n_w:]
    w_parts = [(w_ref, c) for w_ref in w_refs for c in range(w_ref.shape[1] // tn)]
    n_norms = u_ref.shape[0]
    x = x_ref[...]
    xn = x * lax.rsqrt(jnp.mean(x * x, axis=-1, keepdims=True) + EPS)
    for k in range(n_norms):
        u_ref[k] = (xn * g_ref[k:k + 1, :]).astype(u_ref.dtype)
    if has_f:
        f_ref[...] = jnp.dot(u_ref[0], wf_ref[...], preferred_element_type=jnp.float32)
    order = sorted(range(len(parts)), key=lambda j: parts[j][1] is None)
    for j in order:
        norm_idx, gain_idx = parts[j]
        w_ref, c = w_parts[j]
        y = jnp.dot(u_ref[norm_idx], w_ref[:, c * tn:(c + 1) * tn],
                    preferred_element_type=jnp.float32)
        if gain_idx is None:
            o_ref[:, j * tn:(j + 1) * tn] = y.astype(o_ref.dtype)
            continue
        gain = gain_ref[gain_idx:gain_idx + 1, :]
        for c in range(tn // HEAD_DIM):
            ys = y[:, c * HEAD_DIM:(c + 1) * HEAD_DIM]
            r = lax.rsqrt(jnp.mean(ys * ys, axis=-1, keepdims=True) + EPS)
            sl = slice(j * tn + c * HEAD_DIM, j * tn + (c + 1) * HEAD_DIM)
            o_ref[:, sl] = (ys * r * gain).astype(o_ref.dtype)


def _proj(x, norm_gains, ws, head_gains, parts, wf=None, *, tm=512, tn=2048):
    n, d = x.shape
    nout = sum(w.shape[1] for w in ws)
    assert len(parts) * tn == nout and all(w.shape[1] % tn == 0 for w in ws)
    has_f = wf is not None
    n_norms = len(norm_gains)
    once = pl.Buffered(1)
    in_specs = [
        pl.BlockSpec((tm, d), lambda i: (i, 0)),
        pl.BlockSpec((n_norms, d), lambda i: (0, 0)),
        pl.BlockSpec((len(head_gains), HEAD_DIM), lambda i: (0, 0)),
    ] + [pl.BlockSpec(w.shape, lambda i: (0, 0), pipeline_mode=once) for w in ws]
    args = [x, jnp.stack(norm_gains), jnp.stack(head_gains), *ws]
    out_shape = [jax.ShapeDtypeStruct((n, nout), jnp.bfloat16)]
    out_specs = [pl.BlockSpec((tm, nout), lambda i: (i, 0))]
    if has_f:
        in_specs.append(pl.BlockSpec((d, LANES), lambda i: (0, 0), pipeline_mode=once))
        args.append(wf)
        out_shape.append(jax.ShapeDtypeStruct((n, LANES), jnp.float32))
        out_specs.append(pl.BlockSpec((tm, LANES), lambda i: (i, 0)))
    res = pl.pallas_call(
        functools.partial(_proj_kernel, parts=tuple(parts), n_w=len(ws), has_f=has_f, tn=tn),
        grid=(n // tm,),
        in_specs=in_specs,
        out_specs=out_specs,
        out_shape=out_shape,
        scratch_shapes=[pltpu.VMEM((n_norms, tm, d), jnp.bfloat16)],
        compiler_params=_params(("arbitrary",)),
        name="norm_proj",
    )(*args)
    return res if has_f else res[0]


_HEAD_LANES = LANES // N_HEADS
_N_TERMS = 3


def _bias_column_constants():
    sel = np.zeros((_N_TERMS * LANES, 2 * LANES), np.float32)
    one = np.zeros((1, 2 * LANES), np.float32)
    for h in range(N_HEADS):
        for term in range(_N_TERMS):
            sel[term * LANES + h, h * _HEAD_LANES + term] = -1.0
            sel[term * LANES + h, LANES + h * _HEAD_LANES + _N_TERMS + term] = 1.0
            one[0, h * _HEAD_LANES + _N_TERMS + term] = 1.0
            one[0, LANES + h * _HEAD_LANES + term] = 1.0
    return jnp.asarray(sel, jnp.bfloat16), jnp.asarray(one, jnp.float32)


def _logf_cumsum_kernel(f_ref, b_ref, sel_ref, one_ref, qx_ref, kx_ref):
    z = f_ref[0] + b_ref[...]
    c = jnp.minimum(z, 0.0) - jnp.log1p(jnp.exp(-jnp.abs(z)))
    s = c.shape[0]
    row = lax.broadcasted_iota(jnp.int32, c.shape, 0)
    d = 1
    while d < s:
        c = c + jnp.where(row >= d, pltpu.roll(c, d, axis=0), 0.0)
        d *= 2
    c = c * LOG2E
    hi = c.astype(jnp.bfloat16)
    r = c - hi.astype(jnp.float32)
    mid = r.astype(jnp.bfloat16)
    lo = (r - mid.astype(jnp.float32)).astype(jnp.bfloat16)
    terms = jnp.concatenate([hi, mid, lo], axis=1)
    y = jnp.dot(terms, sel_ref[...], preferred_element_type=jnp.float32) + one_ref[...]
    kx_ref[0] = y[:, :LANES].astype(kx_ref.dtype)
    qx_ref[0] = y[:, LANES:].astype(qx_ref.dtype)


def _logf_cumsum(fz, b):
    bsz, s, _ = fz.shape
    sel, one = _bias_column_constants()
    spec = pl.BlockSpec((1, s, LANES), lambda i: (i, 0, 0))
    out = jax.ShapeDtypeStruct((bsz, s, LANES), jnp.bfloat16)
    return pl.pallas_call(
        _logf_cumsum_kernel,
        grid=(bsz,),
        in_specs=[spec, pl.BlockSpec((1, LANES), lambda i: (0, 0)),
                  pl.BlockSpec(sel.shape, lambda i: (0, 0)),
                  pl.BlockSpec(one.shape, lambda i: (0, 0))],
        out_specs=[spec, spec],
        out_shape=[out, out],
        compiler_params=_params(("arbitrary",)),
        name="logf_cumsum",
    )(fz, b, sel, one)


_NT = (((1,), (1,)), ((), ()))


def _fox_kernel(q_ref, qx_ref, k_ref, kx_ref, v_ref, o_ref,
                qa_ref, ka_ref, vt_ref, s0_ref, s1_ref, p0_ref, p1_ref, acc_ref, *, t, tc, rb):
    nt = ka_ref.shape[0]
    nc = t // tc
    nr = t // rb
    f32 = jnp.float32
    s_refs, p_refs = (s0_ref, s1_ref), (p0_ref, p1_ref)
    tiles = [(qi, kv) for qi in range(nt) for kv in range(qi + 1)]

    lane = lax.broadcasted_iota(jnp.int32, (t, LANES), 1)
    own_lanes = lane // _HEAD_LANES == pl.program_id(1)

    for j in range(nt):
        rows = slice(j * t, (j + 1) * t)
        ka_ref[j, :, 0:HEAD_DIM] = k_ref[0, rows, :]
        ka_ref[j, :, HEAD_DIM:] = kx_ref[0, rows, :]
        qa_ref[j, :, 0:HEAD_DIM] = q_ref[0, rows, :]
        qx = qx_ref[0, rows, :]
        qa_ref[j, :, HEAD_DIM:] = jnp.where(own_lanes, qx, jnp.zeros_like(qx))
        vt_ref[:, rows] = v_ref[0, rows, :].T

    def fold8(x, op):
        return op(x.reshape(x.shape[0] // 8, 8, x.shape[1]), axis=0)

    def live(tile, r, c):
        return tile[1] < tile[0] or r * rb < (c + 1) * tc

    def key_rows(tile, c):
        return t if tile[1] < tile[0] else min(t, -(-(c + 1) * tc // rb) * rb)

    def logits_block(tile, r, c, s_out):
        qi, kv = tile
        s = lax.dot_general(ka_ref[kv, r * rb:(r + 1) * rb, :],
                            qa_ref[qi, c * tc:(c + 1) * tc, :], _NT,
                            preferred_element_type=f32)
        if kv == qi and (r + 1) * rb > c * tc:
            key = lax.broadcasted_iota(jnp.int32, s.shape, 0) + r * rb
            qry = lax.broadcasted_iota(jnp.int32, s.shape, 1) + c * tc
            s = jnp.where(key <= qry, s, -jnp.inf)
        s_out[r * rb:(r + 1) * rb, c * tc:(c + 1) * tc] = s
        return fold8(s, jnp.max)

    blocks = [(r, c) for r in range(nr) for c in range(nc)]

    def tile_max(mx):
        return [jnp.max(mx[c], axis=0, keepdims=True) for c in range(nc)]

    mx = [None] * nc
    for r, c in blocks:
        if not live(tiles[0], r, c):
            continue
        pm = logits_block(tiles[0], r, c, s_refs[0])
        mx[c] = pm if mx[c] is None else jnp.maximum(mx[c], pm)
    mc = tile_max(mx)

    m_run = l_run = alpha_prev = None
    for n, (qi, kv) in enumerate(tiles):
        cur, nxt = n % 2, 1 - n % 2
        nxt_tile = tiles[n + 1] if n + 1 < len(tiles) else None
        prv_tile = tiles[n - 1] if n > 0 else None
        if kv == 0:
            m_new = mc
            alpha = None
        else:
            m_new = [jnp.maximum(m_run[c], mc[c]) for c in range(nc)]
            alpha = [jnp.exp2(m_run[c] - m_new[c]) for c in range(nc)]
        mx = [None] * nc
        ls = [None] * nc
        pv = [None] * nc
        for r, c in blocks:
            rows = slice(r * rb, (r + 1) * rb)
            cols = slice(c * tc, (c + 1) * tc)
            if nxt_tile is not None and live(nxt_tile, r, c):
                pm = logits_block(nxt_tile, r, c, s_refs[nxt])
                mx[c] = pm if mx[c] is None else jnp.maximum(mx[c], pm)
            if live((qi, kv), r, c):
                p = jnp.exp2(s_refs[cur][rows, cols] - m_new[c])
                ps = fold8(p, jnp.sum)
                ls[c] = ps if ls[c] is None else ls[c] + ps
                p_refs[cur][rows, cols] = p.astype(p_refs[cur].dtype)
            if prv_tile is not None and r == nr - 1:
                nk = key_rows(prv_tile, c)
                k0 = prv_tile[1] * t
                pv[c] = jnp.dot(vt_ref[:, k0:k0 + nk], p_refs[nxt][0:nk, cols],
                                preferred_element_type=f32)
        lsum = [jnp.sum(ls[c], axis=0, keepdims=True) for c in range(nc)]
        l_prev_block = l_run
        l_run = lsum if kv == 0 else [alpha[c] * l_run[c] + lsum[c] for c in range(nc)]
        if prv_tile is not None:
            for c in range(nc):
                cols = slice(c * tc, (c + 1) * tc)
                if prv_tile[1] == 0:
                    acc_ref[:, cols] = pv[c]
                else:
                    acc_ref[:, cols] = alpha_prev[c] * acc_ref[:, cols] + pv[c]
            if prv_tile[1] == prv_tile[0]:
                l_fin = jnp.concatenate(l_prev_block, axis=1)
                o_ref[0, prv_tile[0] * t:(prv_tile[0] + 1) * t, :] = (
                    acc_ref[...] / l_fin).T.astype(o_ref.dtype)
        m_run, alpha_prev = m_new, alpha
        if nxt_tile is not None:
            mc = tile_max(mx)

    last = len(tiles) - 1
    qi, kv = tiles[last]
    for c in range(nc):
        cols = slice(c * tc, (c + 1) * tc)
        nk = key_rows(tiles[last], c)
        d = jnp.dot(vt_ref[:, kv * t:kv * t + nk], p_refs[last % 2][0:nk, cols],
                    preferred_element_type=f32)
        acc_ref[:, cols] = d if kv == 0 else alpha_prev[c] * acc_ref[:, cols] + d
    o_ref[0, qi * t:(qi + 1) * t, :] = (
        acc_ref[...] / jnp.concatenate(l_run, axis=1)).T.astype(o_ref.dtype)


def _fox_attention(qkv, qx, kx, *, t=512, tc=256, rb=256):
    bsz, s, d3 = qkv.shape
    d = d3 // 3
    nh = d // HEAD_DIM
    nt = s // t
    f32 = jnp.float32
    bf = jnp.bfloat16
    return pl.pallas_call(
        functools.partial(_fox_kernel, t=t, tc=tc, rb=rb),
        grid=(bsz, nh),
        in_specs=[
            pl.BlockSpec((1, s, HEAD_DIM), lambda b, h: (b, 0, h)),
            pl.BlockSpec((1, s, LANES), lambda b, h: (b, 0, 0)),
            pl.BlockSpec((1, s, HEAD_DIM), lambda b, h: (b, 0, nh + h)),
            pl.BlockSpec((1, s, LANES), lambda b, h: (b, 0, 0)),
            pl.BlockSpec((1, s, HEAD_DIM), lambda b, h: (b, 0, 2 * nh + h)),
        ],
        out_specs=pl.BlockSpec((1, s, HEAD_DIM), lambda b, h: (b, 0, h)),
        out_shape=jax.ShapeDtypeStruct((bsz, s, d), bf),
        scratch_shapes=[pltpu.VMEM((nt, t, HEAD_DIM + LANES), bf),
                        pltpu.VMEM((nt, t, HEAD_DIM + LANES), bf),
                        pltpu.VMEM((HEAD_DIM, s), bf),
                        pltpu.VMEM((t, t), f32),
                        pltpu.VMEM((t, t), f32),
                        pltpu.VMEM((t, t), bf),
                        pltpu.VMEM((t, t), bf),
                        pltpu.VMEM((HEAD_DIM, t), f32)],
        compiler_params=_params(("arbitrary", "arbitrary")),
        name="fox_attention",
    )(qkv, qx, qkv, kx, qkv)


def _fox_from_fz(qkv, fz, b_f):
    qx, kx = _logf_cumsum(fz, b_f)
    return _fox_attention(qkv, qx, kx)


def _out_proj_kernel(h_ref, o_ref, w_ref, out_ref):
    out_ref[...] = h_ref[...] + jnp.dot(o_ref[...], w_ref[...],
                                        preferred_element_type=jnp.float32)


def _out_proj(h, o, w, *, tm=512):
    n, d = h.shape
    return pl.pallas_call(
        _out_proj_kernel,
        grid=(n // tm,),
        in_specs=[pl.BlockSpec((tm, d), lambda i: (i, 0)),
                  pl.BlockSpec((tm, d), lambda i: (i, 0)),
                  pl.BlockSpec((d, d), lambda i: (0, 0))],
        out_specs=pl.BlockSpec((tm, d), lambda i: (i, 0)),
        out_shape=jax.ShapeDtypeStruct((n, d), jnp.float32),
        compiler_params=_params(("arbitrary",)),
        name="out_proj",
    )(h, o, w)


def _mlp_kernel(h_ref, g_ref, w1_ref, w2_ref, out_ref, u_ref):
    f = pl.program_id(1)

    @pl.when(f == 0)
    def _():
        x = h_ref[...]
        r = lax.rsqrt(jnp.mean(x * x, axis=-1, keepdims=True) + EPS)
        u_ref[...] = (x * r * g_ref[...]).astype(u_ref.dtype)
        out_ref[...] = x

    a = jnp.dot(u_ref[...], w1_ref[...], preferred_element_type=jnp.float32)
    a = jnp.maximum(a, 0.0)
    a = (a * a).astype(w2_ref.dtype)
    out_ref[...] += jnp.dot(a, w2_ref[...], preferred_element_type=jnp.float32)


def _mlp(h, g, w1, w2, layer, *, tm=512, tf=2048):
    n, d = h.shape
    dff = w1.shape[2]
    return pl.pallas_call(
        _mlp_kernel,
        grid=(n // tm, dff // tf),
        in_specs=[pl.BlockSpec((tm, d), lambda i, f: (i, 0)),
                  pl.BlockSpec((1, d), lambda i, f: (0, 0)),
                  pl.BlockSpec((None, d, tf), lambda i, f: (layer, 0, f)),
                  pl.BlockSpec((None, tf, d), lambda i, f: (layer, f, 0))],
        out_specs=pl.BlockSpec((tm, d), lambda i, f: (i, 0)),
        out_shape=jax.ShapeDtypeStruct((n, d), jnp.float32),
        scratch_shapes=[pltpu.VMEM((tm, d), jnp.bfloat16)],
        compiler_params=_params(("arbitrary", "arbitrary")),
        name="sq_relu_mlp",
    )(h, g.reshape(1, d), w1, w2)


_KB = _GROUP
_NBLK = _WIN // _KB
_ONES_ROWS = 16


def _band_kernel(q_ref, k_ref, v_ref, bias_ref, o_ref, vt_ref, s0_ref, s1_ref, p0_ref, p1_ref):
    ng = vt_ref.shape[1] // _GROUP
    f32 = jnp.float32
    s_refs, p_refs = (s0_ref, s1_ref), (p0_ref, p1_ref)

    for j in range(ng):
        vt_ref[0:HEAD_DIM, j * _KB:(j + 1) * _KB] = v_ref[0, j * _KB:(j + 1) * _KB, :].T
    sub = lax.broadcasted_iota(jnp.int32, (_ONES_ROWS, vt_ref.shape[1]), 0)
    vt_ref[HEAD_DIM:, :] = jnp.where(sub == 0, 1.0, 0.0).astype(vt_ref.dtype)

    def fold8(x, op):
        return op(x.reshape(x.shape[0] // 8, 8, x.shape[1]), axis=0)

    def rows_of(g):
        return slice(g * _GROUP, (g + 1) * _GROUP)

    def key_blocks(g):
        return [(blk, g - (_NBLK - 1) + blk) for blk in range(_NBLK) if g - (_NBLK - 1) + blk >= 0]

    def logits_block(g, blk, kb, s_out):
        s = lax.dot_general(k_ref[0, rows_of(kb), :], q_ref[0, rows_of(g), :], _NT,
                            preferred_element_type=f32) + bias_ref[0, blk * _KB:(blk + 1) * _KB, :]
        s_out[blk * _KB:(blk + 1) * _KB, :] = s
        return fold8(s, jnp.max)

    def values(g, p_ref):
        blks = key_blocks(g)
        (b0, k0), nb = blks[0], len(blks)
        return jnp.dot(vt_ref[:, k0 * _KB:(k0 + nb) * _KB], p_ref[b0 * _KB:(b0 + nb) * _KB, :],
                       preferred_element_type=f32)

    def write_out(g, pv):
        o = pv[0:HEAD_DIM] / pv[HEAD_DIM:HEAD_DIM + 1]
        o_ref[0, rows_of(g), :] = o.T.astype(o_ref.dtype)

    mx = None
    for blk, kb in key_blocks(0):
        pm = logits_block(0, blk, kb, s_refs[0])
        mx = pm if mx is None else jnp.maximum(mx, pm)
    m = jnp.max(mx, axis=0, keepdims=True)

    for g in range(ng):
        cur, nxt = g % 2, 1 - g % 2
        nxt_blocks = key_blocks(g + 1) if g + 1 < ng else []
        cur_blocks = key_blocks(g)
        mx = None
        for i in range(_NBLK):
            if i < len(nxt_blocks):
                blk, kb = nxt_blocks[i]
                pm = logits_block(g + 1, blk, kb, s_refs[nxt])
                mx = pm if mx is None else jnp.maximum(mx, pm)
            if i < len(cur_blocks):
                blk, _ = cur_blocks[i]
                e = (s_refs[cur][blk * _KB:(blk + 1) * _KB, :] - m).astype(p_refs[cur].dtype)
                p_refs[cur][blk * _KB:(blk + 1) * _KB, :] = jnp.exp2(e)
        if g > 0:
            write_out(g - 1, values(g - 1, p_refs[nxt]))
        if nxt_blocks:
            m = jnp.max(mx, axis=0, keepdims=True)

    write_out(ng - 1, values(ng - 1, p_refs[(ng - 1) % 2]))


def _band_attention(kvq, bias_t):
    bsz, s, d3 = kvq.shape
    d = d3 // 3
    nh = d // HEAD_DIM
    assert s % _GROUP == 0
    f32 = jnp.float32
    bf = jnp.bfloat16
    return pl.pallas_call(
        _band_kernel,
        grid=(bsz, nh),
        in_specs=[
            pl.BlockSpec((1, s, HEAD_DIM), lambda b, h: (b, 0, 2 * nh + h)),
            pl.BlockSpec((1, s, HEAD_DIM), lambda b, h: (b, 0, h)),
            pl.BlockSpec((1, s, HEAD_DIM), lambda b, h: (b, 0, nh + h)),
            pl.BlockSpec((1, _WIN, _GROUP), lambda b, h: (h, 0, 0)),
        ],
        out_specs=pl.BlockSpec((1, s, HEAD_DIM), lambda b, h: (b, 0, h)),
        out_shape=jax.ShapeDtypeStruct((bsz, s, d), bf),
        scratch_shapes=[pltpu.VMEM((HEAD_DIM + _ONES_ROWS, s), bf),
                        pltpu.VMEM((_WIN, _GROUP), f32),
                        pltpu.VMEM((_WIN, _GROUP), f32),
                        pltpu.VMEM((_WIN, _GROUP), bf),
                        pltpu.VMEM((_WIN, _GROUP), bf)],
        compiler_params=_params(("arbitrary", "arbitrary")),
        name="band_attention",
    )(kvq, kvq, kvq, bias_t)


_BIAS_SPAN = _WIN + _GROUP


def _band_bias_kernel(g_ref, o_ref):
    row = lax.broadcasted_iota(jnp.int32, (_KB, _BIAS_SPAN), 0)
    x = jnp.broadcast_to(g_ref[0], (_KB, _BIAS_SPAN))
    shift = 1
    while shift < _KB:
        x = jnp.where((row & shift) != 0, pltpu.roll(x, shift, axis=1), x)
        shift *= 2
    for blk in range(_NBLK):
        start = (_NBLK - blk) * _KB
        b = x[:, start:start + _GROUP]
        j = lax.broadcasted_iota(jnp.int32, (_KB, _GROUP), 0) + blk * _KB
        i = lax.broadcasted_iota(jnp.int32, (_KB, _GROUP), 1)
        km = j - (i // CHUNK) * CHUNK
        o_ref[0, blk * _KB:(blk + 1) * _KB, :] = jnp.where((km >= 0) & (km < _BAND), b, -jnp.inf)


def _band_bias(rel_table):
    nh = rel_table.shape[0]
    k = jnp.arange(_BIAS_SPAN)
    idx = jnp.clip(k - _WIN + _PAD, -(CHUNK - 1), REL_CLIP) + (CHUNK - 1)
    g = (rel_table[:, idx].astype(jnp.float32) * LOG2E).reshape(nh, 1, _BIAS_SPAN)
    return pl.pallas_call(
        _band_bias_kernel,
        grid=(nh,),
        in_specs=[pl.BlockSpec((1, 1, _BIAS_SPAN), lambda h: (h, 0, 0))],
        out_specs=pl.BlockSpec((1, _WIN, _GROUP), lambda h: (h, 0, 0)),
        out_shape=jax.ShapeDtypeStruct((nh, _WIN, _GROUP), jnp.float32),
        compiler_params=_params(("arbitrary",)),
        name="band_bias",
    )(g)


def kernel(x, a_norm_g, a_w_in, a_b_f, a_q_g, a_k_g, a_w_out, mlp_norm_g, mlp_w1, mlp_w2,
           kv_norm_g, kv_w, kv_k_g, b_norm_g, b_w_q, b_q_g, b_rel, b_w_out):
    bsz, s, d = x.shape
    n = bsz * s
    nh = N_HEADS
    bf = jnp.bfloat16
    q_scale = HEAD_DIM ** -0.5 * LOG2E

    h = x.reshape(n, d)

    w_qkv, w_f = _cast_split_cols(a_w_in, 0, 3 * d)
    w1_all = mlp_w1.astype(bf)
    w2_all = mlp_w2.astype(bf)
    qkv, fz = _proj(h, [a_norm_g[0]], [w_qkv], [a_q_g[0] * q_scale, a_k_g[0]],
                    [(0, 0), (0, 1), (0, None)], w_f)
    b_f = jnp.pad(a_b_f[0], (0, LANES - nh)).reshape(1, LANES)
    o = _fox_from_fz(qkv.reshape(bsz, s, 3 * d), fz.reshape(bsz, s, LANES), b_f)
    h = _out_proj(h, o.reshape(n, d), a_w_out[0].astype(bf))
    h = _mlp(h, mlp_norm_g[0], w1_all, w2_all, 0)

    kvq = _proj(h, [kv_norm_g, b_norm_g[0]], [kv_w.astype(bf), b_w_q[0].astype(bf)],
                [kv_k_g, b_q_g[0] * q_scale],
                [(0, 0), (0, None), (1, 1)])

    o = _band_attention(kvq.reshape(bsz, s, 3 * d), _band_bias(b_rel[0]))
    h = _out_proj(h, o.reshape(n, d), b_w_out[0].astype(bf))
    h = _mlp(h, mlp_norm_g[1], w1_all, w2_all, 1)
    return h.reshape(bsz, s, d)
```

```python
import functools
import math

import numpy as np

import jax
import jax.numpy as jnp
from jax import lax
from jax.experimental import pallas as pl
from jax.experimental.pallas import tpu as pltpu

N_HEADS = 16
HEAD_DIM = 128
CHUNK = 64
N_PREV_CHUNKS = 8
REL_CLIP = 256
EPS = 1e-6
LOG2E = math.log2(math.e)

LANES = 128
VMEM_LIMIT = 60 * 1024 * 1024

_BAND = (N_PREV_CHUNKS + 1) * CHUNK
_PAD = N_PREV_CHUNKS * CHUNK
_GROUP = 4 * CHUNK
_WIN = _PAD + _GROUP


def _params(sem):
    return pltpu.CompilerParams(dimension_semantics=sem, vmem_limit_bytes=VMEM_LIMIT)


def _cast_kernel(wt_ref, tail_ref, o_ref, otail_ref, *, n_tail):
    o_ref[...] = wt_ref[...].T.astype(o_ref.dtype)
    lane = lax.broadcasted_iota(jnp.int32, otail_ref.shape, 1)
    otail_ref[...] = jnp.where(lane < n_tail, tail_ref[...].T, 0.0).astype(otail_ref.dtype)


def _cast_split_cols(w, layer, ncols, *, tc=256):
    _, rows, total = w.shape
    n_tail = total - ncols
    assert ncols % tc == 0 and tc % LANES == 0 and 0 < n_tail < LANES
    wt = jnp.swapaxes(w, 1, 2)
    return pl.pallas_call(
        functools.partial(_cast_kernel, n_tail=n_tail),
        grid=(ncols // tc,),
        in_specs=[pl.BlockSpec((None, tc, rows), lambda i: (layer, i, 0)),
                  pl.BlockSpec((None, LANES, rows), lambda i: (layer, ncols // LANES, 0))],
        out_specs=[pl.BlockSpec((rows, tc), lambda i: (0, i)),
                   pl.BlockSpec((rows, LANES), lambda i: (0, 0))],
        out_shape=[jax.ShapeDtypeStruct((rows, ncols), jnp.bfloat16),
                   jax.ShapeDtypeStruct((rows, LANES), jnp.bfloat16)],
        compiler_params=_params(("arbitrary",)),
        name="cast_cols",
    )(wt, wt)


def _proj_kernel(*refs, parts, n_w, has_f, tn):
    x_ref, g_ref, gain_ref = refs[:3]
    w_refs = refs[3:3 + n_w]
    if has_f:
        wf_ref, o_ref, f_ref, u_ref = refs[3 + n_w:]
    else:
        o_ref, u_ref = refs[3 + n_w:]
    w_parts = [(w_ref, c) for w_ref in w_refs for c in range(w_ref.shape[1] // tn)]
    n_norms = u_ref.shape[0]
    x = x_ref[...]
    xn = x * lax.rsqrt(jnp.mean(x * x, axis=-1, keepdims=True) + EPS)
    for k in range(n_norms):
        u_ref[k] = (xn * g_ref[k:k + 1, :]).astype(u_ref.dtype)
    if has_f:
        f_ref[...] = jnp.dot(u_ref[0], wf_ref[...], preferred_element_type=jnp.float32)
    order = sorted(range(len(parts)), key=lambda j: parts[j][1] is None)
    for j in order:
        norm_idx, gain_idx = parts[j]
        w_ref, c = w_parts[j]
        y = jnp.dot(u_ref[norm_idx], w_ref[:, c * tn:(c + 1) * tn],
                    preferred_element_type=jnp.float32)
        if gain_idx is None:
            o_ref[:, j * tn:(j + 1) * tn] = y.astype(o_ref.dtype)
            continue
        gain = gain_ref[gain_idx:gain_idx + 1, :]
        for c in range(tn // HEAD_DIM):
            ys = y[:, c * HEAD_DIM:(c + 1) * HEAD_DIM]
            r = lax.rsqrt(jnp.mean(ys * ys, axis=-1, keepdims=True) + EPS)
            sl = slice(j * tn + c * HEAD_DIM, j * tn + (c + 1) * HEAD_DIM)
            o_ref[:, sl] = (ys * r * gain).astype(o_ref.dtype)


def _proj(x, norm_gains, ws, head_gains, parts, wf=None, *, tm=512, tn=2048):
    n, d = x.shape
    nout = sum(w.shape[1] for w in ws)
    assert len(parts) * tn == nout and all(w.shape[1] % tn == 0 for w in ws)
    has_f = wf is not None
    n_norms = len(norm_gains)
    once = pl.Buffered(1)
    in_specs = [
        pl.BlockSpec((tm, d), lambda i: (i, 0)),
        pl.BlockSpec((n_norms, d), lambda i: (0, 0)),
        pl.BlockSpec((len(head_gains), HEAD_DIM), lambda i: (0, 0)),
    ] + [pl.BlockSpec(w.shape, lambda i: (0, 0), pipeline_mode=once) for w in ws]
    args = [x, jnp.stack(norm_gains), jnp.stack(head_gains), *ws]
    out_shape = [jax.ShapeDtypeStruct((n, nout), jnp.bfloat16)]
    out_specs = [pl.BlockSpec((tm, nout), lambda i: (i, 0))]
    if has_f:
        in_specs.append(pl.BlockSpec((d, LANES), lambda i: (0, 0), pipeline_mode=once))
        args.append(wf)
        out_shape.append(jax.ShapeDtypeStruct((n, LANES), jnp.float32))
        out_specs.append(pl.BlockSpec((tm, LANES), lambda i: (i, 0)))
    res = pl.pallas_call(
        functools.partial(_proj_kernel, parts=tuple(parts), n_w=len(ws), has_f=has_f, tn=tn),
        grid=(n // tm,),
        in_specs=in_specs,
        out_specs=out_specs,
        out_shape=out_shape,
        scratch_shapes=[pltpu.VMEM((n_norms, tm, d), jnp.bfloat16)],
        compiler_params=_params(("arbitrary",)),
        name="norm_proj",
    )(*args)
    return res if has_f else res[0]


_HEAD_LANES = LANES // N_HEADS
_N_TERMS = 3


def _bias_column_constants():
    sel = np.zeros((_N_TERMS * LANES, 2 * LANES), np.float32)
    one = np.zeros((1, 2 * LANES), np.float32)
    for h in range(N_HEADS):
        for term in range(_N_TERMS):
            sel[term * LANES + h, h * _HEAD_LANES + term] = -1.0
            sel[term * LANES + h, LANES + h * _HEAD_LANES + _N_TERMS + term] = 1.0
            one[0, h * _HEAD_LANES + _N_TERMS + term] = 1.0
            one[0, LANES + h * _HEAD_LANES + term] = 1.0
    return jnp.asarray(sel, jnp.bfloat16), jnp.asarray(one, jnp.float32)


def _logf_cumsum_kernel(f_ref, b_ref, sel_ref, one_ref, qx_ref, kx_ref):
    z = f_ref[0] + b_ref[...]
    c = jnp.minimum(z, 0.0) - jnp.log1p(jnp.exp(-jnp.abs(z)))
    s = c.shape[0]
    row = lax.broadcasted_iota(jnp.int32, c.shape, 0)
    d = 1
    while d < s:
        c = c + jnp.where(row >= d, pltpu.roll(c, d, axis=0), 0.0)
        d *= 2
    c = c * LOG2E
    hi = c.astype(jnp.bfloat16)
    r = c - hi.astype(jnp.float32)
    mid = r.astype(jnp.bfloat16)
    lo = (r - mid.astype(jnp.float32)).astype(jnp.bfloat16)
    terms = jnp.concatenate([hi, mid, lo], axis=1)
    y = jnp.dot(terms, sel_ref[...], preferred_element_type=jnp.float32) + one_ref[...]
    kx_ref[0] = y[:, :LANES].astype(kx_ref.dtype)
    qx_ref[0] = y[:, LANES:].astype(qx_ref.dtype)


def _logf_cumsum(fz, b):
    bsz, s, _ = fz.shape
    sel, one = _bias_column_constants()
    spec = pl.BlockSpec((1, s, LANES), lambda i: (i, 0, 0))
    out = jax.ShapeDtypeStruct((bsz, s, LANES), jnp.bfloat16)
    return pl.pallas_call(
        _logf_cumsum_kernel,
        grid=(bsz,),
        in_specs=[spec, pl.BlockSpec((1, LANES), lambda i: (0, 0)),
                  pl.BlockSpec(sel.shape, lambda i: (0, 0)),
                  pl.BlockSpec(one.shape, lambda i: (0, 0))],
        out_specs=[spec, spec],
        out_shape=[out, out],
        compiler_params=_params(("arbitrary",)),
        name="logf_cumsum",
    )(fz, b, sel, one)


_NT = (((1,), (1,)), ((), ()))


def _fox_kernel(q_ref, qx_ref, k_ref, kx_ref, v_ref, o_ref,
                qa_ref, ka_ref, vt_ref, s0_ref, s1_ref, p0_ref, p1_ref, acc_ref, *, t, tc, rb):
    nt = ka_ref.shape[0]
    nc = t // tc
    nr = t // rb
    f32 = jnp.float32
    s_refs, p_refs = (s0_ref, s1_ref), (p0_ref, p1_ref)
    tiles = [(qi, kv) for qi in range(nt) for kv in range(qi + 1)]

    lane = lax.broadcasted_iota(jnp.int32, (t, LANES), 1)
    own_lanes = lane // _HEAD_LANES == pl.program_id(1)

    for j in range(nt):
        rows = slice(j * t, (j + 1) * t)
        ka_ref[j, :, 0:HEAD_DIM] = k_ref[0, rows, :]
        ka_ref[j, :, HEAD_DIM:] = kx_ref[0, rows, :]
        qa_ref[j, :, 0:HEAD_DIM] = q_ref[0, rows, :]
        qx = qx_ref[0, rows, :]
        qa_ref[j, :, HEAD_DIM:] = jnp.where(own_lanes, qx, jnp.zeros_like(qx))
        vt_ref[:, rows] = v_ref[0, rows, :].T

    def fold8(x, op):
        return op(x.reshape(x.shape[0] // 8, 8, x.shape[1]), axis=0)

    def live(tile, r, c):
        return tile[1] < tile[0] or r * rb < (c + 1) * tc

    def key_rows(tile, c):
        return t if tile[1] < tile[0] else min(t, -(-(c + 1) * tc // rb) * rb)

    def logits_block(tile, r, c, s_out):
        qi, kv = tile
        s = lax.dot_general(ka_ref[kv, r * rb:(r + 1) * rb, :],
                            qa_ref[qi, c * tc:(c + 1) * tc, :], _NT,
                            preferred_element_type=f32)
        if kv == qi and (r + 1) * rb > c * tc:
            key = lax.broadcasted_iota(jnp.int32, s.shape, 0) + r * rb
            qry = lax.broadcasted_iota(jnp.int32, s.shape, 1) + c * tc
            s = jnp.where(key <= qry, s, -jnp.inf)
        s_out[r * rb:(r + 1) * rb, c * tc:(c + 1) * tc] = s
        return fold8(s, jnp.max)

    blocks = [(r, c) for r in range(nr) for c in range(nc)]

    def tile_max(mx):
        return [jnp.max(mx[c], axis=0, keepdims=True) for c in range(nc)]

    mx = [None] * nc
    for r, c in blocks:
        if not live(tiles[0], r, c):
            continue
        pm = logits_block(tiles[0], r, c, s_refs[0])
        mx[c] = pm if mx[c] is None else jnp.maximum(mx[c], pm)
    mc = tile_max(mx)

    m_run = l_run = alpha_prev = None
    for n, (qi, kv) in enumerate(tiles):
        cur, nxt = n % 2, 1 - n % 2
        nxt_tile = tiles[n + 1] if n + 1 < len(tiles) else None
        prv_tile = tiles[n - 1] if n > 0 else None
        if kv == 0:
            m_new = mc
            alpha = None
        else:
            m_new = [jnp.maximum(m_run[c], mc[c]) for c in range(nc)]
            alpha = [jnp.exp2(m_run[c] - m_new[c]) for c in range(nc)]
        mx = [None] * nc
        ls = [None] * nc
        pv = [None] * nc
        for r, c in blocks:
            rows = slice(r * rb, (r + 1) * rb)
            cols = slice(c * tc, (c + 1) * tc)
            if nxt_tile is not None and live(nxt_tile, r, c):
                pm = logits_block(nxt_tile, r, c, s_refs[nxt])
                mx[c] = pm if mx[c] is None else jnp.maximum(mx[c], pm)
            if live((qi, kv), r, c):
                p = jnp.exp2(s_refs[cur][rows, cols] - m_new[c])
                ps = fold8(p, jnp.sum)
                ls[c] = ps if ls[c] is None else ls[c] + ps
                p_refs[cur][rows, cols] = p.astype(p_refs[cur].dtype)
            if prv_tile is not None and r == nr - 1:
                nk = key_rows(prv_tile, c)
                k0 = prv_tile[1] * t
                pv[c] = jnp.dot(vt_ref[:, k0:k0 + nk], p_refs[nxt][0:nk, cols],
                                preferred_element_type=f32)
        lsum = [jnp.sum(ls[c], axis=0, keepdims=True) for c in range(nc)]
        l_prev_block = l_run
        l_run = lsum if kv == 0 else [alpha[c] * l_run[c] + lsum[c] for c in range(nc)]
        if prv_tile is not None:
            for c in range(nc):
                cols = slice(c * tc, (c + 1) * tc)
                if prv_tile[1] == 0:
                    acc_ref[:, cols] = pv[c]
                else:
                    acc_ref[:, cols] = alpha_prev[c] * acc_ref[:, cols] + pv[c]
            if prv_tile[1] == prv_tile[0]:
                l_fin = jnp.concatenate(l_prev_block, axis=1)
                o_ref[0, prv_tile[0] * t:(prv_tile[0] + 1) * t, :] = (
                    acc_ref[...] / l_fin).T.astype(o_ref.dtype)
        m_run, alpha_prev = m_new, alpha
        if nxt_tile is not None:
            mc = tile_max(mx)

    last = len(tiles) - 1
    qi, kv = tiles[last]
    for c in range(nc):
        cols = slice(c * tc, (c + 1) * tc)
        nk = key_rows(tiles[last], c)
        d = jnp.dot(vt_ref[:, kv * t:kv * t + nk], p_refs[last % 2][0:nk, cols],
                    preferred_element_type=f32)
        acc_ref[:, cols] = d if kv == 0 else alpha_prev[c] * acc_ref[:, cols] + d
    o_ref[0, qi * t:(qi + 1) * t, :] = (
        acc_ref[...] / jnp.concatenate(l_run, axis=1)).T.astype(o_ref.dtype)


def _fox_attention(qkv, qx, kx, *, t=512, tc=256, rb=256):
    bsz, s, d3 = qkv.shape
    d = d3 // 3
    nh = d // HEAD_DIM
    nt = s // t
    f32 = jnp.float32
    bf = jnp.bfloat16
    return pl.pallas_call(
        functools.partial(_fox_kernel, t=t, tc=tc, rb=rb),
        grid=(bsz, nh),
        in_specs=[
            pl.BlockSpec((1, s, HEAD_DIM), lambda b, h: (b, 0, h)),
            pl.BlockSpec((1, s, LANES), lambda b, h: (b, 0, 0)),
            pl.BlockSpec((1, s, HEAD_DIM), lambda b, h: (b, 0, nh + h)),
            pl.BlockSpec((1, s, LANES), lambda b, h: (b, 0, 0)),
            pl.BlockSpec((1, s, HEAD_DIM), lambda b, h: (b, 0, 2 * nh + h)),
        ],
        out_specs=pl.BlockSpec((1, s, HEAD_DIM), lambda b, h: (b, 0, h)),
        out_shape=jax.ShapeDtypeStruct((bsz, s, d), bf),
        scratch_shapes=[pltpu.VMEM((nt, t, HEAD_DIM + LANES), bf),
                        pltpu.VMEM((nt, t, HEAD_DIM + LANES), bf),
                        pltpu.VMEM((HEAD_DIM, s), bf),
                        pltpu.VMEM((t, t), f32),
                        pltpu.VMEM((t, t), f32),
                        pltpu.VMEM((t, t), bf),
                        pltpu.VMEM((t, t), bf),
                        pltpu.VMEM((HEAD_DIM, t), f32)],
        compiler_params=_params(("arbitrary", "arbitrary")),
        name="fox_attention",
    )(qkv, qx, qkv, kx, qkv)


def _fox_from_fz(qkv, fz, b_f):
    qx, kx = _logf_cumsum(fz, b_f)
    return _fox_attention(qkv, qx, kx)


def _out_proj_kernel(h_ref, o_ref, w_ref, out_ref):
    out_ref[...] = h_ref[...] + jnp.dot(o_ref[...], w_ref[...],
                                        preferred_element_type=jnp.float32)


def _out_proj(h, o, w, *, tm=512):
    n, d = h.shape
    return pl.pallas_call(
        _out_proj_kernel,
        grid=(n // tm,),
        in_specs=[pl.BlockSpec((tm, d), lambda i: (i, 0)),
                  pl.BlockSpec((tm, d), lambda i: (i, 0)),
                  pl.BlockSpec((d, d), lambda i: (0, 0))],
        out_specs=pl.BlockSpec((tm, d), lambda i: (i, 0)),
        out_shape=jax.ShapeDtypeStruct((n, d), jnp.float32),
        compiler_params=_params(("arbitrary",)),
        name="out_proj",
    )(h, o, w)


def _mlp_kernel(h_ref, g_ref, w1_ref, w2_ref, out_ref, u_ref):
    f = pl.program_id(1)

    @pl.when(f == 0)
    def _():
        x = h_ref[...]
        r = lax.rsqrt(jnp.mean(x * x, axis=-1, keepdims=True) + EPS)
        u_ref[...] = (x * r * g_ref[...]).astype(u_ref.dtype)
        out_ref[...] = x

    a = jnp.dot(u_ref[...], w1_ref[...], preferred_element_type=jnp.float32)
    a = jnp.maximum(a, 0.0)
    a = (a * a).astype(w2_ref.dtype)
    out_ref[...] += jnp.dot(a, w2_ref[...], preferred_element_type=jnp.float32)


def _mlp(h, g, w1, w2, layer, *, tm=512, tf=2048):
    n, d = h.shape
    dff = w1.shape[2]
    return pl.pallas_call(
        _mlp_kernel,
        grid=(n // tm, dff // tf),
        in_specs=[pl.BlockSpec((tm, d), lambda i, f: (i, 0)),
                  pl.BlockSpec((1, d), lambda i, f: (0, 0)),
                  pl.BlockSpec((None, d, tf), lambda i, f: (layer, 0, f)),
                  pl.BlockSpec((None, tf, d), lambda i, f: (layer, f, 0))],
        out_specs=pl.BlockSpec((tm, d), lambda i, f: (i, 0)),
        out_shape=jax.ShapeDtypeStruct((n, d), jnp.float32),
        scratch_shapes=[pltpu.VMEM((tm, d), jnp.bfloat16)],
        compiler_params=_params(("arbitrary", "arbitrary")),
        name="sq_relu_mlp",
    )(h, g.reshape(1, d), w1, w2)


_KB = _GROUP
_NBLK = _WIN // _KB
_ONES_ROWS = 16


def _band_kernel(q_ref, k_ref, v_ref, bias_ref, o_ref, vt_ref, s0_ref, s1_ref, p0_ref, p1_ref):
    ng = vt_ref.shape[1] // _GROUP
    f32 = jnp.float32
    s_refs, p_refs = (s0_ref, s1_ref), (p0_ref, p1_ref)

    for j in range(ng):
        vt_ref[0:HEAD_DIM, j * _KB:(j + 1) * _KB] = v_ref[0, j * _KB:(j + 1) * _KB, :].T
    sub = lax.broadcasted_iota(jnp.int32, (_ONES_ROWS, vt_ref.shape[1]), 0)
    vt_ref[HEAD_DIM:, :] = jnp.where(sub == 0, 1.0, 0.0).astype(vt_ref.dtype)

    def fold8(x, op):
        return op(x.reshape(x.shape[0] // 8, 8, x.shape[1]), axis=0)

    def rows_of(g):
        return slice(g * _GROUP, (g + 1) * _GROUP)

    def key_blocks(g):
        return [(blk, g - (_NBLK - 1) + blk) for blk in range(_NBLK) if g - (_NBLK - 1) + blk >= 0]

    def logits_block(g, blk, kb, s_out):
        s = lax.dot_general(k_ref[0, rows_of(kb), :], q_ref[0, rows_of(g), :], _NT,
                            preferred_element_type=f32) + bias_ref[0, blk * _KB:(blk + 1) * _KB, :]
        s_out[blk * _KB:(blk + 1) * _KB, :] = s
        return fold8(s, jnp.max)

    def values(g, p_ref):
        blks = key_blocks(g)
        (b0, k0), nb = blks[0], len(blks)
        return jnp.dot(vt_ref[:, k0 * _KB:(k0 + nb) * _KB], p_ref[b0 * _KB:(b0 + nb) * _KB, :],
                       preferred_element_type=f32)

    def write_out(g, pv):
        o = pv[0:HEAD_DIM] / pv[HEAD_DIM:HEAD_DIM + 1]
        o_ref[0, rows_of(g), :] = o.T.astype(o_ref.dtype)

    mx = None
    for blk, kb in key_blocks(0):
        pm = logits_block(0, blk, kb, s_refs[0])
        mx = pm if mx is None else jnp.maximum(mx, pm)
    m = jnp.max(mx, axis=0, keepdims=True)

    for g in range(ng):
        cur, nxt = g % 2, 1 - g % 2
        nxt_blocks = key_blocks(g + 1) if g + 1 < ng else []
        cur_blocks = key_blocks(g)
        mx = None
        for i in range(_NBLK):
            if i < len(nxt_blocks):
                blk, kb = nxt_blocks[i]
                pm = logits_block(g + 1, blk, kb, s_refs[nxt])
                mx = pm if mx is None else jnp.maximum(mx, pm)
            if i < len(cur_blocks):
                blk, _ = cur_blocks[i]
                e = (s_refs[cur][blk * _KB:(blk + 1) * _KB, :] - m).astype(p_refs[cur].dtype)
                p_refs[cur][blk * _KB:(blk + 1) * _KB, :] = jnp.exp2(e)
        if g > 0:
            write_out(g - 1, values(g - 1, p_refs[nxt]))
        if nxt_blocks:
            m = jnp.max(mx, axis=0, keepdims=True)

    write_out(ng - 1, values(ng - 1, p_refs[(ng - 1) % 2]))


def _band_attention(kvq, bias_t):
    bsz, s, d3 = kvq.shape
    d = d3 // 3
    nh = d // HEAD_DIM
    assert s % _GROUP == 0
    f32 = jnp.float32
    bf = jnp.bfloat16
    return pl.pallas_call(
        _band_kernel,
        grid=(bsz, nh),
        in_specs=[
            pl.BlockSpec((1, s, HEAD_DIM), lambda b, h: (b, 0, 2 * nh + h)),
            pl.BlockSpec((1, s, HEAD_DIM), lambda b, h: (b, 0, h)),
            pl.BlockSpec((1, s, HEAD_DIM), lambda b, h: (b, 0, nh + h)),
            pl.BlockSpec((1, _WIN, _GROUP), lambda b, h: (h, 0, 0)),
        ],
        out_specs=pl.BlockSpec((1, s, HEAD_DIM), lambda b, h: (b, 0, h)),
        out_shape=jax.ShapeDtypeStruct((bsz, s, d), bf),
        scratch_shapes=[pltpu.VMEM((HEAD_DIM + _ONES_ROWS, s), bf),
                        pltpu.VMEM((_WIN, _GROUP), f32),
                        pltpu.VMEM((_WIN, _GROUP), f32),
                        pltpu.VMEM((_WIN, _GROUP), bf),
                        pltpu.VMEM((_WIN, _GROUP), bf)],
        compiler_params=_params(("arbitrary", "arbitrary")),
        name="band_attention",
    )(kvq, kvq, kvq, bias_t)


_BIAS_SPAN = _WIN + _GROUP


def _band_bias_kernel(g_ref, o_ref):
    row = lax.broadcasted_iota(jnp.int32, (_KB, _BIAS_SPAN), 0)
    x = jnp.broadcast_to(g_ref[0], (_KB, _BIAS_SPAN))
    shift = 1
    while shift < _KB:
        x = jnp.where((row & shift) != 0, pltpu.roll(x, shift, axis=1), x)
        shift *= 2
    for blk in range(_NBLK):
        start = (_NBLK - blk) * _KB
        b = x[:, start:start + _GROUP]
        j = lax.broadcasted_iota(jnp.int32, (_KB, _GROUP), 0) + blk * _KB
        i = lax.broadcasted_iota(jnp.int32, (_KB, _GROUP), 1)
        km = j - (i // CHUNK) * CHUNK
        o_ref[0, blk * _KB:(blk + 1) * _KB, :] = jnp.where((km >= 0) & (km < _BAND), b, -jnp.inf)


def _band_bias(rel_table):
    nh = rel_table.shape[0]
    k = jnp.arange(_BIAS_SPAN)
    idx = jnp.clip(k - _WIN + _PAD, -(CHUNK - 1), REL_CLIP) + (CHUNK - 1)
    g = (rel_table[:, idx].astype(jnp.float32) * LOG2E).reshape(nh, 1, _BIAS_SPAN)
    return pl.pallas_call(
        _band_bias_kernel,
        grid=(nh,),
        in_specs=[pl.BlockSpec((1, 1, _BIAS_SPAN), lambda h: (h, 0, 0))],
        out_specs=pl.BlockSpec((1, _WIN, _GROUP), lambda h: (h, 0, 0)),
        out_shape=jax.ShapeDtypeStruct((nh, _WIN, _GROUP), jnp.float32),
        compiler_params=_params(("arbitrary",)),
        name="band_bias",
    )(g)


def kernel(x, a_norm_g, a_w_in, a_b_f, a_q_g, a_k_g, a_w_out, mlp_norm_g, mlp_w1, mlp_w2,
           kv_norm_g, kv_w, kv_k_g, b_norm_g, b_w_q, b_q_g, b_rel, b_w_out):
    bsz, s, d = x.shape
    n = bsz * s
    nh = N_HEADS
    bf = jnp.bfloat16
    q_scale = HEAD_DIM ** -0.5 * LOG2E

    h = x.reshape(n, d)

    w_qkv, w_f = _cast_split_cols(a_w_in, 0, 3 * d)
    w1_all = mlp_w1.astype(bf)
    w2_all = mlp_w2.astype(bf)
    qkv, fz = _proj(h, [a_norm_g[0]], [w_qkv], [a_q_g[0] * q_scale, a_k_g[0]],
                    [(0, 0), (0, 1), (0, None)], w_f)
    b_f = jnp.pad(a_b_f[0], (0, LANES - nh)).reshape(1, LANES)
    o = _fox_from_fz(qkv.reshape(bsz, s, 3 * d), fz.reshape(bsz, s, LANES), b_f)
    h = _out_proj(h, o.reshape(n, d), a_w_out[0].astype(bf))
    h = _mlp(h, mlp_norm_g[0], w1_all, w2_all, 0)

    kvq = _proj(h, [kv_norm_g, b_norm_g[0]], [kv_w.astype(bf), b_w_q[0].astype(bf)],
                [kv_k_g, b_q_g[0] * q_scale],
                [(0, 0), (0, None), (1, 1)])

    o = _band_attention(kvq.reshape(bsz, s, 3 * d), _band_bias(b_rel[0]))
    h = _out_proj(h, o.reshape(n, d), b_w_out[0].astype(bf))
    h = _mlp(h, mlp_norm_g[1], w1_all, w2_all, 1)
    return h.reshape(bsz, s, d)
```

```python
import functools
import math

import numpy as np

import jax
import jax.numpy as jnp
from jax import lax
from jax.experimental import pallas as pl
from jax.experimental.pallas import tpu as pltpu

N_HEADS = 16
HEAD_DIM = 128
CHUNK = 64
N_PREV_CHUNKS = 8
REL_CLIP = 256
EPS = 1e-6
LOG2E = math.log2(math.e)

LANES = 128
VMEM_LIMIT = 60 * 1024 * 1024

_BAND = (N_PREV_CHUNKS + 1) * CHUNK
_PAD = N_PREV_CHUNKS * CHUNK
_GROUP = 4 * CHUNK
_WIN = _PAD + _GROUP


def _params(sem):
    return pltpu.CompilerParams(dimension_semantics=sem, vmem_limit_bytes=VMEM_LIMIT)


def _cast_kernel(wt_ref, tail_ref, o_ref, otail_ref, *, n_tail):
    o_ref[...] = wt_ref[...].T.astype(o_ref.dtype)
    lane = lax.broadcasted_iota(jnp.int32, otail_ref.shape, 1)
    otail_ref[...] = jnp.where(lane < n_tail, tail_ref[...].T, 0.0).astype(otail_ref.dtype)


def _cast_split_cols(w, layer, ncols, *, tc=256):
    _, rows, total = w.shape
    n_tail = total - ncols
    assert ncols % tc == 0 and tc % LANES == 0 and 0 < n_tail < LANES
    wt = jnp.swapaxes(w, 1, 2)
    return pl.pallas_call(
        functools.partial(_cast_kernel, n_tail=n_tail),
        grid=(ncols // tc,),
        in_specs=[pl.BlockSpec((None, tc, rows), lambda i: (layer, i, 0)),
                  pl.BlockSpec((None, LANES, rows), lambda i: (layer, ncols // LANES, 0))],
        out_specs=[pl.BlockSpec((rows, tc), lambda i: (0, i)),
                   pl.BlockSpec((rows, LANES), lambda i: (0, 0))],
        out_shape=[jax.ShapeDtypeStruct((rows, ncols), jnp.bfloat16),
                   jax.ShapeDtypeStruct((rows, LANES), jnp.bfloat16)],
        compiler_params=_params(("arbitrary",)),
        name="cast_cols",
    )(wt, wt)


def _proj_kernel(*refs, parts, n_w, has_f, tn):
    x_ref, g_ref, gain_ref = refs[:3]
    w_refs = refs[3:3 + n_w]
    if has_f:
        wf_ref, o_ref, f_ref, u_ref = refs[3 + n_w:]
    else:
        o_ref, u_ref = refs[3 + n_w:]
    w_parts = [(w_ref, c) for w_ref in w_refs for c in range(w_ref.shape[1] // tn)]
    n_norms = u_ref.shape[0]
    x = x_ref[...]
    xn = x * lax.rsqrt(jnp.mean(x * x, axis=-1, keepdims=True) + EPS)
    for k in range(n_norms):
        u_ref[k] = (xn * g_ref[k:k + 1, :]).astype(u_ref.dtype)
    if has_f:
        f_ref[...] = jnp.dot(u_ref[0], wf_ref[...], preferred_element_type=jnp.float32)
    order = sorted(range(len(parts)), key=lambda j: parts[j][1] is None)
    for j in order:
        norm_idx, gain_idx = parts[j]
        w_ref, c = w_parts[j]
        y = jnp.dot(u_ref[norm_idx], w_ref[:, c * tn:(c + 1) * tn],
                    preferred_element_type=jnp.float32)
        if gain_idx is None:
            o_ref[:, j * tn:(j + 1) * tn] = y.astype(o_ref.dtype)
            continue
        gain = gain_ref[gain_idx:gain_idx + 1, :]
        for c in range(tn // HEAD_DIM):
            ys = y[:, c * HEAD_DIM:(c + 1) * HEAD_DIM]
            r = lax.rsqrt(jnp.mean(ys * ys, axis=-1, keepdims=True) + EPS)
            sl = slice(j * tn + c * HEAD_DIM, j * tn + (c + 1) * HEAD_DIM)
            o_ref[:, sl] = (ys * r * gain).astype(o_ref.dtype)


def _proj(x, norm_gains, ws, head_gains, parts, wf=None, *, tm=512, tn=2048):
    n, d = x.shape
    nout = sum(w.shape[1] for w in ws)
    assert len(parts) * tn == nout and all(w.shape[1] % tn == 0 for w in ws)
    has_f = wf is not None
    n_norms = len(norm_gains)
    once = pl.Buffered(1)
    in_specs = [
        pl.BlockSpec((tm, d), lambda i: (i, 0)),
        pl.BlockSpec((n_norms, d), lambda i: (0, 0)),
        pl.BlockSpec((len(head_gains), HEAD_DIM), lambda i: (0, 0)),
    ] + [pl.BlockSpec(w.shape, lambda i: (0, 0), pipeline_mode=once) for w in ws]
    args = [x, jnp.stack(norm_gains), jnp.stack(head_gains), *ws]
    out_shape = [jax.ShapeDtypeStruct((n, nout), jnp.bfloat16)]
    out_specs = [pl.BlockSpec((tm, nout), lambda i: (i, 0))]
    if has_f:
        in_specs.append(pl.BlockSpec((d, LANES), lambda i: (0, 0), pipeline_mode=once))
        args.append(wf)
        out_shape.append(jax.ShapeDtypeStruct((n, LANES), jnp.float32))
        out_specs.append(pl.BlockSpec((tm, LANES), lambda i: (i, 0)))
    res = pl.pallas_call(
        functools.partial(_proj_kernel, parts=tuple(parts), n_w=len(ws), has_f=has_f, tn=tn),
        grid=(n // tm,),
        in_specs=in_specs,
        out_specs=out_specs,
        out_shape=out_shape,
        scratch_shapes=[pltpu.VMEM((n_norms, tm, d), jnp.bfloat16)],
        compiler_params=_params(("arbitrary",)),
        name="norm_proj",
    )(*args)
    return res if has_f else res[0]


_HEAD_LANES = LANES // N_HEADS
_N_TERMS = 3


def _bias_column_constants():
    sel = np.zeros((_N_TERMS * LANES, 2 * LANES), np.float32)
    one = np.zeros((1, 2 * LANES), np.float32)
    for h in range(N_HEADS):
        for term in range(_N_TERMS):
            sel[term * LANES + h, h * _HEAD_LANES + term] = -1.0
            sel[term * LANES + h, LANES + h * _HEAD_LANES + _N_TERMS + term] = 1.0
            one[0, h * _HEAD_LANES + _N_TERMS + term] = 1.0
            one[0, LANES + h * _HEAD_LANES + term] = 1.0
    return jnp.asarray(sel, jnp.bfloat16), jnp.asarray(one, jnp.float32)


def _logf_cumsum_kernel(f_ref, b_ref, sel_ref, one_ref, qx_ref, kx_ref):
    z = f_ref[0] + b_ref[...]
    c = jnp.minimum(z, 0.0) - jnp.log1p(jnp.exp(-jnp.abs(z)))
    s = c.shape[0]
    row = lax.broadcasted_iota(jnp.int32, c.shape, 0)
    d = 1
    while d < s:
        c = c + jnp.where(row >= d, pltpu.roll(c, d, axis=0), 0.0)
        d *= 2
    c = c * LOG2E
    hi = c.astype(jnp.bfloat16)
    r = c - hi.astype(jnp.float32)
    mid = r.astype(jnp.bfloat16)
    lo = (r - mid.astype(jnp.float32)).astype(jnp.bfloat16)
    terms = jnp.concatenate([hi, mid, lo], axis=1)
    y = jnp.dot(terms, sel_ref[...], preferred_element_type=jnp.float32) + one_ref[...]
    kx_ref[0] = y[:, :LANES].astype(kx_ref.dtype)
    qx_ref[0] = y[:, LANES:].astype(qx_ref.dtype)


def _logf_cumsum(fz, b):
    bsz, s, _ = fz.shape
    sel, one = _bias_column_constants()
    spec = pl.BlockSpec((1, s, LANES), lambda i: (i, 0, 0))
    out = jax.ShapeDtypeStruct((bsz, s, LANES), jnp.bfloat16)
    return pl.pallas_call(
        _logf_cumsum_kernel,
        grid=(bsz,),
        in_specs=[spec, pl.BlockSpec((1, LANES), lambda i: (0, 0)),
                  pl.BlockSpec(sel.shape, lambda i: (0, 0)),
                  pl.BlockSpec(one.shape, lambda i: (0, 0))],
        out_specs=[spec, spec],
        out_shape=[out, out],
        compiler_params=_params(("arbitrary",)),
        name="logf_cumsum",
    )(fz, b, sel, one)


_NT = (((1,), (1,)), ((), ()))


def _fox_kernel(q_ref, qx_ref, k_ref, kx_ref, v_ref, o_ref,
                qa_ref, ka_ref, vt_ref, s0_ref, s1_ref, p0_ref, p1_ref, acc_ref, *, t, tc, rb):
    nt = ka_ref.shape[0]
    nc = t // tc
    nr = t // rb
    f32 = jnp.float32
    s_refs, p_refs = (s0_ref, s1_ref), (p0_ref, p1_ref)
    tiles = [(qi, kv) for qi in range(nt) for kv in range(qi + 1)]

    lane = lax.broadcasted_iota(jnp.int32, (t, LANES), 1)
    own_lanes = lane // _HEAD_LANES == pl.program_id(1)

    for j in range(nt):
        rows = slice(j * t, (j + 1) * t)
        ka_ref[j, :, 0:HEAD_DIM] = k_ref[0, rows, :]
        ka_ref[j, :, HEAD_DIM:] = kx_ref[0, rows, :]
        qa_ref[j, :, 0:HEAD_DIM] = q_ref[0, rows, :]
        qx = qx_ref[0, rows, :]
        qa_ref[j, :, HEAD_DIM:] = jnp.where(own_lanes, qx, jnp.zeros_like(qx))
        vt_ref[:, rows] = v_ref[0, rows, :].T

    def fold8(x, op):
        return op(x.reshape(x.shape[0] // 8, 8, x.shape[1]), axis=0)

    def live(tile, r, c):
        return tile[1] < tile[0] or r * rb < (c + 1) * tc

    def key_rows(tile, c):
        return t if tile[1] < tile[0] else min(t, -(-(c + 1) * tc // rb) * rb)

    def logits_block(tile, r, c, s_out):
        qi, kv = tile
        s = lax.dot_general(ka_ref[kv, r * rb:(r + 1) * rb, :],
                            qa_ref[qi, c * tc:(c + 1) * tc, :], _NT,
                            preferred_element_type=f32)
        if kv == qi and (r + 1) * rb > c * tc:
            key = lax.broadcasted_iota(jnp.int32, s.shape, 0) + r * rb
            qry = lax.broadcasted_iota(jnp.int32, s.shape, 1) + c * tc
            s = jnp.where(key <= qry, s, -jnp.inf)
        s_out[r * rb:(r + 1) * rb, c * tc:(c + 1) * tc] = s
        return fold8(s, jnp.max)

    blocks = [(r, c) for r in range(nr) for c in range(nc)]

    def tile_max(mx):
        return [jnp.max(mx[c], axis=0, keepdims=True) for c in range(nc)]

    mx = [None] * nc
    for r, c in blocks:
        if not live(tiles[0], r, c):
            continue
        pm = logits_block(tiles[0], r, c, s_refs[0])
        mx[c] = pm if mx[c] is None else jnp.maximum(mx[c], pm)
    mc = tile_max(mx)

    m_run = l_run = alpha_prev = None
    for n, (qi, kv) in enumerate(tiles):
        cur, nxt = n % 2, 1 - n % 2
        nxt_tile = tiles[n + 1] if n + 1 < len(tiles) else None
        prv_tile = tiles[n - 1] if n > 0 else None
        if kv == 0:
            m_new = mc
            alpha = None
        else:
            m_new = [jnp.maximum(m_run[c], mc[c]) for c in range(nc)]
            alpha = [jnp.exp2(m_run[c] - m_new[c]) for c in range(nc)]
        mx = [None] * nc
        ls = [None] * nc
        pv = [None] * nc
        for r, c in blocks:
            rows = slice(r * rb, (r + 1) * rb)
            cols = slice(c * tc, (c + 1) * tc)
            if nxt_tile is not None and live(nxt_tile, r, c):
                pm = logits_block(nxt_tile, r, c, s_refs[nxt])
                mx[c] = pm if mx[c] is None else jnp.maximum(mx[c], pm)
            if live((qi, kv), r, c):
                p = jnp.exp2(s_refs[cur][rows, cols] - m_new[c])
                ps = fold8(p, jnp.sum)
                ls[c] = ps if ls[c] is None else ls[c] + ps
                p_refs[cur][rows, cols] = p.astype(p_refs[cur].dtype)
            if prv_tile is not None and r == nr - 1:
                nk = key_rows(prv_tile, c)
                k0 = prv_tile[1] * t
                pv[c] = jnp.dot(vt_ref[:, k0:k0 + nk], p_refs[nxt][0:nk, cols],
                                preferred_element_type=f32)
        lsum = [jnp.sum(ls[c], axis=0, keepdims=True) for c in range(nc)]
        l_prev_block = l_run
        l_run = lsum if kv == 0 else [alpha[c] * l_run[c] + lsum[c] for c in range(nc)]
        if prv_tile is not None:
            for c in range(nc):
                cols = slice(c * tc, (c + 1) * tc)
                if prv_tile[1] == 0:
                    acc_ref[:, cols] = pv[c]
                else:
                    acc_ref[:, cols] = alpha_prev[c] * acc_ref[:, cols] + pv[c]
            if prv_tile[1] == prv_tile[0]:
                l_fin = jnp.concatenate(l_prev_block, axis=1)
                o_ref[0, prv_tile[0] * t:(prv_tile[0] + 1) * t, :] = (
                    acc_ref[...] / l_fin).T.astype(o_ref.dtype)
        m_run, alpha_prev = m_new, alpha
        if nxt_tile is not None:
            mc = tile_max(mx)

    last = len(tiles) - 1
    qi, kv = tiles[last]
    for c in range(nc):
        cols = slice(c * tc, (c + 1) * tc)
        nk = key_rows(tiles[last], c)
        d = jnp.dot(vt_ref[:, kv * t:kv * t + nk], p_refs[last % 2][0:nk, cols],
                    preferred_element_type=f32)
        acc_ref[:, cols] = d if kv == 0 else alpha_prev[c] * acc_ref[:, cols] + d
    o_ref[0, qi * t:(qi + 1) * t, :] = (
        acc_ref[...] / jnp.concatenate(l_run, axis=1)).T.astype(o_ref.dtype)


def _fox_attention(qkv, qx, kx, *, t=512, tc=256, rb=256):
    bsz, s, d3 = qkv.shape
    d = d3 // 3
    nh = d // HEAD_DIM
    nt = s // t
    f32 = jnp.float32
    bf = jnp.bfloat16
    return pl.pallas_call(
        functools.partial(_fox_kernel, t=t, tc=tc, rb=rb),
        grid=(bsz, nh),
        in_specs=[
            pl.BlockSpec((1, s, HEAD_DIM), lambda b, h: (b, 0, h)),
            pl.BlockSpec((1, s, LANES), lambda b, h: (b, 0, 0)),
            pl.BlockSpec((1, s, HEAD_DIM), lambda b, h: (b, 0, nh + h)),
            pl.BlockSpec((1, s, LANES), lambda b, h: (b, 0, 0)),
            pl.BlockSpec((1, s, HEAD_DIM), lambda b, h: (b, 0, 2 * nh + h)),
        ],
        out_specs=pl.BlockSpec((1, s, HEAD_DIM), lambda b, h: (b, 0, h)),
        out_shape=jax.ShapeDtypeStruct((bsz, s, d), bf),
        scratch_shapes=[pltpu.VMEM((nt, t, HEAD_DIM + LANES), bf),
                        pltpu.VMEM((nt, t, HEAD_DIM + LANES), bf),
                        pltpu.VMEM((HEAD_DIM, s), bf),
                        pltpu.VMEM((t, t), f32),
                        pltpu.VMEM((t, t), f32),
                        pltpu.VMEM((t, t), bf),
                        pltpu.VMEM((t, t), bf),
                        pltpu.VMEM((HEAD_DIM, t), f32)],
        compiler_params=_params(("arbitrary", "arbitrary")),
        name="fox_attention",
    )(qkv, qx, qkv, kx, qkv)


def _fox_from_fz(qkv, fz, b_f):
    qx, kx = _logf_cumsum(fz, b_f)
    return _fox_attention(qkv, qx, kx)


def _out_proj_kernel(h_ref, o_ref, w_ref, out_ref):
    out_ref[...] = h_ref[...] + jnp.dot(o_ref[...], w_ref[...],
                                        preferred_element_type=jnp.float32)


def _out_proj(h, o, w, *, tm=512):
    n, d = h.shape
    return pl.pallas_call(
        _out_proj_kernel,
        grid=(n // tm,),
        in_specs=[pl.BlockSpec((tm, d), lambda i: (i, 0)),
                  pl.BlockSpec((tm, d), lambda i: (i, 0)),
                  pl.BlockSpec((d, d), lambda i: (0, 0))],
        out_specs=pl.BlockSpec((tm, d), lambda i: (i, 0)),
        out_shape=jax.ShapeDtypeStruct((n, d), jnp.float32),
        compiler_params=_params(("arbitrary",)),
        name="out_proj",
    )(h, o, w)


def _mlp_kernel(h_ref, g_ref, w1_ref, w2_ref, out_ref, u_ref):
    f = pl.program_id(1)

    def hidden():
        a = jnp.dot(u_ref[...], w1_ref[...], preferred_element_type=jnp.float32)
        a = jnp.maximum(a, 0.0)
        a = (a * a).astype(w2_ref.dtype)
        return jnp.dot(a, w2_ref[...], preferred_element_type=jnp.float32)

    @pl.when(f == 0)
    def _():
        x = h_ref[...]
        r = lax.rsqrt(jnp.mean(x * x, axis=-1, keepdims=True) + EPS)
        u_ref[...] = (x * r * g_ref[...]).astype(u_ref.dtype)
        out_ref[...] = x + hidden()

    @pl.when(f > 0)
    def _():
        out_ref[...] += hidden()


def _mlp(h, g, w1, w2, layer, *, tm=512, tf=2048):
    n, d = h.shape
    dff = w1.shape[2]
    return pl.pallas_call(
        _mlp_kernel,
        grid=(n // tm, dff // tf),
        in_specs=[pl.BlockSpec((tm, d), lambda i, f: (i, 0)),
                  pl.BlockSpec((1, d), lambda i, f: (0, 0)),
                  pl.BlockSpec((None, d, tf), lambda i, f: (layer, 0, f)),
                  pl.BlockSpec((None, tf, d), lambda i, f: (layer, f, 0))],
        out_specs=pl.BlockSpec((tm, d), lambda i, f: (i, 0)),
        out_shape=jax.ShapeDtypeStruct((n, d), jnp.float32),
        scratch_shapes=[pltpu.VMEM((tm, d), jnp.bfloat16)],
        compiler_params=_params(("arbitrary", "arbitrary")),
        name="sq_relu_mlp",
    )(h, g.reshape(1, d), w1, w2)


_KB = _GROUP
_NBLK = _WIN // _KB
_ONES_ROWS = 16


def _band_kernel(q_ref, k_ref, v_ref, bias_ref, o_ref, vt_ref, s0_ref, s1_ref, p0_ref, p1_ref):
    ng = vt_ref.shape[1] // _GROUP
    f32 = jnp.float32
    s_refs, p_refs = (s0_ref, s1_ref), (p0_ref, p1_ref)

    for j in range(ng):
        vt_ref[0:HEAD_DIM, j * _KB:(j + 1) * _KB] = v_ref[0, j * _KB:(j + 1) * _KB, :].T
    sub = lax.broadcasted_iota(jnp.int32, (_ONES_ROWS, vt_ref.shape[1]), 0)
    vt_ref[HEAD_DIM:, :] = jnp.where(sub == 0, 1.0, 0.0).astype(vt_ref.dtype)

    def fold8(x, op):
        return op(x.reshape(x.shape[0] // 8, 8, x.shape[1]), axis=0)

    def rows_of(g):
        return slice(g * _GROUP, (g + 1) * _GROUP)

    def key_blocks(g):
        return [(blk, g - (_NBLK - 1) + blk) for blk in range(_NBLK) if g - (_NBLK - 1) + blk >= 0]

    def logits_block(g, blk, kb, s_out):
        s = lax.dot_general(k_ref[0, rows_of(kb), :], q_ref[0, rows_of(g), :], _NT,
                            preferred_element_type=f32) + bias_ref[0, blk * _KB:(blk + 1) * _KB, :]
        s_out[blk * _KB:(blk + 1) * _KB, :] = s
        return fold8(s, jnp.max)

    def values(g, p_ref):
        blks = key_blocks(g)
        (b0, k0), nb = blks[0], len(blks)
        return jnp.dot(vt_ref[:, k0 * _KB:(k0 + nb) * _KB], p_ref[b0 * _KB:(b0 + nb) * _KB, :],
                       preferred_element_type=f32)

    def write_out(g, pv):
        o = pv[0:HEAD_DIM] / pv[HEAD_DIM:HEAD_DIM + 1]
        o_ref[0, rows_of(g), :] = o.T.astype(o_ref.dtype)

    mx = None
    for blk, kb in key_blocks(0):
        pm = logits_block(0, blk, kb, s_refs[0])
        mx = pm if mx is None else jnp.maximum(mx, pm)
    m = jnp.max(mx, axis=0, keepdims=True)

    for g in range(ng):
        cur, nxt = g % 2, 1 - g % 2
        nxt_blocks = key_blocks(g + 1) if g + 1 < ng else []
        cur_blocks = key_blocks(g)
        mx = None
        for i in range(_NBLK):
            if i < len(nxt_blocks):
                blk, kb = nxt_blocks[i]
                pm = logits_block(g + 1, blk, kb, s_refs[nxt])
                mx = pm if mx is None else jnp.maximum(mx, pm)
            if i < len(cur_blocks):
                blk, _ = cur_blocks[i]
                e = (s_refs[cur][blk * _KB:(blk + 1) * _KB, :] - m).astype(p_refs[cur].dtype)
                p_refs[cur][blk * _KB:(blk + 1) * _KB, :] = jnp.exp2(e)
        if g > 0:
            write_out(g - 1, values(g - 1, p_refs[nxt]))
        if nxt_blocks:
            m = jnp.max(mx, axis=0, keepdims=True)

    write_out(ng - 1, values(ng - 1, p_refs[(ng - 1) % 2]))


def _band_attention(kvq, bias_t):
    bsz, s, d3 = kvq.shape
    d = d3 // 3
    nh = d // HEAD_DIM
    assert s % _GROUP == 0
    f32 = jnp.float32
    bf = jnp.bfloat16
    return pl.pallas_call(
        _band_kernel,
        grid=(bsz, nh),
        in_specs=[
            pl.BlockSpec((1, s, HEAD_DIM), lambda b, h: (b, 0, 2 * nh + h)),
            pl.BlockSpec((1, s, HEAD_DIM), lambda b, h: (b, 0, h)),
            pl.BlockSpec((1, s, HEAD_DIM), lambda b, h: (b, 0, nh + h)),
            pl.BlockSpec((1, _WIN, _GROUP), lambda b, h: (h, 0, 0)),
        ],
        out_specs=pl.BlockSpec((1, s, HEAD_DIM), lambda b, h: (b, 0, h)),
        out_shape=jax.ShapeDtypeStruct((bsz, s, d), bf),
        scratch_shapes=[pltpu.VMEM((HEAD_DIM + _ONES_ROWS, s), bf),
                        pltpu.VMEM((_WIN, _GROUP), f32),
                        pltpu.VMEM((_WIN, _GROUP), f32),
                        pltpu.VMEM((_WIN, _GROUP), bf),
                        pltpu.VMEM((_WIN, _GROUP), bf)],
        compiler_params=_params(("arbitrary", "arbitrary")),
        name="band_attention",
    )(kvq, kvq, kvq, bias_t)


_BIAS_SPAN = _WIN + _GROUP


def _band_bias_kernel(g_ref, o_ref):
    row = lax.broadcasted_iota(jnp.int32, (_KB, _BIAS_SPAN), 0)
    x = jnp.broadcast_to(g_ref[0], (_KB, _BIAS_SPAN))
    shift = 1
    while shift < _KB:
        x = jnp.where((row & shift) != 0, pltpu.roll(x, shift, axis=1), x)
        shift *= 2
    for blk in range(_NBLK):
        start = (_NBLK - blk) * _KB
        b = x[:, start:start + _GROUP]
        j = lax.broadcasted_iota(jnp.int32, (_KB, _GROUP), 0) + blk * _KB
        i = lax.broadcasted_iota(jnp.int32, (_KB, _GROUP), 1)
        km = j - (i // CHUNK) * CHUNK
        o_ref[0, blk * _KB:(blk + 1) * _KB, :] = jnp.where((km >= 0) & (km < _BAND), b, -jnp.inf)


def _band_bias(rel_table):
    nh = rel_table.shape[0]
    k = jnp.arange(_BIAS_SPAN)
    idx = jnp.clip(k - _WIN + _PAD, -(CHUNK - 1), REL_CLIP) + (CHUNK - 1)
    g = (rel_table[:, idx].astype(jnp.float32) * LOG2E).reshape(nh, 1, _BIAS_SPAN)
    return pl.pallas_call(
        _band_bias_kernel,
        grid=(nh,),
        in_specs=[pl.BlockSpec((1, 1, _BIAS_SPAN), lambda h: (h, 0, 0))],
        out_specs=pl.BlockSpec((1, _WIN, _GROUP), lambda h: (h, 0, 0)),
        out_shape=jax.ShapeDtypeStruct((nh, _WIN, _GROUP), jnp.float32),
        compiler_params=_params(("arbitrary",)),
        name="band_bias",
    )(g)


def kernel(x, a_norm_g, a_w_in, a_b_f, a_q_g, a_k_g, a_w_out, mlp_norm_g, mlp_w1, mlp_w2,
           kv_norm_g, kv_w, kv_k_g, b_norm_g, b_w_q, b_q_g, b_rel, b_w_out):
    bsz, s, d = x.shape
    n = bsz * s
    nh = N_HEADS
    bf = jnp.bfloat16
    q_scale = HEAD_DIM ** -0.5 * LOG2E

    h = x.reshape(n, d)

    w_qkv, w_f = _cast_split_cols(a_w_in, 0, 3 * d)
    w1_all = mlp_w1.astype(bf)
    w2_all = mlp_w2.astype(bf)
    qkv, fz = _proj(h, [a_norm_g[0]], [w_qkv], [a_q_g[0] * q_scale, a_k_g[0]],
                    [(0, 0), (0, 1), (0, None)], w_f)
    b_f = jnp.pad(a_b_f[0], (0, LANES - nh)).reshape(1, LANES)
    o = _fox_from_fz(qkv.reshape(bsz, s, 3 * d), fz.reshape(bsz, s, LANES), b_f)
    h = _out_proj(h, o.reshape(n, d), a_w_out[0].astype(bf))
    h = _mlp(h, mlp_norm_g[0], w1_all, w2_all, 0)

    kvq = _proj(h, [kv_norm_g, b_norm_g[0]], [kv_w.astype(bf), b_w_q[0].astype(bf)],
                [kv_k_g, b_q_g[0] * q_scale],
                [(0, 0), (0, None), (1, 1)])

    o = _band_attention(kvq.reshape(bsz, s, 3 * d), _band_bias(b_rel[0]))
    h = _out_proj(h, o.reshape(n, d), b_w_out[0].astype(bf))
    h = _mlp(h, mlp_norm_g[1], w1_all, w2_all, 1)
    return h.reshape(bsz, s, d)
```

```python
import functools
import math

import numpy as np

import jax
import jax.numpy as jnp
from jax import lax
from jax.experimental import pallas as pl
from jax.experimental.pallas import tpu as pltpu

N_HEADS = 16
HEAD_DIM = 128
CHUNK = 64
N_PREV_CHUNKS = 8
REL_CLIP = 256
EPS = 1e-6
LOG2E = math.log2(math.e)

LANES = 128
VMEM_LIMIT = 60 * 1024 * 1024

_BAND = (N_PREV_CHUNKS + 1) * CHUNK
_PAD = N_PREV_CHUNKS * CHUNK
_GROUP = 4 * CHUNK
_WIN = _PAD + _GROUP


def _params(sem):
    return pltpu.CompilerParams(dimension_semantics=sem, vmem_limit_bytes=VMEM_LIMIT)


def _cast_kernel(wt_ref, tail_ref, o_ref, otail_ref, *, n_tail):
    o_ref[...] = wt_ref[...].T.astype(o_ref.dtype)
    lane = lax.broadcasted_iota(jnp.int32, otail_ref.shape, 1)
    otail_ref[...] = jnp.where(lane < n_tail, tail_ref[...].T, 0.0).astype(otail_ref.dtype)


def _cast_split_cols(w, layer, ncols, *, tc=256):
    _, rows, total = w.shape
    n_tail = total - ncols
    assert ncols % tc == 0 and tc % LANES == 0 and 0 < n_tail < LANES
    wt = jnp.swapaxes(w, 1, 2)
    return pl.pallas_call(
        functools.partial(_cast_kernel, n_tail=n_tail),
        grid=(ncols // tc,),
        in_specs=[pl.BlockSpec((None, tc, rows), lambda i: (layer, i, 0)),
                  pl.BlockSpec((None, LANES, rows), lambda i: (layer, ncols // LANES, 0))],
        out_specs=[pl.BlockSpec((rows, tc), lambda i: (0, i)),
                   pl.BlockSpec((rows, LANES), lambda i: (0, 0))],
        out_shape=[jax.ShapeDtypeStruct((rows, ncols), jnp.bfloat16),
                   jax.ShapeDtypeStruct((rows, LANES), jnp.bfloat16)],
        compiler_params=_params(("arbitrary",)),
        name="cast_cols",
    )(wt, wt)


def _proj_kernel(*refs, parts, n_w, has_f, tn):
    x_ref, g_ref, gain_ref = refs[:3]
    w_refs = refs[3:3 + n_w]
    if has_f:
        wf_ref, o_ref, f_ref, u_ref = refs[3 + n_w:]
    else:
        o_ref, u_ref = refs[3 + n_w:]
    w_parts = [(w_ref, c) for w_ref in w_refs for c in range(w_ref.shape[1] // tn)]
    n_norms = u_ref.shape[0]
    x = x_ref[...]
    xn = x * lax.rsqrt(jnp.mean(x * x, axis=-1, keepdims=True) + EPS)
    for k in range(n_norms):
        u_ref[k] = (xn * g_ref[k:k + 1, :]).astype(u_ref.dtype)
    if has_f:
        f_ref[...] = jnp.dot(u_ref[0], wf_ref[...], preferred_element_type=jnp.float32)
    order = sorted(range(len(parts)), key=lambda j: parts[j][1] is None)
    for j in order:
        norm_idx, gain_idx = parts[j]
        w_ref, c = w_parts[j]
        y = jnp.dot(u_ref[norm_idx], w_ref[:, c * tn:(c + 1) * tn],
                    preferred_element_type=jnp.float32)
        if gain_idx is None:
            o_ref[:, j * tn:(j + 1) * tn] = y.astype(o_ref.dtype)
            continue
        gain = gain_ref[gain_idx:gain_idx + 1, :]
        for c in range(tn // HEAD_DIM):
            ys = y[:, c * HEAD_DIM:(c + 1) * HEAD_DIM]
            r = lax.rsqrt(jnp.mean(ys * ys, axis=-1, keepdims=True) + EPS)
            sl = slice(j * tn + c * HEAD_DIM, j * tn + (c + 1) * HEAD_DIM)
            o_ref[:, sl] = (ys * r * gain).astype(o_ref.dtype)


def _proj(x, norm_gains, ws, head_gains, parts, wf=None, *, tm=512, tn=2048):
    n, d = x.shape
    nout = sum(w.shape[1] for w in ws)
    assert len(parts) * tn == nout and all(w.shape[1] % tn == 0 for w in ws)
    has_f = wf is not None
    n_norms = len(norm_gains)
    once = pl.Buffered(1)
    in_specs = [
        pl.BlockSpec((tm, d), lambda i: (i, 0)),
        pl.BlockSpec((n_norms, d), lambda i: (0, 0)),
        pl.BlockSpec((len(head_gains), HEAD_DIM), lambda i: (0, 0)),
    ] + [pl.BlockSpec(w.shape, lambda i: (0, 0), pipeline_mode=once) for w in ws]
    args = [x, jnp.stack(norm_gains), jnp.stack(head_gains), *ws]
    out_shape = [jax.ShapeDtypeStruct((n, nout), jnp.bfloat16)]
    out_specs = [pl.BlockSpec((tm, nout), lambda i: (i, 0))]
    if has_f:
        in_specs.append(pl.BlockSpec((d, LANES), lambda i: (0, 0), pipeline_mode=once))
        args.append(wf)
        out_shape.append(jax.ShapeDtypeStruct((n, LANES), jnp.float32))
        out_specs.append(pl.BlockSpec((tm, LANES), lambda i: (i, 0)))
    res = pl.pallas_call(
        functools.partial(_proj_kernel, parts=tuple(parts), n_w=len(ws), has_f=has_f, tn=tn),
        grid=(n // tm,),
        in_specs=in_specs,
        out_specs=out_specs,
        out_shape=out_shape,
        scratch_shapes=[pltpu.VMEM((n_norms, tm, d), jnp.bfloat16)],
        compiler_params=_params(("arbitrary",)),
        name="norm_proj",
    )(*args)
    return res if has_f else res[0]


_HEAD_LANES = LANES // N_HEADS
_N_TERMS = 3


def _bias_column_constants():
    sel = np.zeros((_N_TERMS * LANES, 2 * LANES), np.float32)
    one = np.zeros((1, 2 * LANES), np.float32)
    for h in range(N_HEADS):
        for term in range(_N_TERMS):
            sel[term * LANES + h, h * _HEAD_LANES + term] = -1.0
            sel[term * LANES + h, LANES + h * _HEAD_LANES + _N_TERMS + term] = 1.0
            one[0, h * _HEAD_LANES + _N_TERMS + term] = 1.0
            one[0, LANES + h * _HEAD_LANES + term] = 1.0
    return jnp.asarray(sel, jnp.bfloat16), jnp.asarray(one, jnp.float32)


def _logf_cumsum_kernel(f_ref, b_ref, sel_ref, one_ref, qx_ref, kx_ref):
    z = f_ref[0] + b_ref[...]
    c = jnp.minimum(z, 0.0) - jnp.log1p(jnp.exp(-jnp.abs(z)))
    s = c.shape[0]
    row = lax.broadcasted_iota(jnp.int32, c.shape, 0)
    d = 1
    while d < s:
        c = c + jnp.where(row >= d, pltpu.roll(c, d, axis=0), 0.0)
        d *= 2
    c = c * LOG2E
    hi = c.astype(jnp.bfloat16)
    r = c - hi.astype(jnp.float32)
    mid = r.astype(jnp.bfloat16)
    lo = (r - mid.astype(jnp.float32)).astype(jnp.bfloat16)
    terms = jnp.concatenate([hi, mid, lo], axis=1)
    y = jnp.dot(terms, sel_ref[...], preferred_element_type=jnp.float32) + one_ref[...]
    kx_ref[0] = y[:, :LANES].astype(kx_ref.dtype)
    qx_ref[0] = y[:, LANES:].astype(qx_ref.dtype)


def _logf_cumsum(fz, b):
    bsz, s, _ = fz.shape
    sel, one = _bias_column_constants()
    spec = pl.BlockSpec((1, s, LANES), lambda i: (i, 0, 0))
    out = jax.ShapeDtypeStruct((bsz, s, LANES), jnp.bfloat16)
    return pl.pallas_call(
        _logf_cumsum_kernel,
        grid=(bsz,),
        in_specs=[spec, pl.BlockSpec((1, LANES), lambda i: (0, 0)),
                  pl.BlockSpec(sel.shape, lambda i: (0, 0)),
                  pl.BlockSpec(one.shape, lambda i: (0, 0))],
        out_specs=[spec, spec],
        out_shape=[out, out],
        compiler_params=_params(("arbitrary",)),
        name="logf_cumsum",
    )(fz, b, sel, one)


_NT = (((1,), (1,)), ((), ()))


def _fox_kernel(q_ref, qx_ref, k_ref, kx_ref, v_ref, o_ref,
                qa_ref, ka_ref, vt_ref, s0_ref, s1_ref, p0_ref, p1_ref, acc_ref, *, t, tc, rb):
    nt = ka_ref.shape[0]
    nc = t // tc
    nr = t // rb
    f32 = jnp.float32
    s_refs, p_refs = (s0_ref, s1_ref), (p0_ref, p1_ref)
    tiles = [(qi, kv) for qi in range(nt) for kv in range(qi + 1)]

    lane = lax.broadcasted_iota(jnp.int32, (t, LANES), 1)
    own_lanes = lane // _HEAD_LANES == pl.program_id(1)

    for j in range(nt):
        rows = slice(j * t, (j + 1) * t)
        ka_ref[j, :, 0:HEAD_DIM] = k_ref[0, rows, :]
        ka_ref[j, :, HEAD_DIM:] = kx_ref[0, rows, :]
        qa_ref[j, :, 0:HEAD_DIM] = q_ref[0, rows, :]
        qx = qx_ref[0, rows, :]
        qa_ref[j, :, HEAD_DIM:] = jnp.where(own_lanes, qx, jnp.zeros_like(qx))
        vt_ref[:, rows] = v_ref[0, rows, :].T

    def fold8(x, op):
        return op(x.reshape(x.shape[0] // 8, 8, x.shape[1]), axis=0)

    def live(tile, r, c):
        return tile[1] < tile[0] or r * rb < (c + 1) * tc

    def key_rows(tile, c):
        return t if tile[1] < tile[0] else min(t, -(-(c + 1) * tc // rb) * rb)

    def logits_block(tile, r, c, s_out):
        qi, kv = tile
        s = lax.dot_general(ka_ref[kv, r * rb:(r + 1) * rb, :],
                            qa_ref[qi, c * tc:(c + 1) * tc, :], _NT,
                            preferred_element_type=f32)
        if kv == qi and (r + 1) * rb > c * tc:
            key = lax.broadcasted_iota(jnp.int32, s.shape, 0) + r * rb
            qry = lax.broadcasted_iota(jnp.int32, s.shape, 1) + c * tc
            s = jnp.where(key <= qry, s, -jnp.inf)
        s_out[r * rb:(r + 1) * rb, c * tc:(c + 1) * tc] = s
        return fold8(s, jnp.max)

    blocks = [(r, c) for r in range(nr) for c in range(nc)]

    def tile_max(mx):
        return [jnp.max(mx[c], axis=0, keepdims=True) for c in range(nc)]

    mx = [None] * nc
    for r, c in blocks:
        if not live(tiles[0], r, c):
            continue
        pm = logits_block(tiles[0], r, c, s_refs[0])
        mx[c] = pm if mx[c] is None else jnp.maximum(mx[c], pm)
    mc = tile_max(mx)

    m_run = l_run = alpha_prev = None
    for n, (qi, kv) in enumerate(tiles):
        cur, nxt = n % 2, 1 - n % 2
        nxt_tile = tiles[n + 1] if n + 1 < len(tiles) else None
        prv_tile = tiles[n - 1] if n > 0 else None
        if kv == 0:
            m_new = mc
            alpha = None
        else:
            m_new = [jnp.maximum(m_run[c], mc[c]) for c in range(nc)]
            alpha = [jnp.exp2(m_run[c] - m_new[c]) for c in range(nc)]
        mx = [None] * nc
        ls = [None] * nc
        pv = [None] * nc
        for r, c in blocks:
            rows = slice(r * rb, (r + 1) * rb)
            cols = slice(c * tc, (c + 1) * tc)
            if nxt_tile is not None and live(nxt_tile, r, c):
                pm = logits_block(nxt_tile, r, c, s_refs[nxt])
                mx[c] = pm if mx[c] is None else jnp.maximum(mx[c], pm)
            if live((qi, kv), r, c):
                p = jnp.exp2(s_refs[cur][rows, cols] - m_new[c])
                ps = fold8(p, jnp.sum)
                ls[c] = ps if ls[c] is None else ls[c] + ps
                p_refs[cur][rows, cols] = p.astype(p_refs[cur].dtype)
            if prv_tile is not None and r == nr - 1:
                nk = key_rows(prv_tile, c)
                k0 = prv_tile[1] * t
                pv[c] = jnp.dot(vt_ref[:, k0:k0 + nk], p_refs[nxt][0:nk, cols],
                                preferred_element_type=f32)
        lsum = [jnp.sum(ls[c], axis=0, keepdims=True) for c in range(nc)]
        l_prev_block = l_run
        l_run = lsum if kv == 0 else [alpha[c] * l_run[c] + lsum[c] for c in range(nc)]
        if prv_tile is not None:
            for c in range(nc):
                cols = slice(c * tc, (c + 1) * tc)
                if prv_tile[1] == 0:
                    acc_ref[:, cols] = pv[c]
                else:
                    acc_ref[:, cols] = alpha_prev[c] * acc_ref[:, cols] + pv[c]
            if prv_tile[1] == prv_tile[0]:
                l_fin = jnp.concatenate(l_prev_block, axis=1)
                o_ref[0, prv_tile[0] * t:(prv_tile[0] + 1) * t, :] = (
                    acc_ref[...] / l_fin).T.astype(o_ref.dtype)
        m_run, alpha_prev = m_new, alpha
        if nxt_tile is not None:
            mc = tile_max(mx)

    last = len(tiles) - 1
    qi, kv = tiles[last]
    for c in range(nc):
        cols = slice(c * tc, (c + 1) * tc)
        nk = key_rows(tiles[last], c)
        d = jnp.dot(vt_ref[:, kv * t:kv * t + nk], p_refs[last % 2][0:nk, cols],
                    preferred_element_type=f32)
        acc_ref[:, cols] = d if kv == 0 else alpha_prev[c] * acc_ref[:, cols] + d
    o_ref[0, qi * t:(qi + 1) * t, :] = (
        acc_ref[...] / jnp.concatenate(l_run, axis=1)).T.astype(o_ref.dtype)


def _fox_attention(qkv, qx, kx, *, t=512, tc=256, rb=256):
    bsz, s, d3 = qkv.shape
    d = d3 // 3
    nh = d // HEAD_DIM
    nt = s // t
    f32 = jnp.float32
    bf = jnp.bfloat16
    return pl.pallas_call(
        functools.partial(_fox_kernel, t=t, tc=tc, rb=rb),
        grid=(bsz, nh),
        in_specs=[
            pl.BlockSpec((1, s, HEAD_DIM), lambda b, h: (b, 0, h)),
            pl.BlockSpec((1, s, LANES), lambda b, h: (b, 0, 0)),
            pl.BlockSpec((1, s, HEAD_DIM), lambda b, h: (b, 0, nh + h)),
            pl.BlockSpec((1, s, LANES), lambda b, h: (b, 0, 0)),
            pl.BlockSpec((1, s, HEAD_DIM), lambda b, h: (b, 0, 2 * nh + h)),
        ],
        out_specs=pl.BlockSpec((1, s, HEAD_DIM), lambda b, h: (b, 0, h)),
        out_shape=jax.ShapeDtypeStruct((bsz, s, d), bf),
        scratch_shapes=[pltpu.VMEM((nt, t, HEAD_DIM + LANES), bf),
                        pltpu.VMEM((nt, t, HEAD_DIM + LANES), bf),
                        pltpu.VMEM((HEAD_DIM, s), bf),
                        pltpu.VMEM((t, t), f32),
                        pltpu.VMEM((t, t), f32),
                        pltpu.VMEM((t, t), bf),
                        pltpu.VMEM((t, t), bf),
                        pltpu.VMEM((HEAD_DIM, t), f32)],
        compiler_params=_params(("arbitrary", "arbitrary")),
        name="fox_attention",
    )(qkv, qx, qkv, kx, qkv)


def _fox_from_fz(qkv, fz, b_f):
    qx, kx = _logf_cumsum(fz, b_f)
    return _fox_attention(qkv, qx, kx)


def _out_proj_kernel(h_ref, o_ref, w_ref, out_ref):
    out_ref[...] = h_ref[...] + jnp.dot(o_ref[...], w_ref[...],
                                        preferred_element_type=jnp.float32)


def _out_proj(h, o, w, *, tm=1024):
    n, d = h.shape
    return pl.pallas_call(
        _out_proj_kernel,
        grid=(n // tm,),
        in_specs=[pl.BlockSpec((tm, d), lambda i: (i, 0)),
                  pl.BlockSpec((tm, d), lambda i: (i, 0)),
                  pl.BlockSpec((d, d), lambda i: (0, 0),
                               pipeline_mode=pl.Buffered(1))],
        out_specs=pl.BlockSpec((tm, d), lambda i: (i, 0)),
        out_shape=jax.ShapeDtypeStruct((n, d), jnp.float32),
        compiler_params=_params(("arbitrary",)),
        name="out_proj",
    )(h, o, w)


def _mlp_kernel(h_ref, g_ref, w1_ref, w2_ref, out_ref, u_ref):
    f = pl.program_id(1)

    def hidden():
        a = jnp.dot(u_ref[...], w1_ref[...], preferred_element_type=jnp.float32)
        a = jnp.maximum(a, 0.0)
        a = (a * a).astype(w2_ref.dtype)
        return jnp.dot(a, w2_ref[...], preferred_element_type=jnp.float32)

    @pl.when(f == 0)
    def _():
        x = h_ref[...]
        r = lax.rsqrt(jnp.mean(x * x, axis=-1, keepdims=True) + EPS)
        u_ref[...] = (x * r * g_ref[...]).astype(u_ref.dtype)
        out_ref[...] = x + hidden()

    @pl.when(f > 0)
    def _():
        out_ref[...] += hidden()


def _mlp(h, g, w1, w2, layer, *, tm=512, tf=2048):
    n, d = h.shape
    dff = w1.shape[2]
    return pl.pallas_call(
        _mlp_kernel,
        grid=(n // tm, dff // tf),
        in_specs=[pl.BlockSpec((tm, d), lambda i, f: (i, 0)),
                  pl.BlockSpec((1, d), lambda i, f: (0, 0)),
                  pl.BlockSpec((None, d, tf), lambda i, f: (layer, 0, f)),
                  pl.BlockSpec((None, tf, d), lambda i, f: (layer, f, 0))],
        out_specs=pl.BlockSpec((tm, d), lambda i, f: (i, 0)),
        out_shape=jax.ShapeDtypeStruct((n, d), jnp.float32),
        scratch_shapes=[pltpu.VMEM((tm, d), jnp.bfloat16)],
        compiler_params=_params(("arbitrary", "arbitrary")),
        name="sq_relu_mlp",
    )(h, g.reshape(1, d), w1, w2)


_KB = _GROUP
_NBLK = _WIN // _KB
_ONES_ROWS = 16


def _band_kernel(q_ref, k_ref, v_ref, bias_ref, o_ref, vt_ref, s0_ref, s1_ref, p0_ref, p1_ref):
    ng = vt_ref.shape[1] // _GROUP
    f32 = jnp.float32
    s_refs, p_refs = (s0_ref, s1_ref), (p0_ref, p1_ref)

    for j in range(ng):
        vt_ref[0:HEAD_DIM, j * _KB:(j + 1) * _KB] = v_ref[0, j * _KB:(j + 1) * _KB, :].T
    sub = lax.broadcasted_iota(jnp.int32, (_ONES_ROWS, vt_ref.shape[1]), 0)
    vt_ref[HEAD_DIM:, :] = jnp.where(sub == 0, 1.0, 0.0).astype(vt_ref.dtype)

    def fold8(x, op):
        return op(x.reshape(x.shape[0] // 8, 8, x.shape[1]), axis=0)

    def rows_of(g):
        return slice(g * _GROUP, (g + 1) * _GROUP)

    def key_blocks(g):
        return [(blk, g - (_NBLK - 1) + blk) for blk in range(_NBLK) if g - (_NBLK - 1) + blk >= 0]

    def logits_block(g, blk, kb, s_out):
        s = lax.dot_general(k_ref[0, rows_of(kb), :], q_ref[0, rows_of(g), :], _NT,
                            preferred_element_type=f32) + bias_ref[0, blk * _KB:(blk + 1) * _KB, :]
        s_out[blk * _KB:(blk + 1) * _KB, :] = s
        return fold8(s, jnp.max)

    def values(g, p_ref):
        blks = key_blocks(g)
        (b0, k0), nb = blks[0], len(blks)
        return jnp.dot(vt_ref[:, k0 * _KB:(k0 + nb) * _KB], p_ref[b0 * _KB:(b0 + nb) * _KB, :],
                       preferred_element_type=f32)

    def write_out(g, pv):
        o = pv[0:HEAD_DIM] / pv[HEAD_DIM:HEAD_DIM + 1]
        o_ref[0, rows_of(g), :] = o.T.astype(o_ref.dtype)

    mx = None
    for blk, kb in key_blocks(0):
        pm = logits_block(0, blk, kb, s_refs[0])
        mx = pm if mx is None else jnp.maximum(mx, pm)
    m = jnp.max(mx, axis=0, keepdims=True)

    for g in range(ng):
        cur, nxt = g % 2, 1 - g % 2
        nxt_blocks = key_blocks(g + 1) if g + 1 < ng else []
        cur_blocks = key_blocks(g)
        mx = None
        for i in range(_NBLK):
            if i < len(nxt_blocks):
                blk, kb = nxt_blocks[i]
                pm = logits_block(g + 1, blk, kb, s_refs[nxt])
                mx = pm if mx is None else jnp.maximum(mx, pm)
            if i < len(cur_blocks):
                blk, _ = cur_blocks[i]
                e = (s_refs[cur][blk * _KB:(blk + 1) * _KB, :] - m).astype(p_refs[cur].dtype)
                p_refs[cur][blk * _KB:(blk + 1) * _KB, :] = jnp.exp2(e)
        if g > 0:
            write_out(g - 1, values(g - 1, p_refs[nxt]))
        if nxt_blocks:
            m = jnp.max(mx, axis=0, keepdims=True)

    write_out(ng - 1, values(ng - 1, p_refs[(ng - 1) % 2]))


def _band_attention(kvq, bias_t):
    bsz, s, d3 = kvq.shape
    d = d3 // 3
    nh = d // HEAD_DIM
    assert s % _GROUP == 0
    f32 = jnp.float32
    bf = jnp.bfloat16
    return pl.pallas_call(
        _band_kernel,
        grid=(bsz, nh),
        in_specs=[
            pl.BlockSpec((1, s, HEAD_DIM), lambda b, h: (b, 0, 2 * nh + h)),
            pl.BlockSpec((1, s, HEAD_DIM), lambda b, h: (b, 0, h)),
            pl.BlockSpec((1, s, HEAD_DIM), lambda b, h: (b, 0, nh + h)),
            pl.BlockSpec((1, _WIN, _GROUP), lambda b, h: (h, 0, 0)),
        ],
        out_specs=pl.BlockSpec((1, s, HEAD_DIM), lambda b, h: (b, 0, h)),
        out_shape=jax.ShapeDtypeStruct((bsz, s, d), bf),
        scratch_shapes=[pltpu.VMEM((HEAD_DIM + _ONES_ROWS, s), bf),
                        pltpu.VMEM((_WIN, _GROUP), f32),
                        pltpu.VMEM((_WIN, _GROUP), f32),
                        pltpu.VMEM((_WIN, _GROUP), bf),
                        pltpu.VMEM((_WIN, _GROUP), bf)],
        compiler_params=_params(("arbitrary", "arbitrary")),
        name="band_attention",
    )(kvq, kvq, kvq, bias_t)


_BIAS_SPAN = _WIN + _GROUP


def _band_bias_kernel(g_ref, o_ref):
    row = lax.broadcasted_iota(jnp.int32, (_KB, _BIAS_SPAN), 0)
    x = jnp.broadcast_to(g_ref[0], (_KB, _BIAS_SPAN))
    shift = 1
    while shift < _KB:
        x = jnp.where((row & shift) != 0, pltpu.roll(x, shift, axis=1), x)
        shift *= 2
    for blk in range(_NBLK):
        start = (_NBLK - blk) * _KB
        b = x[:, start:start + _GROUP]
        j = lax.broadcasted_iota(jnp.int32, (_KB, _GROUP), 0) + blk * _KB
        i = lax.broadcasted_iota(jnp.int32, (_KB, _GROUP), 1)
        km = j - (i // CHUNK) * CHUNK
        o_ref[0, blk * _KB:(blk + 1) * _KB, :] = jnp.where((km >= 0) & (km < _BAND), b, -jnp.inf)


def _band_bias(rel_table):
    nh = rel_table.shape[0]
    k = jnp.arange(_BIAS_SPAN)
    idx = jnp.clip(k - _WIN + _PAD, -(CHUNK - 1), REL_CLIP) + (CHUNK - 1)
    g = (rel_table[:, idx].astype(jnp.float32) * LOG2E).reshape(nh, 1, _BIAS_SPAN)
    return pl.pallas_call(
        _band_bias_kernel,
        grid=(nh,),
        in_specs=[pl.BlockSpec((1, 1, _BIAS_SPAN), lambda h: (h, 0, 0))],
        out_specs=pl.BlockSpec((1, _WIN, _GROUP), lambda h: (h, 0, 0)),
        out_shape=jax.ShapeDtypeStruct((nh, _WIN, _GROUP), jnp.float32),
        compiler_params=_params(("arbitrary",)),
        name="band_bias",
    )(g)


def kernel(x, a_norm_g, a_w_in, a_b_f, a_q_g, a_k_g, a_w_out, mlp_norm_g, mlp_w1, mlp_w2,
           kv_norm_g, kv_w, kv_k_g, b_norm_g, b_w_q, b_q_g, b_rel, b_w_out):
    bsz, s, d = x.shape
    n = bsz * s
    nh = N_HEADS
    bf = jnp.bfloat16
    q_scale = HEAD_DIM ** -0.5 * LOG2E

    h = x.reshape(n, d)

    w_qkv, w_f = _cast_split_cols(a_w_in, 0, 3 * d)
    w1_all = mlp_w1.astype(bf)
    w2_all = mlp_w2.astype(bf)
    qkv, fz = _proj(h, [a_norm_g[0]], [w_qkv], [a_q_g[0] * q_scale, a_k_g[0]],
                    [(0, 0), (0, 1), (0, None)], w_f)
    b_f = jnp.pad(a_b_f[0], (0, LANES - nh)).reshape(1, LANES)
    o = _fox_from_fz(qkv.reshape(bsz, s, 3 * d), fz.reshape(bsz, s, LANES), b_f)
    h = _out_proj(h, o.reshape(n, d), a_w_out[0].astype(bf))
    h = _mlp(h, mlp_norm_g[0], w1_all, w2_all, 0)

    kvq = _proj(h, [kv_norm_g, b_norm_g[0]], [kv_w.astype(bf), b_w_q[0].astype(bf)],
                [kv_k_g, b_q_g[0] * q_scale],
                [(0, 0), (0, None), (1, 1)])

    o = _band_attention(kvq.reshape(bsz, s, 3 * d), _band_bias(b_rel[0]))
    h = _out_proj(h, o.reshape(n, d), b_w_out[0].astype(bf))
    h = _mlp(h, mlp_norm_g[1], w1_all, w2_all, 1)
    return h.reshape(bsz, s, d)
```

```python
import functools
import math

import numpy as np

import jax
import jax.numpy as jnp
from jax import lax
from jax.experimental import pallas as pl
from jax.experimental.pallas import tpu as pltpu

N_HEADS = 16
HEAD_DIM = 128
CHUNK = 64
N_PREV_CHUNKS = 8
REL_CLIP = 256
EPS = 1e-6
LOG2E = math.log2(math.e)

LANES = 128
VMEM_LIMIT = 60 * 1024 * 1024

_BAND = (N_PREV_CHUNKS + 1) * CHUNK
_PAD = N_PREV_CHUNKS * CHUNK
_GROUP = 4 * CHUNK
_WIN = _PAD + _GROUP


def _params(sem):
    return pltpu.CompilerParams(dimension_semantics=sem, vmem_limit_bytes=VMEM_LIMIT)


def _cast_kernel(wt_ref, tail_ref, o_ref, otail_ref, *, n_tail):
    o_ref[...] = wt_ref[...].T.astype(o_ref.dtype)
    lane = lax.broadcasted_iota(jnp.int32, otail_ref.shape, 1)
    otail_ref[...] = jnp.where(lane < n_tail, tail_ref[...].T, 0.0).astype(otail_ref.dtype)


def _cast_split_cols(w, layer, ncols, *, tc=256):
    _, rows, total = w.shape
    n_tail = total - ncols
    assert ncols % tc == 0 and tc % LANES == 0 and 0 < n_tail < LANES
    wt = jnp.swapaxes(w, 1, 2)
    return pl.pallas_call(
        functools.partial(_cast_kernel, n_tail=n_tail),
        grid=(ncols // tc,),
        in_specs=[pl.BlockSpec((None, tc, rows), lambda i: (layer, i, 0)),
                  pl.BlockSpec((None, LANES, rows), lambda i: (layer, ncols // LANES, 0))],
        out_specs=[pl.BlockSpec((rows, tc), lambda i: (0, i)),
                   pl.BlockSpec((rows, LANES), lambda i: (0, 0))],
        out_shape=[jax.ShapeDtypeStruct((rows, ncols), jnp.bfloat16),
                   jax.ShapeDtypeStruct((rows, LANES), jnp.bfloat16)],
        compiler_params=_params(("arbitrary",)),
        name="cast_cols",
    )(wt, wt)


def _proj_kernel(*refs, parts, n_w, has_f, tn):
    x_ref, g_ref, gain_ref = refs[:3]
    w_refs = refs[3:3 + n_w]
    if has_f:
        wf_ref, o_ref, f_ref, u_ref = refs[3 + n_w:]
    else:
        o_ref, u_ref = refs[3 + n_w:]
    w_parts = [(w_ref, c) for w_ref in w_refs for c in range(w_ref.shape[1] // tn)]
    n_norms = u_ref.shape[0]
    x = x_ref[...]
    xn = x * lax.rsqrt(jnp.mean(x * x, axis=-1, keepdims=True) + EPS)
    for k in range(n_norms):
        u_ref[k] = (xn * g_ref[k:k + 1, :]).astype(u_ref.dtype)
    if has_f:
        f_ref[...] = jnp.dot(u_ref[0], wf_ref[...], preferred_element_type=jnp.float32)
    order = sorted(range(len(parts)), key=lambda j: parts[j][1] is None)
    for j in order:
        norm_idx, gain_idx = parts[j]
        w_ref, c = w_parts[j]
        y = jnp.dot(u_ref[norm_idx], w_ref[:, c * tn:(c + 1) * tn],
                    preferred_element_type=jnp.float32)
        if gain_idx is None:
            o_ref[:, j * tn:(j + 1) * tn] = y.astype(o_ref.dtype)
            continue
        gain = gain_ref[gain_idx:gain_idx + 1, :]
        for c in range(tn // HEAD_DIM):
            ys = y[:, c * HEAD_DIM:(c + 1) * HEAD_DIM]
            r = lax.rsqrt(jnp.mean(ys * ys, axis=-1, keepdims=True) + EPS)
            sl = slice(j * tn + c * HEAD_DIM, j * tn + (c + 1) * HEAD_DIM)
            o_ref[:, sl] = (ys * r * gain).astype(o_ref.dtype)


def _proj(x, norm_gains, ws, head_gains, parts, wf=None, *, tm=512, tn=2048):
    n, d = x.shape
    nout = sum(w.shape[1] for w in ws)
    assert len(parts) * tn == nout and all(w.shape[1] % tn == 0 for w in ws)
    has_f = wf is not None
    n_norms = len(norm_gains)
    once = pl.Buffered(1)
    in_specs = [
        pl.BlockSpec((tm, d), lambda i: (i, 0)),
        pl.BlockSpec((n_norms, d), lambda i: (0, 0)),
        pl.BlockSpec((len(head_gains), HEAD_DIM), lambda i: (0, 0)),
    ] + [pl.BlockSpec(w.shape, lambda i: (0, 0), pipeline_mode=once) for w in ws]
    args = [x, jnp.stack(norm_gains), jnp.stack(head_gains), *ws]
    out_shape = [jax.ShapeDtypeStruct((n, nout), jnp.bfloat16)]
    out_specs = [pl.BlockSpec((tm, nout), lambda i: (i, 0))]
    if has_f:
        in_specs.append(pl.BlockSpec((d, LANES), lambda i: (0, 0), pipeline_mode=once))
        args.append(wf)
        out_shape.append(jax.ShapeDtypeStruct((n, LANES), jnp.float32))
        out_specs.append(pl.BlockSpec((tm, LANES), lambda i: (i, 0)))
    res = pl.pallas_call(
        functools.partial(_proj_kernel, parts=tuple(parts), n_w=len(ws), has_f=has_f, tn=tn),
        grid=(n // tm,),
        in_specs=in_specs,
        out_specs=out_specs,
        out_shape=out_shape,
        scratch_shapes=[pltpu.VMEM((n_norms, tm, d), jnp.bfloat16)],
        compiler_params=_params(("arbitrary",)),
        name="norm_proj",
    )(*args)
    return res if has_f else res[0]


_HEAD_LANES = LANES // N_HEADS
_N_TERMS = 3


def _bias_column_constants():
    sel = np.zeros((_N_TERMS * LANES, 2 * LANES), np.float32)
    one = np.zeros((1, 2 * LANES), np.float32)
    for h in range(N_HEADS):
        for term in range(_N_TERMS):
            sel[term * LANES + h, h * _HEAD_LANES + term] = -1.0
            sel[term * LANES + h, LANES + h * _HEAD_LANES + _N_TERMS + term] = 1.0
            one[0, h * _HEAD_LANES + _N_TERMS + term] = 1.0
            one[0, LANES + h * _HEAD_LANES + term] = 1.0
    return jnp.asarray(sel, jnp.bfloat16), jnp.asarray(one, jnp.float32)


def _logf_cumsum_kernel(f_ref, b_ref, sel_ref, one_ref, qx_ref, kx_ref):
    z = f_ref[0] + b_ref[...]
    c = jnp.minimum(z, 0.0) - jnp.log1p(jnp.exp(-jnp.abs(z)))
    s = c.shape[0]
    row = lax.broadcasted_iota(jnp.int32, c.shape, 0)
    d = 1
    while d < s:
        c = c + jnp.where(row >= d, pltpu.roll(c, d, axis=0), 0.0)
        d *= 2
    c = c * LOG2E
    hi = c.astype(jnp.bfloat16)
    r = c - hi.astype(jnp.float32)
    mid = r.astype(jnp.bfloat16)
    lo = (r - mid.astype(jnp.float32)).astype(jnp.bfloat16)
    terms = jnp.concatenate([hi, mid, lo], axis=1)
    y = jnp.dot(terms, sel_ref[...], preferred_element_type=jnp.float32) + one_ref[...]
    kx_ref[0] = y[:, :LANES].astype(kx_ref.dtype)
    qx_ref[0] = y[:, LANES:].astype(qx_ref.dtype)


def _logf_cumsum(fz, b):
    bsz, s, _ = fz.shape
    sel, one = _bias_column_constants()
    spec = pl.BlockSpec((1, s, LANES), lambda i: (i, 0, 0))
    out = jax.ShapeDtypeStruct((bsz, s, LANES), jnp.bfloat16)
    return pl.pallas_call(
        _logf_cumsum_kernel,
        grid=(bsz,),
        in_specs=[spec, pl.BlockSpec((1, LANES), lambda i: (0, 0)),
                  pl.BlockSpec(sel.shape, lambda i: (0, 0)),
                  pl.BlockSpec(one.shape, lambda i: (0, 0))],
        out_specs=[spec, spec],
        out_shape=[out, out],
        compiler_params=_params(("arbitrary",)),
        name="logf_cumsum",
    )(fz, b, sel, one)


_NT = (((1,), (1,)), ((), ()))


def _fox_kernel(q_ref, qx_ref, k_ref, kx_ref, v_ref, o_ref,
                qa_ref, ka_ref, vt_ref, s0_ref, s1_ref, p0_ref, p1_ref, acc_ref, *, t, tc, rb):
    nt = ka_ref.shape[0]
    nc = t // tc
    nr = t // rb
    f32 = jnp.float32
    s_refs, p_refs = (s0_ref, s1_ref), (p0_ref, p1_ref)
    tiles = [(qi, kv) for qi in range(nt) for kv in range(qi + 1)]

    lane = lax.broadcasted_iota(jnp.int32, (t, LANES), 1)
    own_lanes = lane // _HEAD_LANES == pl.program_id(1)

    for j in range(nt):
        rows = slice(j * t, (j + 1) * t)
        ka_ref[j, :, 0:HEAD_DIM] = k_ref[0, rows, :]
        ka_ref[j, :, HEAD_DIM:] = kx_ref[0, rows, :]
        qa_ref[j, :, 0:HEAD_DIM] = q_ref[0, rows, :]
        qx = qx_ref[0, rows, :]
        qa_ref[j, :, HEAD_DIM:] = jnp.where(own_lanes, qx, jnp.zeros_like(qx))
        vt_ref[:, rows] = v_ref[0, rows, :].T

    def fold8(x, op):
        return op(x.reshape(x.shape[0] // 8, 8, x.shape[1]), axis=0)

    def live(tile, r, c):
        return tile[1] < tile[0] or r * rb < (c + 1) * tc

    def key_rows(tile, c):
        return t if tile[1] < tile[0] else min(t, -(-(c + 1) * tc // rb) * rb)

    def logits_block(tile, r, c, s_out):
        qi, kv = tile
        s = lax.dot_general(ka_ref[kv, r * rb:(r + 1) * rb, :],
                            qa_ref[qi, c * tc:(c + 1) * tc, :], _NT,
                            preferred_element_type=f32)
        if kv == qi and (r + 1) * rb > c * tc:
            key = lax.broadcasted_iota(jnp.int32, s.shape, 0) + r * rb
            qry = lax.broadcasted_iota(jnp.int32, s.shape, 1) + c * tc
            s = jnp.where(key <= qry, s, -jnp.inf)
        s_out[r * rb:(r + 1) * rb, c * tc:(c + 1) * tc] = s
        return fold8(s, jnp.max)

    blocks = [(r, c) for r in range(nr) for c in range(nc)]

    def tile_max(mx):
        return [jnp.max(mx[c], axis=0, keepdims=True) for c in range(nc)]

    mx = [None] * nc
    for r, c in blocks:
        if not live(tiles[0], r, c):
            continue
        pm = logits_block(tiles[0], r, c, s_refs[0])
        mx[c] = pm if mx[c] is None else jnp.maximum(mx[c], pm)
    mc = tile_max(mx)

    m_run = l_run = alpha_prev = None
    for n, (qi, kv) in enumerate(tiles):
        cur, nxt = n % 2, 1 - n % 2
        nxt_tile = tiles[n + 1] if n + 1 < len(tiles) else None
        prv_tile = tiles[n - 1] if n > 0 else None
        if kv == 0:
            m_new = mc
            alpha = None
        else:
            m_new = [jnp.maximum(m_run[c], mc[c]) for c in range(nc)]
            alpha = [jnp.exp2(m_run[c] - m_new[c]) for c in range(nc)]
        mx = [None] * nc
        ls = [None] * nc
        pv = [None] * nc
        for r, c in blocks:
            rows = slice(r * rb, (r + 1) * rb)
            cols = slice(c * tc, (c + 1) * tc)
            if nxt_tile is not None and live(nxt_tile, r, c):
                pm = logits_block(nxt_tile, r, c, s_refs[nxt])
                mx[c] = pm if mx[c] is None else jnp.maximum(mx[c], pm)
            if live((qi, kv), r, c):
                for hb in range(2):
                    hrows = slice(r * rb + hb * (rb // 2), r * rb + (hb + 1) * (rb // 2))
                    p = jnp.exp2(s_refs[cur][hrows, cols] - m_new[c])
                    ps = fold8(p, jnp.sum)
                    ls[c] = ps if ls[c] is None else ls[c] + ps
                    p_refs[cur][hrows, cols] = p.astype(p_refs[cur].dtype)
            if prv_tile is not None and r == nr - 1:
                nk = key_rows(prv_tile, c)
                k0 = prv_tile[1] * t
                pv[c] = jnp.dot(vt_ref[:, k0:k0 + nk], p_refs[nxt][0:nk, cols],
                                preferred_element_type=f32)
        lsum = [jnp.sum(ls[c], axis=0, keepdims=True) for c in range(nc)]
        l_prev_block = l_run
        l_run = lsum if kv == 0 else [alpha[c] * l_run[c] + lsum[c] for c in range(nc)]
        if prv_tile is not None:
            for c in range(nc):
                cols = slice(c * tc, (c + 1) * tc)
                if prv_tile[1] == 0:
                    acc_ref[:, cols] = pv[c]
                else:
                    acc_ref[:, cols] = alpha_prev[c] * acc_ref[:, cols] + pv[c]
            if prv_tile[1] == prv_tile[0]:
                l_fin = jnp.concatenate(l_prev_block, axis=1)
                o_ref[0, prv_tile[0] * t:(prv_tile[0] + 1) * t, :] = (
                    acc_ref[...] / l_fin).T.astype(o_ref.dtype)
        m_run, alpha_prev = m_new, alpha
        if nxt_tile is not None:
            mc = tile_max(mx)

    last = len(tiles) - 1
    qi, kv = tiles[last]
    for c in range(nc):
        cols = slice(c * tc, (c + 1) * tc)
        nk = key_rows(tiles[last], c)
        d = jnp.dot(vt_ref[:, kv * t:kv * t + nk], p_refs[last % 2][0:nk, cols],
                    preferred_element_type=f32)
        acc_ref[:, cols] = d if kv == 0 else alpha_prev[c] * acc_ref[:, cols] + d
    o_ref[0, qi * t:(qi + 1) * t, :] = (
        acc_ref[...] / jnp.concatenate(l_run, axis=1)).T.astype(o_ref.dtype)


def _fox_attention(qkv, qx, kx, *, t=512, tc=256, rb=256):
    bsz, s, d3 = qkv.shape
    d = d3 // 3
    nh = d // HEAD_DIM
    nt = s // t
    f32 = jnp.float32
    bf = jnp.bfloat16
    return pl.pallas_call(
        functools.partial(_fox_kernel, t=t, tc=tc, rb=rb),
        grid=(bsz, nh),
        in_specs=[
            pl.BlockSpec((1, s, HEAD_DIM), lambda b, h: (b, 0, h)),
            pl.BlockSpec((1, s, LANES), lambda b, h: (b, 0, 0)),
            pl.BlockSpec((1, s, HEAD_DIM), lambda b, h: (b, 0, nh + h)),
            pl.BlockSpec((1, s, LANES), lambda b, h: (b, 0, 0)),
            pl.BlockSpec((1, s, HEAD_DIM), lambda b, h: (b, 0, 2 * nh + h)),
        ],
        out_specs=pl.BlockSpec((1, s, HEAD_DIM), lambda b, h: (b, 0, h)),
        out_shape=jax.ShapeDtypeStruct((bsz, s, d), bf),
        scratch_shapes=[pltpu.VMEM((nt, t, HEAD_DIM + LANES), bf),
                        pltpu.VMEM((nt, t, HEAD_DIM + LANES), bf),
                        pltpu.VMEM((HEAD_DIM, s), bf),
                        pltpu.VMEM((t, t), f32),
                        pltpu.VMEM((t, t), f32),
                        pltpu.VMEM((t, t), bf),
                        pltpu.VMEM((t, t), bf),
                        pltpu.VMEM((HEAD_DIM, t), f32)],
        compiler_params=_params(("arbitrary", "arbitrary")),
        name="fox_attention",
    )(qkv, qx, qkv, kx, qkv)


def _fox_from_fz(qkv, fz, b_f):
    qx, kx = _logf_cumsum(fz, b_f)
    return _fox_attention(qkv, qx, kx)


def _out_proj_kernel(h_ref, o_ref, w_ref, out_ref):
    out_ref[...] = h_ref[...] + jnp.dot(o_ref[...], w_ref[...],
                                        preferred_element_type=jnp.float32)


def _out_proj(h, o, w, *, tm=1024):
    n, d = h.shape
    return pl.pallas_call(
        _out_proj_kernel,
        grid=(n // tm,),
        in_specs=[pl.BlockSpec((tm, d), lambda i: (i, 0)),
                  pl.BlockSpec((tm, d), lambda i: (i, 0)),
                  pl.BlockSpec((d, d), lambda i: (0, 0),
                               pipeline_mode=pl.Buffered(1))],
        out_specs=pl.BlockSpec((tm, d), lambda i: (i, 0)),
        out_shape=jax.ShapeDtypeStruct((n, d), jnp.float32),
        compiler_params=_params(("arbitrary",)),
        name="out_proj",
    )(h, o, w)


def _mlp_kernel(h_ref, g_ref, w1_ref, w2_ref, out_ref, u_ref):
    f = pl.program_id(1)

    def hidden():
        a = jnp.dot(u_ref[...], w1_ref[...], preferred_element_type=jnp.float32)
        a = jnp.maximum(a, 0.0)
        a = (a * a).astype(w2_ref.dtype)
        return jnp.dot(a, w2_ref[...], preferred_element_type=jnp.float32)

    @pl.when(f == 0)
    def _():
        x = h_ref[...]
        r = lax.rsqrt(jnp.mean(x * x, axis=-1, keepdims=True) + EPS)
        u_ref[...] = (x * r * g_ref[...]).astype(u_ref.dtype)
        out_ref[...] = x + hidden()

    @pl.when(f > 0)
    def _():
        out_ref[...] += hidden()


def _mlp(h, g, w1, w2, layer, *, tm=512, tf=2048):
    n, d = h.shape
    dff = w1.shape[2]
    return pl.pallas_call(
        _mlp_kernel,
        grid=(n // tm, dff // tf),
        in_specs=[pl.BlockSpec((tm, d), lambda i, f: (i, 0)),
                  pl.BlockSpec((1, d), lambda i, f: (0, 0)),
                  pl.BlockSpec((None, d, tf), lambda i, f: (layer, 0, f)),
                  pl.BlockSpec((None, tf, d), lambda i, f: (layer, f, 0))],
        out_specs=pl.BlockSpec((tm, d), lambda i, f: (i, 0)),
        out_shape=jax.ShapeDtypeStruct((n, d), jnp.float32),
        scratch_shapes=[pltpu.VMEM((tm, d), jnp.bfloat16)],
        compiler_params=_params(("arbitrary", "arbitrary")),
        name="sq_relu_mlp",
    )(h, g.reshape(1, d), w1, w2)


_KB = _GROUP
_NBLK = _WIN // _KB
_ONES_ROWS = 16


def _band_kernel(q_ref, k_ref, v_ref, bias_ref, o_ref, vt_ref, s0_ref, s1_ref, p0_ref, p1_ref):
    ng = vt_ref.shape[1] // _GROUP
    f32 = jnp.float32
    s_refs, p_refs = (s0_ref, s1_ref), (p0_ref, p1_ref)

    for j in range(ng):
        vt_ref[0:HEAD_DIM, j * _KB:(j + 1) * _KB] = v_ref[0, j * _KB:(j + 1) * _KB, :].T
    sub = lax.broadcasted_iota(jnp.int32, (_ONES_ROWS, vt_ref.shape[1]), 0)
    vt_ref[HEAD_DIM:, :] = jnp.where(sub == 0, 1.0, 0.0).astype(vt_ref.dtype)

    def fold8(x, op):
        return op(x.reshape(x.shape[0] // 8, 8, x.shape[1]), axis=0)

    def rows_of(g):
        return slice(g * _GROUP, (g + 1) * _GROUP)

    def key_blocks(g):
        return [(blk, g - (_NBLK - 1) + blk) for blk in range(_NBLK) if g - (_NBLK - 1) + blk >= 0]

    def logits_block(g, blk, kb, s_out):
        s = lax.dot_general(k_ref[0, rows_of(kb), :], q_ref[0, rows_of(g), :], _NT,
                            preferred_element_type=f32) + bias_ref[0, blk * _KB:(blk + 1) * _KB, :]
        s_out[blk * _KB:(blk + 1) * _KB, :] = s
        return fold8(s, jnp.max)

    def values(g, p_ref):
        blks = key_blocks(g)
        (b0, k0), nb = blks[0], len(blks)
        return jnp.dot(vt_ref[:, k0 * _KB:(k0 + nb) * _KB], p_ref[b0 * _KB:(b0 + nb) * _KB, :],
                       preferred_element_type=f32)

    def write_out(g, pv):
        o = pv[0:HEAD_DIM] / pv[HEAD_DIM:HEAD_DIM + 1]
        o_ref[0, rows_of(g), :] = o.T.astype(o_ref.dtype)

    mx = None
    for blk, kb in key_blocks(0):
        pm = logits_block(0, blk, kb, s_refs[0])
        mx = pm if mx is None else jnp.maximum(mx, pm)
    m = jnp.max(mx, axis=0, keepdims=True)

    for g in range(ng):
        cur, nxt = g % 2, 1 - g % 2
        nxt_blocks = key_blocks(g + 1) if g + 1 < ng else []
        cur_blocks = key_blocks(g)
        mx = None
        for i in range(_NBLK):
            if i < len(nxt_blocks):
                blk, kb = nxt_blocks[i]
                pm = logits_block(g + 1, blk, kb, s_refs[nxt])
                mx = pm if mx is None else jnp.maximum(mx, pm)
            if i < len(cur_blocks):
                blk, _ = cur_blocks[i]
                for hb in range(2):
                    hr = slice(blk * _KB + hb * (_KB // 2), blk * _KB + (hb + 1) * (_KB // 2))
                    e = (s_refs[cur][hr, :] - m).astype(p_refs[cur].dtype)
                    p_refs[cur][hr, :] = jnp.exp2(e)
        if g > 0:
            write_out(g - 1, values(g - 1, p_refs[nxt]))
        if nxt_blocks:
            m = jnp.max(mx, axis=0, keepdims=True)

    write_out(ng - 1, values(ng - 1, p_refs[(ng - 1) % 2]))


def _band_attention(kvq, bias_t):
    bsz, s, d3 = kvq.shape
    d = d3 // 3
    nh = d // HEAD_DIM
    assert s % _GROUP == 0
    f32 = jnp.float32
    bf = jnp.bfloat16
    return pl.pallas_call(
        _band_kernel,
        grid=(bsz, nh),
        in_specs=[
            pl.BlockSpec((1, s, HEAD_DIM), lambda b, h: (b, 0, 2 * nh + h)),
            pl.BlockSpec((1, s, HEAD_DIM), lambda b, h: (b, 0, h)),
            pl.BlockSpec((1, s, HEAD_DIM), lambda b, h: (b, 0, nh + h)),
            pl.BlockSpec((1, _WIN, _GROUP), lambda b, h: (h, 0, 0)),
        ],
        out_specs=pl.BlockSpec((1, s, HEAD_DIM), lambda b, h: (b, 0, h)),
        out_shape=jax.ShapeDtypeStruct((bsz, s, d), bf),
        scratch_shapes=[pltpu.VMEM((HEAD_DIM + _ONES_ROWS, s), bf),
                        pltpu.VMEM((_WIN, _GROUP), f32),
                        pltpu.VMEM((_WIN, _GROUP), f32),
                        pltpu.VMEM((_WIN, _GROUP), bf),
                        pltpu.VMEM((_WIN, _GROUP), bf)],
        compiler_params=_params(("arbitrary", "arbitrary")),
        name="band_attention",
    )(kvq, kvq, kvq, bias_t)


_BIAS_SPAN = _WIN + _GROUP


def _band_bias_kernel(g_ref, o_ref):
    row = lax.broadcasted_iota(jnp.int32, (_KB, _BIAS_SPAN), 0)
    x = jnp.broadcast_to(g_ref[0], (_KB, _BIAS_SPAN))
    shift = 1
    while shift < _KB:
        x = jnp.where((row & shift) != 0, pltpu.roll(x, shift, axis=1), x)
        shift *= 2
    for blk in range(_NBLK):
        start = (_NBLK - blk) * _KB
        b = x[:, start:start + _GROUP]
        j = lax.broadcasted_iota(jnp.int32, (_KB, _GROUP), 0) + blk * _KB
        i = lax.broadcasted_iota(jnp.int32, (_KB, _GROUP), 1)
        km = j - (i // CHUNK) * CHUNK
        o_ref[0, blk * _KB:(blk + 1) * _KB, :] = jnp.where((km >= 0) & (km < _BAND), b, -jnp.inf)


def _band_bias(rel_table):
    nh = rel_table.shape[0]
    k = jnp.arange(_BIAS_SPAN)
    idx = jnp.clip(k - _WIN + _PAD, -(CHUNK - 1), REL_CLIP) + (CHUNK - 1)
    g = (rel_table[:, idx].astype(jnp.float32) * LOG2E).reshape(nh, 1, _BIAS_SPAN)
    return pl.pallas_call(
        _band_bias_kernel,
        grid=(nh,),
        in_specs=[pl.BlockSpec((1, 1, _BIAS_SPAN), lambda h: (h, 0, 0))],
        out_specs=pl.BlockSpec((1, _WIN, _GROUP), lambda h: (h, 0, 0)),
        out_shape=jax.ShapeDtypeStruct((nh, _WIN, _GROUP), jnp.float32),
        compiler_params=_params(("arbitrary",)),
        name="band_bias",
    )(g)


def kernel(x, a_norm_g, a_w_in, a_b_f, a_q_g, a_k_g, a_w_out, mlp_norm_g, mlp_w1, mlp_w2,
           kv_norm_g, kv_w, kv_k_g, b_norm_g, b_w_q, b_q_g, b_rel, b_w_out):
    bsz, s, d = x.shape
    n = bsz * s
    nh = N_HEADS
    bf = jnp.bfloat16
    q_scale = HEAD_DIM ** -0.5 * LOG2E

    h = x.reshape(n, d)

    w_qkv, w_f = _cast_split_cols(a_w_in, 0, 3 * d)
    w1_all = mlp_w1.astype(bf)
    w2_all = mlp_w2.astype(bf)
    qkv, fz = _proj(h, [a_norm_g[0]], [w_qkv], [a_q_g[0] * q_scale, a_k_g[0]],
                    [(0, 0), (0, 1), (0, None)], w_f)
    b_f = jnp.pad(a_b_f[0], (0, LANES - nh)).reshape(1, LANES)
    o = _fox_from_fz(qkv.reshape(bsz, s, 3 * d), fz.reshape(bsz, s, LANES), b_f)
    h = _out_proj(h, o.reshape(n, d), a_w_out[0].astype(bf))
    h = _mlp(h, mlp_norm_g[0], w1_all, w2_all, 0)

    kvq = _proj(h, [kv_norm_g, b_norm_g[0]], [kv_w.astype(bf), b_w_q[0].astype(bf)],
                [kv_k_g, b_q_g[0] * q_scale],
                [(0, 0), (0, None), (1, 1)])

    o = _band_attention(kvq.reshape(bsz, s, 3 * d), _band_bias(b_rel[0]))
    h = _out_proj(h, o.reshape(n, d), b_w_out[0].astype(bf))
    h = _mlp(h, mlp_norm_g[1], w1_all, w2_all, 1)
    return h.reshape(bsz, s, d)
```
